```python
import math
import jax, jax.numpy as jnp
from jax import lax
import numpy as np

D_MODEL = 2048
BATCH = 2
SEQ = 8192
DEPTH = 4

D_SSD = D_MODEL // 2
D_ATT = D_MODEL // 4
D_SC = D_MODEL // 4
D_MIX = D_SSD + D_ATT + D_SC

SSD_HEAD_DIM = 64
SSD_HEADS = D_SSD // SSD_HEAD_DIM
SSD_GROUPS = 2
SSD_HEADS_PER_GROUP = SSD_HEADS // SSD_GROUPS
SSD_STATE = 128
SSD_CONV = 4
SSD_CHUNK = 128
SSD_CONV_DIM = D_SSD + 2 * SSD_GROUPS * SSD_STATE

ATT_HEAD_DIM = 64
ATT_HEADS = D_ATT // ATT_HEAD_DIM
IDX_HEADS = 8
IDX_DIM = 64
TOPK_MAX = 256
Q_BLOCK = 128
REL_BUCKETS = 32
REL_MAX_DIST = 128

SC_GROUPS = 8
SC_CONV = 3

D_FF = 5632
FFN_CONV = 3

NORM_EPS = 1e-6

IN_SIZES = (D_SSD, SSD_CONV_DIM, SSD_HEADS,
            D_ATT, D_ATT, D_ATT,
            IDX_HEADS * IDX_DIM, IDX_DIM, IDX_HEADS,
            D_SC, D_SC, D_SC)
D_IN_PROJ = (D_SSD + SSD_CONV_DIM + SSD_HEADS + 3 * D_ATT
             + IDX_HEADS * IDX_DIM + IDX_DIM + IDX_HEADS + 3 * D_SC)

kernel_name = "hybrid_ssd_dsa_shortconv_block"


def rmsnorm(x, g):
    xf = x.astype(jnp.float32)
    y = xf * lax.rsqrt(jnp.mean(xf * xf, axis=-1, keepdims=True) + NORM_EPS)
    return (y * g.astype(jnp.float32)).astype(x.dtype)


def causal_dwconv(x, w):
    k, c = w.shape
    return lax.conv_general_dilated(
        x, w[:, None, :].astype(x.dtype), window_strides=(1,), padding=[(k - 1, 0)],
        dimension_numbers=('NWC', 'WIO', 'NWC'), feature_group_count=c)


def t5_causal_bucket(dist):
    n = jnp.maximum(dist, 0)
    max_exact = REL_BUCKETS // 2
    large = max_exact + (jnp.log(jnp.maximum(n, max_exact).astype(jnp.float32) / max_exact)
                         / math.log(REL_MAX_DIST / max_exact)
                         * (REL_BUCKETS - max_exact)).astype(jnp.int32)
    large = jnp.minimum(large, REL_BUCKETS - 1)
    return jnp.where(n < max_exact, n, large)


def ssd_chunked_scan(xh, dt, a_head, bm, cm):
    bsz, t = xh.shape[:2]
    nc = t // SSD_CHUNK

    def chunk(a):
        return a.reshape(bsz, nc, SSD_CHUNK, *a.shape[2:])

    x_c = chunk(xh.astype(jnp.float32) * dt[..., None])
    b_c = chunk(bm.astype(jnp.float32))
    c_c = chunk(cm.astype(jnp.float32))
    a_cs = jnp.cumsum(chunk(dt * a_head), axis=2)

    seg = a_cs[:, :, :, None] - a_cs[:, :, None, :]
    tril = jnp.tril(jnp.ones((SSD_CHUNK, SSD_CHUNK), dtype=bool))[:, :, None, None]
    lmat = jnp.exp(jnp.where(tril, seg, -jnp.inf))
    cb = jnp.einsum('bclgn,bcsgn->bclsg', c_c, b_c)
    y_diag = jnp.einsum('bclsg,bclsgr,bcsgrp->bclgrp', cb, lmat, x_c)

    decay_to_end = jnp.exp(a_cs[:, :, -1:] - a_cs)
    chunk_states = jnp.einsum('bclgn,bclgr,bclgrp->bcgrpn', b_c, decay_to_end, x_c)
    chunk_decay = jnp.exp(a_cs[:, :, -1])

    def step(h, inp):
        s_c, d_c = inp
        return h * d_c[..., None, None] + s_c, h

    h0 = jnp.zeros((bsz, SSD_GROUPS, SSD_HEADS_PER_GROUP, SSD_HEAD_DIM, SSD_STATE), jnp.float32)
    _, h_prev = lax.scan(step, h0, (jnp.moveaxis(chunk_states, 1, 0), jnp.moveaxis(chunk_decay, 1, 0)))
    h_prev = jnp.moveaxis(h_prev, 0, 1)
    y_off = jnp.einsum('bclgn,bcgrpn,bclgr->bclgrp', c_c, h_prev, jnp.exp(a_cs))
    return (y_diag + y_off).reshape(bsz, t, SSD_GROUPS, SSD_HEADS_PER_GROUP, SSD_HEAD_DIM)


def ssd_mixer(z, xbc, dt_raw, conv_w, conv_b, dt_bias, a_log, d_skip, norm_g):
    bsz, t, _ = z.shape
    xbc = jax.nn.silu(causal_dwconv(xbc, conv_w) + conv_b.astype(xbc.dtype))
    xs, bm, cm = jnp.split(xbc, [D_SSD, D_SSD + SSD_GROUPS * SSD_STATE], axis=-1)
    xh = xs.reshape(bsz, t, SSD_GROUPS, SSD_HEADS_PER_GROUP, SSD_HEAD_DIM)
    bm = bm.reshape(bsz, t, SSD_GROUPS, SSD_STATE)
    cm = cm.reshape(bsz, t, SSD_GROUPS, SSD_STATE)
    dt = jax.nn.softplus(dt_raw.astype(jnp.float32) + dt_bias.astype(jnp.float32))
    dt = dt.reshape(bsz, t, SSD_GROUPS, SSD_HEADS_PER_GROUP)
    a_head = -jnp.exp(a_log.astype(jnp.float32)).reshape(SSD_GROUPS, SSD_HEADS_PER_GROUP)
    y = ssd_chunked_scan(xh, dt, a_head, bm, cm)
    y = y + xh.astype(jnp.float32) * d_skip.astype(jnp.float32).reshape(SSD_GROUPS, SSD_HEADS_PER_GROUP, 1)
    y = y.reshape(bsz, t, D_SSD) * jax.nn.silu(z.astype(jnp.float32))
    y = rmsnorm(y.reshape(bsz, t, SSD_GROUPS, D_SSD // SSD_GROUPS),
                norm_g.reshape(SSD_GROUPS, D_SSD // SSD_GROUPS))
    return y.reshape(bsz, t, D_SSD).astype(z.dtype)


def dsa_mixer(q, k, v, q_idx, k_idx, w_idx, q_norm, k_norm, rel_bias):
    bsz, t, _ = q.shape
    q = rmsnorm(q.reshape(bsz, t, ATT_HEADS, ATT_HEAD_DIM), q_norm)
    k = rmsnorm(k.reshape(bsz, t, ATT_HEADS, ATT_HEAD_DIM), k_norm)
    v = v.reshape(bsz, t, ATT_HEADS, ATT_HEAD_DIM)
    qi = q_idx.reshape(bsz, t, IDX_HEADS, IDX_DIM)
    top_k = min(TOPK_MAX, t // 4)
    nb = t // Q_BLOCK
    scale = ATT_HEAD_DIM ** -0.5
    key_pos = jnp.arange(t, dtype=jnp.int32)
    gather = jax.vmap(lambda a, i: a[i])

    def to_blocks(a):
        return jnp.moveaxis(a.reshape(bsz, nb, Q_BLOCK, *a.shape[2:]), 1, 0)

    def block(args):
        start, qb, qib, wb = args
        qpos = start + jnp.arange(Q_BLOCK, dtype=jnp.int32)
        rel = jax.nn.relu(jnp.einsum('bqhd,bsd->bqsh', qib, k_idx).astype(jnp.float32))
        score = jnp.einsum('bqsh,bqh->bqs', rel, wb.astype(jnp.float32))
        causal = key_pos[None, :] <= qpos[:, None]
        score = jnp.where(causal[None], score, -jnp.inf)
        _, sel = lax.top_k(score, top_k)
        k_sel = gather(k, sel)
        v_sel = gather(v, sel)
        dist = qpos[None, :, None] - sel
        valid = dist >= 0
        bias = rel_bias[t5_causal_bucket(dist)].astype(jnp.float32)
        logits = (jnp.einsum('bqhd,bqkhd->bhqk', qb, k_sel).astype(jnp.float32) * scale
                  + jnp.moveaxis(bias, -1, 1))
        logits = jnp.where(valid[:, None], logits, -jnp.inf)
        p = jax.nn.softmax(logits, axis=-1).astype(v.dtype)
        return jnp.einsum('bhqk,bqkhd->bqhd', p, v_sel)

    starts = jnp.arange(nb, dtype=jnp.int32) * Q_BLOCK
    out = lax.map(block, (starts, to_blocks(q), to_blocks(qi), to_blocks(w_idx)))
    return jnp.moveaxis(out, 0, 1).reshape(bsz, t, D_ATT)


def setup_inputs(seed: int = 0) -> dict:
    key = jax.random.key(seed)
    ks = jax.random.split(key, 20)

    def nrm(k, shape, s):
        return jax.random.normal(k, shape, jnp.float32) * s

    x = nrm(ks[0], (BATCH, SEQ, D_MODEL), 1.0)
    norm_mix = 1.0 + nrm(ks[1], (DEPTH, D_MODEL), 0.02)
    w_in = nrm(ks[2], (DEPTH, D_MODEL, D_IN_PROJ), D_MODEL ** -0.5)
    ssd_conv_w = nrm(ks[3], (DEPTH, SSD_CONV, SSD_CONV_DIM), SSD_CONV ** -0.5)
    ssd_conv_b = nrm(ks[4], (DEPTH, SSD_CONV_DIM), 0.02)
    dt0 = jnp.exp(jax.random.uniform(ks[5], (DEPTH, SSD_HEADS), jnp.float32,
                                     minval=math.log(1e-3), maxval=math.log(1e-1)))
    ssd_dt_bias = dt0 + jnp.log(-jnp.expm1(-dt0))
    ssd_a_log = jnp.log(jax.random.uniform(ks[6], (DEPTH, SSD_HEADS), jnp.float32, minval=1.0, maxval=16.0))
    ssd_d = 1.0 + nrm(ks[7], (DEPTH, SSD_HEADS), 0.1)
    ssd_norm = 1.0 + nrm(ks[8], (DEPTH, D_SSD), 0.02)
    att_q_norm = 1.0 + nrm(ks[9], (DEPTH, ATT_HEAD_DIM), 0.02)
    att_k_norm = 1.0 + nrm(ks[10], (DEPTH, ATT_HEAD_DIM), 0.02)
    rel_bias = nrm(ks[11], (REL_BUCKETS, ATT_HEADS), 0.5)
    sc_conv_w = nrm(ks[12], (DEPTH, SC_CONV, D_SC), SC_CONV ** -0.5)
    w_out = nrm(ks[13], (DEPTH, D_MIX, D_MODEL), D_MIX ** -0.5)
    norm_ffn = 1.0 + nrm(ks[14], (DEPTH, D_MODEL), 0.02)
    ffn_w_gate = nrm(ks[15], (DEPTH, D_MODEL, D_FF), D_MODEL ** -0.5)
    ffn_w_up = nrm(ks[16], (DEPTH, D_MODEL, D_FF), D_MODEL ** -0.5)
    ffn_conv_w = nrm(ks[17], (DEPTH, FFN_CONV, D_FF), FFN_CONV ** -0.5)
    ffn_w_down = nrm(ks[18], (DEPTH, D_FF, D_MODEL), D_FF ** -0.5)
    return {"x": x, "norm_mix": norm_mix, "w_in": w_in,
            "ssd_conv_w": ssd_conv_w, "ssd_conv_b": ssd_conv_b, "ssd_dt_bias": ssd_dt_bias,
            "ssd_a_log": ssd_a_log, "ssd_d": ssd_d, "ssd_norm": ssd_norm,
            "att_q_norm": att_q_norm, "att_k_norm": att_k_norm, "rel_bias": rel_bias,
            "sc_conv_w": sc_conv_w, "w_out": w_out, "norm_ffn": norm_ffn,
            "ffn_w_gate": ffn_w_gate, "ffn_w_up": ffn_w_up, "ffn_conv_w": ffn_conv_w,
            "ffn_w_down": ffn_w_down}


def reference(x, norm_mix, w_in, ssd_conv_w, ssd_conv_b, ssd_dt_bias, ssd_a_log, ssd_d, ssd_norm,
              att_q_norm, att_k_norm, rel_bias, sc_conv_w, w_out, norm_ffn,
              ffn_w_gate, ffn_w_up, ffn_conv_w, ffn_w_down):
    split_points = np.cumsum(np.array(IN_SIZES))[:-1].tolist()
    for l in range(DEPTH):
        h = rmsnorm(x, norm_mix[l])
        proj = h @ w_in[l]
        (z, xbc, dt_raw, q, k, v, q_idx, k_idx, w_idx,
         sc_b, sc_c, sc_h) = jnp.split(proj, split_points, axis=-1)
        y_ssd = ssd_mixer(z, xbc, dt_raw, ssd_conv_w[l], ssd_conv_b[l], ssd_dt_bias[l],
                          ssd_a_log[l], ssd_d[l], ssd_norm[l])
        y_att = dsa_mixer(q, k, v, q_idx, k_idx, w_idx, att_q_norm[l], att_k_norm[l], rel_bias)
        y_sc = sc_b * causal_dwconv(sc_c * sc_h, sc_conv_w[l])
        x = x + jnp.concatenate([y_ssd, y_att, y_sc], axis=-1) @ w_out[l]
        h2 = rmsnorm(x, norm_ffn[l])
        g = causal_dwconv(h2 @ ffn_w_gate[l], ffn_conv_w[l])
        x = x + (jax.nn.silu(g) * (h2 @ ffn_w_up[l])) @ ffn_w_down[l]
    return x
```

```python
import functools
import math

import jax
import jax.numpy as jnp
from jax import lax
from jax.experimental import pallas as pl
from jax.experimental.pallas import tpu as pltpu

F32 = jnp.float32
BF16 = jnp.bfloat16
HIGHEST = lax.Precision.HIGHEST

LANES = 128
SUBLANES = 8
VMEM_LIMIT_BYTES = 56 * 1024 * 1024

D_MODEL = 2048
D_SSD = 1024
D_ATT = 512
D_SC = 512
SSD_HEAD_DIM = 64
SSD_HEADS = 16
SSD_GROUPS = 2
SSD_STATE = 128
SSD_CONV = 4
SSD_CHUNK = 128
SSD_CONV_DIM = D_SSD + 2 * SSD_GROUPS * SSD_STATE
ATT_HEAD_DIM = 64
ATT_HEADS = 8
IDX_HEADS = 8
IDX_DIM = 64
TOPK_MAX = 256
REL_BUCKETS = 32
REL_MAX_DIST = 128
SC_CONV = 3
D_FF = 5632
FFN_CONV = 3
NORM_EPS = 1e-6

COL_XBC = 0
COL_Q = 1536
COL_Z = 2048
COL_K = 3072
COL_V = 3584
COL_QIDX = 4096
COL_SCB = 4608
COL_SCC = 5120
COL_SCH = 5632
COL_DT = 6144
COL_KIDX = 6272
COL_WIDX = 6400
D_PACKED = 6656

NEG_BIG = -1e30
INT_MIN = -2147483648


def _cparams(*sem):
    return pltpu.CompilerParams(dimension_semantics=sem, vmem_limit_bytes=VMEM_LIMIT_BYTES)


def _dot(a, b):
    return jnp.dot(a, b, preferred_element_type=F32)


def _dot_nt(a, b):
    return lax.dot_general(a, b, (((1,), (1,)), ((), ())), preferred_element_type=F32)


def _dot_exact(a, b):
    return jnp.dot(a, b, preferred_element_type=F32, precision=HIGHEST)


def _sigmoid(x):
    return 1.0 / (1.0 + jnp.exp(-x))


def _shift_rows(x, prev8, s):
    xr = pltpu.roll(x, s, 0)
    pr = pltpu.roll(prev8, s, 0)
    rows = lax.broadcasted_iota(jnp.int32, (SUBLANES, x.shape[1]), 0)
    top = jnp.where(rows < s, pr, xr[:SUBLANES])
    return jnp.concatenate([top, xr[SUBLANES:]], axis=0)


def _causal_conv(x, prev8, w):
    k = w.shape[0]
    y = w[k - 1:k] * x
    for s in range(1, k):
        y = y + w[k - 1 - s:k - s] * _shift_rows(x, prev8, s)
    return y


def _rms_matmul_kernel(x_ref, g_ref, w_ref, o_ref, h_ref):
    @pl.when(pl.program_id(1) == 0)
    def _():
        x = x_ref[...]
        ms = jnp.mean(x * x, axis=-1, keepdims=True)
        h_ref[...] = (x * lax.rsqrt(ms + NORM_EPS) * g_ref[...]).astype(BF16)

    o_ref[...] = _dot(h_ref[...], w_ref[...])


def _rms_matmul(x, g, w, *, tm, tn):
    n, d = x.shape
    dout = w.shape[1]
    return pl.pallas_call(
        _rms_matmul_kernel,
        out_shape=jax.ShapeDtypeStruct((n, dout), F32),
        grid=(n // tm, dout // tn),
        in_specs=[
            pl.BlockSpec((tm, d), lambda i, j: (i, 0)),
            pl.BlockSpec((1, d), lambda i, j: (0, 0)),
            pl.BlockSpec((d, tn), lambda i, j: (0, j)),
        ],
        out_specs=pl.BlockSpec((tm, tn), lambda i, j: (i, j)),
        scratch_shapes=[pltpu.VMEM((tm, d), BF16)],
        compiler_params=_cparams("arbitrary", "arbitrary"),
        name="rms_in_proj",
    )(x, g, w)


def _ssd_kernel(xbc_ref, halo_ref, z_ref, dt_ref, cw_ref, cb_ref, dtb_ref, alog_ref,
                dtbx_ref, alogx_ref, dx_ref, ng_ref, ex_ref, o_ref, h_ref):
    c = pl.program_id(1)
    L = SSD_CHUNK

    @pl.when(c == 0)
    def _():
        h_ref[...] = jnp.zeros(h_ref.shape, F32)

    prev = jnp.where(c == 0, 0.0, halo_ref[...])
    xbc = _causal_conv(xbc_ref[...], prev, cw_ref[...]) + cb_ref[...]
    xbc = xbc * _sigmoid(xbc)
    xs = xbc[:, :D_SSD]

    row = lax.broadcasted_iota(jnp.int32, (L, L), 0)
    col = lax.broadcasted_iota(jnp.int32, (L, L), 1)
    tril = (row >= col).astype(F32)

    def softplus(v):
        return jnp.maximum(v, 0.0) + jnp.log1p(jnp.exp(-jnp.abs(v)))

    dt_s = softplus(dt_ref[...] + dtb_ref[...])
    a_s = dt_s * (-jnp.exp(alog_ref[...]))
    acs_s = _dot_exact(tril, a_s)
    acs_t = acs_s.T

    dt_x = softplus(_dot_exact(dt_ref[...], ex_ref[...]) + dtbx_ref[...])
    a_x = dt_x * (-jnp.exp(alogx_ref[...]))
    acs_x = _dot_exact(tril, a_x)
    last = acs_x[L - 1:L, :]
    xdt = xs * dt_x
    xdt_end = (xdt * jnp.exp(last - acs_x)).astype(BF16)
    xdt = xdt.astype(BF16)
    e_x = jnp.exp(acs_x)
    decay = jnp.exp(last)

    lane = lax.broadcasted_iota(jnp.int32, (L, LANES), 1)
    lo = lane < SSD_HEAD_DIM
    lower = row >= col

    for g in range(SSD_GROUPS):
        bm = xbc[:, D_SSD + g * SSD_STATE:D_SSD + (g + 1) * SSD_STATE].astype(BF16)
        cm = xbc[:, D_SSD + (SSD_GROUPS + g) * SSD_STATE:
                 D_SSD + (SSD_GROUPS + g + 1) * SSD_STATE].astype(BF16)
        cb = _dot_nt(cm, bm)
        bm_t = bm.T
        for q in range(4):
            pair = g * 4 + q
            sl = slice(pair * LANES, (pair + 1) * LANES)
            gs = []
            for hh in range(2):
                h = 2 * pair + hh
                seg = (jnp.broadcast_to(acs_s[:, h:h + 1], (L, L))
                       - jnp.broadcast_to(acs_t[h:h + 1, :], (L, L)))
                lm = jnp.where(lower, jnp.exp(seg), 0.0)
                gs.append((cb * lm).astype(BF16))
            gpair = jnp.concatenate(gs, axis=1)
            xp = xdt[:, sl]
            zero = jnp.zeros_like(xp)
            x2 = jnp.concatenate([jnp.where(lo, xp, zero), jnp.where(lo, zero, xp)], axis=0)
            y = _dot(gpair, x2)
            hprev = h_ref[pair]
            y = y + e_x[:, sl] * _dot(cm, hprev.astype(BF16))
            y = y + xs[:, sl] * dx_ref[:, sl]
            h_ref[pair] = hprev * decay[:, sl] + _dot(bm_t, xdt_end[:, sl])
            o_ref[:, sl] = y

    z = z_ref[...]
    y = o_ref[...] * (z * _sigmoid(z))
    half = D_SSD // SSD_GROUPS
    outs = []
    for g in range(SSD_GROUPS):
        yg = y[:, g * half:(g + 1) * half]
        ms = jnp.mean(yg * yg, axis=-1, keepdims=True)
        outs.append(yg * lax.rsqrt(ms + NORM_EPS))
    o_ref[...] = jnp.concatenate(outs, axis=1) * ng_ref[...]


def _ssd(proj, conv_w, conv_b, dt_bias, a_log, d_skip, norm_g, *, bsz, seq):
    nc = seq // SSD_CHUNK
    n = bsz * seq
    pad = LANES - SSD_HEADS
    dtb = jnp.pad(dt_bias, (0, pad)).reshape(1, LANES)
    alog = jnp.pad(a_log, (0, pad)).reshape(1, LANES)
    rep = lambda v: jnp.repeat(v, SSD_HEAD_DIM).reshape(1, D_SSD)
    expand = (jnp.arange(LANES)[:, None] == (jnp.arange(D_SSD)[None, :] // SSD_HEAD_DIM)).astype(F32)
    rows_per_halo = SSD_CHUNK // SUBLANES
    const = lambda shape: pl.BlockSpec(shape, lambda b, c: (0,) * len(shape))
    return pl.pallas_call(
        _ssd_kernel,
        out_shape=jax.ShapeDtypeStruct((n, D_SSD), F32),
        grid=(bsz, nc),
        in_specs=[
            pl.BlockSpec((SSD_CHUNK, SSD_CONV_DIM), lambda b, c: (b * nc + c, COL_XBC // SSD_CONV_DIM)),
            pl.BlockSpec((SUBLANES, SSD_CONV_DIM),
                         lambda b, c: (jnp.maximum((b * nc + c) * rows_per_halo - 1, 0), 0)),
            pl.BlockSpec((SSD_CHUNK, D_SSD), lambda b, c: (b * nc + c, COL_Z // D_SSD)),
            pl.BlockSpec((SSD_CHUNK, LANES), lambda b, c: (b * nc + c, COL_DT // LANES)),
            const((SSD_CONV, SSD_CONV_DIM)),
            const((1, SSD_CONV_DIM)),
            const((1, LANES)),
            const((1, LANES)),
            const((1, D_SSD)),
            const((1, D_SSD)),
            const((1, D_SSD)),
            const((1, D_SSD)),
            const((LANES, D_SSD)),
        ],
        out_specs=pl.BlockSpec((SSD_CHUNK, D_SSD), lambda b, c: (b * nc + c, 0)),
        scratch_shapes=[pltpu.VMEM((SSD_HEADS // 2, SSD_STATE, LANES), F32)],
        compiler_params=_cparams("arbitrary", "arbitrary"),
        name="ssd_scan",
    )(proj, proj, proj, proj, conv_w, conv_b.reshape(1, -1), dtb, alog,
      rep(dt_bias), rep(a_log), rep(d_skip), norm_g.reshape(1, -1), expand)


def _dsa_prep_kernel(q_ref, k_ref, v_ref, qi_ref, ki_ref, qg_ref, kg_ref, seg_ref,
                     qn_ref, kn_ref, vb_ref, qib_ref, kib_ref):
    def head_norm(x, g):
        ms = _dot_exact(x * x, seg_ref[...]) * (1.0 / ATT_HEAD_DIM)
        return x * lax.rsqrt(ms + NORM_EPS) * g

    qn_ref[...] = (head_norm(q_ref[...], qg_ref[...]) * (ATT_HEAD_DIM ** -0.5)).astype(BF16)
    kn_ref[...] = head_norm(k_ref[...], kg_ref[...]).astype(BF16)
    vb_ref[...] = v_ref[...].astype(BF16)
    qib_ref[...] = qi_ref[...].astype(BF16)
    kib_ref[...] = ki_ref[...].astype(BF16)


def _dsa_prep(proj, q_norm, k_norm, *, tm):
    n = proj.shape[0]
    seg = (jnp.arange(D_ATT)[:, None] // ATT_HEAD_DIM
           == jnp.arange(D_ATT)[None, :] // ATT_HEAD_DIM).astype(F32)
    tile = lambda v: jnp.tile(v, ATT_HEADS).reshape(1, D_ATT)
    blk = lambda col, w: pl.BlockSpec((tm, w), lambda i: (i, col // w))
    const = lambda shape: pl.BlockSpec(shape, lambda i: (0,) * len(shape))
    return pl.pallas_call(
        _dsa_prep_kernel,
        out_shape=(
            jax.ShapeDtypeStruct((n, D_ATT), BF16),
            jax.ShapeDtypeStruct((n, D_ATT), BF16),
            jax.ShapeDtypeStruct((n, D_ATT), BF16),
            jax.ShapeDtypeStruct((n, D_ATT), BF16),
            jax.ShapeDtypeStruct((n, LANES), BF16),
        ),
        grid=(n // tm,),
        in_specs=[blk(COL_Q, D_ATT), blk(COL_K, D_ATT), blk(COL_V, D_ATT), blk(COL_QIDX, D_ATT),
                  blk(COL_KIDX, LANES), const((1, D_ATT)), const((1, D_ATT)), const((D_ATT, D_ATT))],
        out_specs=(
            pl.BlockSpec((tm, D_ATT), lambda i: (i, 0)),
            pl.BlockSpec((tm, D_ATT), lambda i: (i, 0)),
            pl.BlockSpec((tm, D_ATT), lambda i: (i, 0)),
            pl.BlockSpec((tm, D_ATT), lambda i: (i, 0)),
            pl.BlockSpec((tm, LANES), lambda i: (i, 0)),
        ),
        compiler_params=_cparams("arbitrary"),
        name="dsa_prep",
    )(proj, proj, proj, proj, proj, tile(q_norm), tile(k_norm), seg)


def _dsa_kernel(qn_ref, qi_ref, w_ref, kn_ref, v_ref, ki_ref, bias_ref, o_ref,
                key_ref, qm_ref, qim_ref, wb_ref, m_ref, l_ref, acc_ref, *, blk, topk):
    i = pl.program_id(1)
    nrep = blk // LANES
    lane = lax.broadcasted_iota(jnp.int32, (blk, LANES), 1)
    lo = lane < ATT_HEAD_DIM
    row = lax.broadcasted_iota(jnp.int32, (blk, blk), 0)
    col = lax.broadcasted_iota(jnp.int32, (blk, blk), 1)
    future = col > row

    for p in range(ATT_HEADS // 2):
        sl = slice(p * LANES, (p + 1) * LANES)
        qp = qn_ref[:, sl]
        qip = qi_ref[:, sl]
        zero = jnp.zeros_like(qp)
        qm_ref[2 * p] = jnp.where(lo, qp, zero)
        qm_ref[2 * p + 1] = jnp.where(lo, zero, qp)
        qim_ref[2 * p] = jnp.where(lo, qip, zero)
        qim_ref[2 * p + 1] = jnp.where(lo, zero, qip)
    w = w_ref[...]
    for h in range(IDX_HEADS):
        wb_ref[h] = jnp.broadcast_to(w[:, h:h + 1], (blk, LANES))

    def score_tile(j, diag):
        kk = ki_ref[pl.ds(pl.multiple_of(j * blk, blk), blk), :]
        s = jnp.zeros((blk, blk), F32)
        for h in range(IDX_HEADS):
            d = _dot_nt(qim_ref[h], kk)
            s = s + jnp.maximum(d, 0.0) * pltpu.repeat(wb_ref[h], nrep, 1)
        if diag:
            s = jnp.where(future, -jnp.inf, s)
        s = jnp.where(s == 0.0, 0.0, s)
        bits = pltpu.bitcast(s, jnp.int32)
        key_ref[j] = bits ^ ((bits >> 31) & 0x7FFFFFFF)

    def score_body(j, carry):
        score_tile(j, False)
        return carry

    lax.fori_loop(0, i, score_body, 0)
    score_tile(i, True)

    def count_ge(cand):
        def body(j, cnt):
            kt = key_ref[j]
            for t in range(nrep):
                cnt = cnt + jnp.where(kt[:, t * LANES:(t + 1) * LANES] >= cand, 1.0, 0.0)
            return cnt
        cnt = lax.fori_loop(0, i + 1, body, jnp.zeros((blk, LANES), F32))
        return jnp.sum(cnt, axis=1, keepdims=True)

    def bit_body(b, prefix):
        cand = prefix + lax.shift_left(jnp.int32(1), 31 - b)
        return jnp.where(count_ge(cand) >= topk, cand, prefix)

    thr = lax.fori_loop(0, 32, bit_body, jnp.full((blk, LANES), INT_MIN, jnp.int32))

    m_ref[...] = jnp.full(m_ref.shape, NEG_BIG, F32)
    l_ref[...] = jnp.zeros(l_ref.shape, F32)
    acc_ref[...] = jnp.zeros(acc_ref.shape, F32)

    def attn_tile(j, kind):
        kt = key_ref[j]
        mb = jnp.where(kt >= pltpu.repeat(thr, nrep, 1), 0.0, NEG_BIG)
        if kind == 2:
            mb = jnp.where(future, NEG_BIG, mb)
        start = pl.multiple_of(j * blk, blk)
        for h in range(ATT_HEADS):
            sl = slice((h // 2) * LANES, (h // 2 + 1) * LANES)
            s = _dot_nt(qm_ref[h], kn_ref[pl.ds(start, blk), sl]) + mb
            if kind > 0:
                s = s + bias_ref[kind - 1, h]
            m_prev = m_ref[h]
            m_new = jnp.maximum(m_prev, jnp.max(s, axis=1, keepdims=True))
            alpha = jnp.exp(m_prev - m_new)
            p = jnp.exp(s - pltpu.repeat(m_new, nrep, 1))
            l_ref[h] = alpha * l_ref[h] + jnp.sum(p, axis=1, keepdims=True)
            acc_ref[h] = alpha * acc_ref[h] + _dot(p.astype(BF16), v_ref[pl.ds(start, blk), sl])
            m_ref[h] = m_new

    def attn_body(j, carry):
        attn_tile(j, 0)
        return carry

    lax.fori_loop(0, i - 1, attn_body, 0)

    @pl.when(i > 0)
    def _():
        attn_tile(i - 1, 1)

    attn_tile(i, 2)

    for p in range(ATT_HEADS // 2):
        a = acc_ref[2 * p] / l_ref[2 * p]
        b = acc_ref[2 * p + 1] / l_ref[2 * p + 1]
        o_ref[:, p * LANES:(p + 1) * LANES] = jnp.where(lo, a, b).astype(BF16)


def _t5_bucket(dist):
    n = jnp.maximum(dist, 0)
    max_exact = REL_BUCKETS // 2
    large = max_exact + (jnp.log(jnp.maximum(n, max_exact).astype(F32) / max_exact)
                         / math.log(REL_MAX_DIST / max_exact)
                         * (REL_BUCKETS - max_exact)).astype(jnp.int32)
    large = jnp.minimum(large, REL_BUCKETS - 1)
    return jnp.where(n < max_exact, n, large)


def _bias_tiles(rel_bias, blk):
    tab = rel_bias[_t5_bucket(jnp.arange(2 * blk, dtype=jnp.int32))] - rel_bias[REL_BUCKETS - 1]
    r = jnp.arange(blk)[:, None]
    c = jnp.arange(blk)[None, :]
    sub = tab[blk + r - c]
    diag = tab[jnp.maximum(r - c, 0)]
    return jnp.stack([sub, diag]).transpose(0, 3, 1, 2)


def _dsa(qn, qi, proj, kn, vb, ki, bias, *, bsz, seq, blk):
    n = bsz * seq
    nb = seq // blk
    topk = min(TOPK_MAX, seq // 4)
    once = pl.Buffered(1)
    return pl.pallas_call(
        functools.partial(_dsa_kernel, blk=blk, topk=topk),
        out_shape=jax.ShapeDtypeStruct((n, D_ATT), BF16),
        grid=(bsz, nb),
        in_specs=[
            pl.BlockSpec((blk, D_ATT), lambda b, i: (b * nb + i, 0)),
            pl.BlockSpec((blk, D_ATT), lambda b, i: (b * nb + i, 0)),
            pl.BlockSpec((blk, LANES), lambda b, i: (b * nb + i, COL_WIDX // LANES)),
            pl.BlockSpec((seq, D_ATT), lambda b, i: (b, 0), pipeline_mode=once),
            pl.BlockSpec((seq, D_ATT), lambda b, i: (b, 0), pipeline_mode=once),
            pl.BlockSpec((seq, LANES), lambda b, i: (b, 0), pipeline_mode=once),
            pl.BlockSpec((2, ATT_HEADS, blk, blk), lambda b, i: (0, 0, 0, 0), pipeline_mode=once),
        ],
        out_specs=pl.BlockSpec((blk, D_ATT), lambda b, i: (b * nb + i, 0)),
        scratch_shapes=[
            pltpu.VMEM((nb, blk, blk), jnp.int32),
            pltpu.VMEM((ATT_HEADS, blk, LANES), BF16),
            pltpu.VMEM((IDX_HEADS, blk, LANES), BF16),
            pltpu.VMEM((IDX_HEADS, blk, LANES), F32),
            pltpu.VMEM((ATT_HEADS, blk, LANES), F32),
            pltpu.VMEM((ATT_HEADS, blk, LANES), F32),
            pltpu.VMEM((ATT_HEADS, blk, LANES), F32),
        ],
        compiler_params=_cparams("arbitrary", "arbitrary"),
        name="dsa_attention",
    )(qn, qi, proj, kn, vb, ki, bias)


def _outproj_kernel(x_ref, yssd_ref, yatt_ref, scb_ref, scc_ref, sch_ref, hc_ref, hh_ref,
                    cw_ref, w_ref, o_ref, cat_ref, *, blocks_per_seq):
    i = pl.program_id(0)

    @pl.when(pl.program_id(1) == 0)
    def _():
        prev = jnp.where(i % blocks_per_seq == 0, 0.0, hc_ref[...] * hh_ref[...])
        ysc = scb_ref[...] * _causal_conv(scc_ref[...] * sch_ref[...], prev, cw_ref[...])
        cat_ref[:, :D_SSD] = yssd_ref[...].astype(BF16)
        cat_ref[:, D_SSD:D_SSD + D_ATT] = yatt_ref[...]
        cat_ref[:, D_SSD + D_ATT:] = ysc.astype(BF16)

    o_ref[...] = x_ref[...] + _dot(cat_ref[...], w_ref[...])


def _outproj(x, y_ssd, y_att, proj, sc_w, w_out, *, seq, tm, tn):
    n, d = x.shape
    halo = tm // SUBLANES
    blk = lambda col: pl.BlockSpec((tm, D_SC), lambda i, j: (i, col // D_SC))
    hblk = lambda col: pl.BlockSpec(
        (SUBLANES, D_SC), lambda i, j: (jnp.maximum(i * halo - 1, 0), col // D_SC))
    return pl.pallas_call(
        functools.partial(_outproj_kernel, blocks_per_seq=seq // tm),
        out_shape=jax.ShapeDtypeStruct((n, d), F32),
        grid=(n // tm, d // tn),
        in_specs=[
            pl.BlockSpec((tm, tn), lambda i, j: (i, j)),
            pl.BlockSpec((tm, D_SSD), lambda i, j: (i, 0)),
            pl.BlockSpec((tm, D_ATT), lambda i, j: (i, 0)),
            blk(COL_SCB), blk(COL_SCC), blk(COL_SCH), hblk(COL_SCC), hblk(COL_SCH),
            pl.BlockSpec((SC_CONV, D_SC), lambda i, j: (0, 0)),
            pl.BlockSpec((d, tn), lambda i, j: (0, j)),
        ],
        out_specs=pl.BlockSpec((tm, tn), lambda i, j: (i, j)),
        scratch_shapes=[pltpu.VMEM((tm, d), BF16)],
        compiler_params=_cparams("arbitrary", "arbitrary"),
        name="out_proj",
    )(x, y_ssd, y_att, proj, proj, proj, proj, proj, sc_w, w_out)


def _ffn_kernel(x_ref, g_ref, wg_ref, wu_ref, cw_ref, wd_ref, o_ref, h_ref, acc_ref, halo_ref,
                *, blocks_per_seq):
    i = pl.program_id(0)
    j = pl.program_id(1)
    tm = x_ref.shape[0]

    @pl.when(j == 0)
    def _():
        x = x_ref[...]
        ms = jnp.mean(x * x, axis=-1, keepdims=True)
        h_ref[...] = (x * lax.rsqrt(ms + NORM_EPS) * g_ref[...]).astype(BF16)
        acc_ref[...] = jnp.zeros(acc_ref.shape, F32)

    @pl.when(i % blocks_per_seq == 0)
    def _():
        halo_ref[j] = jnp.zeros(halo_ref.shape[1:], F32)

    h = h_ref[...]
    gate = _dot(h, wg_ref[...])
    up = _dot(h, wu_ref[...])
    prev = halo_ref[j]
    halo_ref[j] = gate[tm - SUBLANES:, :]
    gc = _causal_conv(gate, prev, cw_ref[...])
    act = (gc * _sigmoid(gc) * up).astype(BF16)
    acc_ref[...] += _dot(act, wd_ref[...])

    @pl.when(j == pl.num_programs(1) - 1)
    def _():
        o_ref[...] = x_ref[...] + acc_ref[...]


def _ffn(x, g, wg, wu, cw, wd, *, seq, tm, tf):
    n, d = x.shape
    dff = wg.shape[1]
    return pl.pallas_call(
        functools.partial(_ffn_kernel, blocks_per_seq=seq // tm),
        out_shape=jax.ShapeDtypeStruct((n, d), F32),
        grid=(n // tm, dff // tf),
        in_specs=[
            pl.BlockSpec((tm, d), lambda i, j: (i, 0)),
            pl.BlockSpec((1, d), lambda i, j: (0, 0)),
            pl.BlockSpec((d, tf), lambda i, j: (0, j)),
            pl.BlockSpec((d, tf), lambda i, j: (0, j)),
            pl.BlockSpec((FFN_CONV, tf), lambda i, j: (0, j)),
            pl.BlockSpec((tf, d), lambda i, j: (j, 0)),
        ],
        out_specs=pl.BlockSpec((tm, d), lambda i, j: (i, 0)),
        scratch_shapes=[
            pltpu.VMEM((tm, d), BF16),
            pltpu.VMEM((tm, d), F32),
            pltpu.VMEM((dff // tf, SUBLANES, tf), F32),
        ],
        compiler_params=_cparams("arbitrary", "arbitrary"),
        name="conv_gated_mlp",
    )(x, g, wg, wu, cw, wd)


def _pack_w_in(w_in):
    sizes = (D_SSD, SSD_CONV_DIM, SSD_HEADS, D_ATT, D_ATT, D_ATT,
             IDX_HEADS * IDX_DIM, IDX_DIM, IDX_HEADS, D_SC, D_SC, D_SC)
    offs = [0]
    for s in sizes:
        offs.append(offs[-1] + s)
    z, xbc, dt, q, k, v, qi, ki, wi, scb, scc, sch = (
        w_in[..., offs[t]:offs[t + 1]] for t in range(len(sizes)))
    padto = lambda a, w: jnp.pad(a, ((0, 0), (0, 0), (0, w - a.shape[-1])))
    parts = [xbc, q, z, k, v, qi, scb, scc, sch, padto(dt, LANES),
             jnp.concatenate([ki, ki], axis=-1), padto(wi, LANES)]
    packed = jnp.concatenate(parts, axis=-1)
    return padto(packed, D_PACKED).astype(BF16)


def _forward(x, norm_mix, w_in, ssd_conv_w, ssd_conv_b, ssd_dt_bias, ssd_a_log, ssd_d, ssd_norm,
             att_q_norm, att_k_norm, rel_bias, sc_conv_w, w_out, norm_ffn,
             ffn_w_gate, ffn_w_up, ffn_conv_w, ffn_w_down, *, tm_proj, tn_proj, tm_out, tn_out,
             tm_ffn, tf_ffn, tm_prep, att_blk):
    bsz, seq, d = x.shape
    depth = w_in.shape[0]
    n = bsz * seq
    xf = x.reshape(n, d)
    w_in_p = _pack_w_in(w_in)
    w_out_b = w_out.astype(BF16)
    wg_b = ffn_w_gate.astype(BF16)
    wu_b = ffn_w_up.astype(BF16)
    wd_b = ffn_w_down.astype(BF16)
    bias = _bias_tiles(rel_bias, att_blk)
    for l in range(depth):
        proj = _rms_matmul(xf, norm_mix[l].reshape(1, d), w_in_p[l], tm=tm_proj, tn=tn_proj)
        y_ssd = _ssd(proj, ssd_conv_w[l], ssd_conv_b[l], ssd_dt_bias[l], ssd_a_log[l], ssd_d[l],
                     ssd_norm[l], bsz=bsz, seq=seq)
        qn, kn, vb, qi, ki = _dsa_prep(proj, att_q_norm[l], att_k_norm[l], tm=tm_prep)
        y_att = _dsa(qn, qi, proj, kn, vb, ki, bias, bsz=bsz, seq=seq, blk=att_blk)
        xf = _outproj(xf, y_ssd, y_att, proj, sc_conv_w[l], w_out_b[l], seq=seq, tm=tm_out, tn=tn_out)
        xf = _ffn(xf, norm_ffn[l].reshape(1, d), wg_b[l], wu_b[l], ffn_conv_w[l], wd_b[l],
                  seq=seq, tm=tm_ffn, tf=tf_ffn)
    return xf.reshape(bsz, seq, d)


def kernel(x, norm_mix, w_in, ssd_conv_w, ssd_conv_b, ssd_dt_bias, ssd_a_log, ssd_d, ssd_norm, att_q_norm, att_k_norm, rel_bias, sc_conv_w, w_out, norm_ffn, ffn_w_gate, ffn_w_up, ffn_conv_w, ffn_w_down):
    return _forward(x, norm_mix, w_in, ssd_conv_w, ssd_conv_b, ssd_dt_bias, ssd_a_log, ssd_d, ssd_norm,
                    att_q_norm, att_k_norm, rel_bias, sc_conv_w, w_out, norm_ffn,
                    ffn_w_gate, ffn_w_up, ffn_conv_w, ffn_w_down,
                    tm_proj=1024, tn_proj=512, tm_out=512, tn_out=1024,
                    tm_ffn=512, tf_ffn=512, tm_prep=512, att_blk=256)
```

```python
import functools
import math

import jax
import jax.numpy as jnp
from jax import lax
from jax.experimental import pallas as pl
from jax.experimental.pallas import tpu as pltpu

F32 = jnp.float32
BF16 = jnp.bfloat16
HIGHEST = lax.Precision.HIGHEST

LANES = 128
SUBLANES = 8
VMEM_LIMIT_BYTES = 56 * 1024 * 1024

D_MODEL = 2048
D_SSD = 1024
D_ATT = 512
D_SC = 512
SSD_HEAD_DIM = 64
SSD_HEADS = 16
SSD_GROUPS = 2
SSD_STATE = 128
SSD_CONV = 4
SSD_CHUNK = 128
SSD_CONV_DIM = D_SSD + 2 * SSD_GROUPS * SSD_STATE
ATT_HEAD_DIM = 64
ATT_HEADS = 8
IDX_HEADS = 8
IDX_DIM = 64
TOPK_MAX = 256
REL_BUCKETS = 32
REL_MAX_DIST = 128
SC_CONV = 3
D_FF = 5632
FFN_CONV = 3
NORM_EPS = 1e-6

COL_XBC = 0
COL_Q = 1536
COL_Z = 2048
COL_K = 3072
COL_V = 3584
COL_QIDX = 4096
COL_SCB = 4608
COL_SCC = 5120
COL_SCH = 5632
COL_DT = 6144
COL_KIDX = 6272
COL_WIDX = 6400
D_PACKED = 6656

NEG_BIG = -1e30


def _cparams(*sem):
    return pltpu.CompilerParams(dimension_semantics=sem, vmem_limit_bytes=VMEM_LIMIT_BYTES)


def _dot(a, b):
    return jnp.dot(a, b, preferred_element_type=F32)


def _dot_nt(a, b):
    return lax.dot_general(a, b, (((1,), (1,)), ((), ())), preferred_element_type=F32)


def _dot_exact(a, b):
    return jnp.dot(a, b, preferred_element_type=F32, precision=HIGHEST)


def _sigmoid(x):
    return 1.0 / (1.0 + jnp.exp(-x))


def _shift_rows(x, prev8, s):
    xr = pltpu.roll(x, s, 0)
    pr = pltpu.roll(prev8, s, 0)
    rows = lax.broadcasted_iota(jnp.int32, (SUBLANES, x.shape[1]), 0)
    top = jnp.where(rows < s, pr, xr[:SUBLANES])
    return jnp.concatenate([top, xr[SUBLANES:]], axis=0)


def _causal_conv(x, prev8, w):
    k = w.shape[0]
    y = w[k - 1:k] * x
    for s in range(1, k):
        y = y + w[k - 1 - s:k - s] * _shift_rows(x, prev8, s)
    return y


def _rms_matmul_kernel(x_ref, g_ref, w_ref, o_ref, h_ref):
    @pl.when(pl.program_id(1) == 0)
    def _():
        x = x_ref[...]
        ms = jnp.mean(x * x, axis=-1, keepdims=True)
        h_ref[...] = (x * lax.rsqrt(ms + NORM_EPS) * g_ref[...]).astype(BF16)

    o_ref[...] = _dot(h_ref[...], w_ref[...])


def _rms_matmul(x, g, w, *, tm, tn):
    n, d = x.shape
    dout = w.shape[1]
    return pl.pallas_call(
        _rms_matmul_kernel,
        out_shape=jax.ShapeDtypeStruct((n, dout), F32),
        grid=(n // tm, dout // tn),
        in_specs=[
            pl.BlockSpec((tm, d), lambda i, j: (i, 0)),
            pl.BlockSpec((1, d), lambda i, j: (0, 0)),
            pl.BlockSpec((d, tn), lambda i, j: (0, j)),
        ],
        out_specs=pl.BlockSpec((tm, tn), lambda i, j: (i, j)),
        scratch_shapes=[pltpu.VMEM((tm, d), BF16)],
        compiler_params=_cparams("arbitrary", "arbitrary"),
        name="rms_in_proj",
    )(x, g, w)


def _ssd_kernel(xbc_ref, halo_ref, z_ref, dt_ref, cw_ref, cb_ref, dtb_ref, alog_ref,
                dtbx_ref, alogx_ref, dx_ref, ng_ref, ex_ref, o_ref, h_ref):
    c = pl.program_id(1)
    L = SSD_CHUNK

    @pl.when(c == 0)
    def _():
        h_ref[...] = jnp.zeros(h_ref.shape, F32)

    prev = jnp.where(c == 0, 0.0, halo_ref[...])
    xbc = _causal_conv(xbc_ref[...], prev, cw_ref[...]) + cb_ref[...]
    xbc = xbc * _sigmoid(xbc)
    xs = xbc[:, :D_SSD]

    row = lax.broadcasted_iota(jnp.int32, (L, L), 0)
    col = lax.broadcasted_iota(jnp.int32, (L, L), 1)
    tril = (row >= col).astype(F32)

    def softplus(v):
        return jnp.maximum(v, 0.0) + jnp.log1p(jnp.exp(-jnp.abs(v)))

    dt_s = softplus(dt_ref[...] + dtb_ref[...])
    a_s = dt_s * (-jnp.exp(alog_ref[...]))
    acs_s = _dot_exact(tril, a_s)
    acs_t = acs_s.T

    dt_x = softplus(_dot_exact(dt_ref[...], ex_ref[...]) + dtbx_ref[...])
    a_x = dt_x * (-jnp.exp(alogx_ref[...]))
    acs_x = _dot_exact(tril, a_x)
    last = acs_x[L - 1:L, :]
    xdt = xs * dt_x
    xdt_end = (xdt * jnp.exp(last - acs_x)).astype(BF16)
    xdt = xdt.astype(BF16)
    e_x = jnp.exp(acs_x)
    decay = jnp.exp(last)

    lane = lax.broadcasted_iota(jnp.int32, (L, LANES), 1)
    lo = lane < SSD_HEAD_DIM
    lower = row >= col

    for g in range(SSD_GROUPS):
        bm = xbc[:, D_SSD + g * SSD_STATE:D_SSD + (g + 1) * SSD_STATE].astype(BF16)
        cm = xbc[:, D_SSD + (SSD_GROUPS + g) * SSD_STATE:
                 D_SSD + (SSD_GROUPS + g + 1) * SSD_STATE].astype(BF16)
        cb = _dot_nt(cm, bm)
        bm_t = bm.T
        for q in range(4):
            pair = g * 4 + q
            sl = slice(pair * LANES, (pair + 1) * LANES)
            gs = []
            for hh in range(2):
                h = 2 * pair + hh
                seg = (jnp.broadcast_to(acs_s[:, h:h + 1], (L, L))
                       - jnp.broadcast_to(acs_t[h:h + 1, :], (L, L)))
                lm = jnp.where(lower, jnp.exp(seg), 0.0)
                gs.append((cb * lm).astype(BF16))
            gpair = jnp.concatenate(gs, axis=1)
            xp = xdt[:, sl]
            zero = jnp.zeros_like(xp)
            x2 = jnp.concatenate([jnp.where(lo, xp, zero), jnp.where(lo, zero, xp)], axis=0)
            y = _dot(gpair, x2)
            hprev = h_ref[pair]
            y = y + e_x[:, sl] * _dot(cm, hprev.astype(BF16))
            y = y + xs[:, sl] * dx_ref[:, sl]
            h_ref[pair] = hprev * decay[:, sl] + _dot(bm_t, xdt_end[:, sl])
            o_ref[:, sl] = y

    z = z_ref[...]
    y = o_ref[...] * (z * _sigmoid(z))
    half = D_SSD // SSD_GROUPS
    outs = []
    for g in range(SSD_GROUPS):
        yg = y[:, g * half:(g + 1) * half]
        ms = jnp.mean(yg * yg, axis=-1, keepdims=True)
        outs.append(yg * lax.rsqrt(ms + NORM_EPS))
    o_ref[...] = jnp.concatenate(outs, axis=1) * ng_ref[...]


def _ssd(proj, conv_w, conv_b, dt_bias, a_log, d_skip, norm_g, *, bsz, seq):
    nc = seq // SSD_CHUNK
    n = bsz * seq
    pad = LANES - SSD_HEADS
    dtb = jnp.pad(dt_bias, (0, pad)).reshape(1, LANES)
    alog = jnp.pad(a_log, (0, pad)).reshape(1, LANES)
    rep = lambda v: jnp.repeat(v, SSD_HEAD_DIM).reshape(1, D_SSD)
    expand = (jnp.arange(LANES)[:, None] == (jnp.arange(D_SSD)[None, :] // SSD_HEAD_DIM)).astype(F32)
    rows_per_halo = SSD_CHUNK // SUBLANES
    const = lambda shape: pl.BlockSpec(shape, lambda b, c: (0,) * len(shape))
    return pl.pallas_call(
        _ssd_kernel,
        out_shape=jax.ShapeDtypeStruct((n, D_SSD), F32),
        grid=(bsz, nc),
        in_specs=[
            pl.BlockSpec((SSD_CHUNK, SSD_CONV_DIM), lambda b, c: (b * nc + c, COL_XBC // SSD_CONV_DIM)),
            pl.BlockSpec((SUBLANES, SSD_CONV_DIM),
                         lambda b, c: (jnp.maximum((b * nc + c) * rows_per_halo - 1, 0), 0)),
            pl.BlockSpec((SSD_CHUNK, D_SSD), lambda b, c: (b * nc + c, COL_Z // D_SSD)),
            pl.BlockSpec((SSD_CHUNK, LANES), lambda b, c: (b * nc + c, COL_DT // LANES)),
            const((SSD_CONV, SSD_CONV_DIM)),
            const((1, SSD_CONV_DIM)),
            const((1, LANES)),
            const((1, LANES)),
            const((1, D_SSD)),
            const((1, D_SSD)),
            const((1, D_SSD)),
            const((1, D_SSD)),
            const((LANES, D_SSD)),
        ],
        out_specs=pl.BlockSpec((SSD_CHUNK, D_SSD), lambda b, c: (b * nc + c, 0)),
        scratch_shapes=[pltpu.VMEM((SSD_HEADS // 2, SSD_STATE, LANES), F32)],
        compiler_params=_cparams("arbitrary", "arbitrary"),
        name="ssd_scan",
    )(proj, proj, proj, proj, conv_w, conv_b.reshape(1, -1), dtb, alog,
      rep(dt_bias), rep(a_log), rep(d_skip), norm_g.reshape(1, -1), expand)


def _dsa_prep_kernel(q_ref, k_ref, v_ref, qi_ref, ki_ref, qg_ref, kg_ref, seg_ref,
                     qn_ref, kn_ref, vt_ref, qib_ref, kib_ref, *, blk):
    def head_norm(x, g):
        ms = _dot_exact(x * x, seg_ref[...]) * (1.0 / ATT_HEAD_DIM)
        return x * lax.rsqrt(ms + NORM_EPS) * g

    qn_ref[...] = (head_norm(q_ref[...], qg_ref[...]) * (ATT_HEAD_DIM ** -0.5)).astype(BF16)
    kn_ref[...] = head_norm(k_ref[...], kg_ref[...]).astype(BF16)
    for c in range(vt_ref.shape[0]):
        vt_ref[c] = v_ref[c * blk:(c + 1) * blk, :].T.astype(BF16)
    qib_ref[...] = qi_ref[...].astype(BF16)
    kib_ref[...] = ki_ref[...].astype(BF16)


def _dsa_prep(proj, q_norm, k_norm, *, tm, blk):
    n = proj.shape[0]
    seg = (jnp.arange(D_ATT)[:, None] // ATT_HEAD_DIM
           == jnp.arange(D_ATT)[None, :] // ATT_HEAD_DIM).astype(F32)
    tile = lambda v: jnp.tile(v, ATT_HEADS).reshape(1, D_ATT)
    col_blk = lambda col, w: pl.BlockSpec((tm, w), lambda i: (i, col // w))
    const = lambda shape: pl.BlockSpec(shape, lambda i: (0,) * len(shape))
    return pl.pallas_call(
        functools.partial(_dsa_prep_kernel, blk=blk),
        out_shape=(
            jax.ShapeDtypeStruct((n, D_ATT), BF16),
            jax.ShapeDtypeStruct((n, D_ATT), BF16),
            jax.ShapeDtypeStruct((n // blk, D_ATT, blk), BF16),
            jax.ShapeDtypeStruct((n, D_ATT), BF16),
            jax.ShapeDtypeStruct((n, LANES), BF16),
        ),
        grid=(n // tm,),
        in_specs=[col_blk(COL_Q, D_ATT), col_blk(COL_K, D_ATT), col_blk(COL_V, D_ATT),
                  col_blk(COL_QIDX, D_ATT), col_blk(COL_KIDX, LANES),
                  const((1, D_ATT)), const((1, D_ATT)), const((D_ATT, D_ATT))],
        out_specs=(
            pl.BlockSpec((tm, D_ATT), lambda i: (i, 0)),
            pl.BlockSpec((tm, D_ATT), lambda i: (i, 0)),
            pl.BlockSpec((tm // blk, D_ATT, blk), lambda i: (i, 0, 0)),
            pl.BlockSpec((tm, D_ATT), lambda i: (i, 0)),
            pl.BlockSpec((tm, LANES), lambda i: (i, 0)),
        ),
        compiler_params=_cparams("arbitrary"),
        name="dsa_prep",
    )(proj, proj, proj, proj, proj, tile(q_norm), tile(k_norm), seg)


def _dsa_kernel(qn_ref, qi_ref, w_ref, kn_ref, vt_ref, ki_ref, bias_ref, o_ref,
                key_ref, hi_ref, lo_ref, big_ref, qm_ref, qim_ref, wt_ref, mb_ref, carry_ref,
                m_ref, l_ref, acc_ref, *, blk, topk):
    i = pl.program_id(1)
    pack = 16
    lane = lax.broadcasted_iota(jnp.int32, (blk, LANES), 1)
    lo_lanes = lane < ATT_HEAD_DIM
    krow = lax.broadcasted_iota(jnp.int32, (blk, blk), 0)
    qcol = lax.broadcasted_iota(jnp.int32, (blk, blk), 1)
    future = krow > qcol

    for p in range(ATT_HEADS // 2):
        sl = slice(p * LANES, (p + 1) * LANES)
        qp = qn_ref[:, sl]
        qip = qi_ref[:, sl]
        zero = jnp.zeros_like(qp)
        qm_ref[2 * p * blk:(2 * p + 1) * blk] = jnp.where(lo_lanes, qp, zero)
        qm_ref[(2 * p + 1) * blk:(2 * p + 2) * blk] = jnp.where(lo_lanes, zero, qp)
        qim_ref[2 * p * blk:(2 * p + 1) * blk] = jnp.where(lo_lanes, qip, zero)
        qim_ref[(2 * p + 1) * blk:(2 * p + 2) * blk] = jnp.where(lo_lanes, zero, qip)
    wt_ref[...] = w_ref[...].T

    def score_tile(j, diag):
        kk = ki_ref[pl.ds(pl.multiple_of(j * blk, blk), blk), :]
        d_all = _dot_nt(kk, qim_ref[...])
        s = jnp.zeros((blk, blk), F32)
        for h in range(IDX_HEADS):
            s = s + jnp.maximum(d_all[:, h * blk:(h + 1) * blk], 0.0) * wt_ref[h:h + 1, :]
        if diag:
            s = jnp.where(future, -jnp.inf, s)
        s = jnp.where(s == 0.0, 0.0, s)
        bits = pltpu.bitcast(s, jnp.int32)
        key = bits ^ ((bits >> 31) & 0x7FFFFFFF)
        key_ref[j] = key
        hi_ref[j] = (key >> 16).astype(jnp.int16)
        lo_ref[j] = ((key & 0xFFFF) - 32768).astype(jnp.int16)

    def score_body(j, carry):
        score_tile(j, False)
        return carry

    lax.fori_loop(0, i, score_body, 0)
    score_tile(i, True)

    one16 = jnp.ones((), jnp.int16)
    zero16 = jnp.zeros((), jnp.int16)

    def count_ge(src_ref, cand):
        cand_b = jnp.broadcast_to(cand.astype(jnp.int16), (blk, blk))

        def add_tile(j, cnt, on):
            ind = jnp.where(src_ref[j] >= cand_b, on, zero16)
            for g in range(blk // pack):
                cnt = cnt + ind[g * pack:(g + 1) * pack]
            return cnt

        def body(t, cnt):
            return add_tile(2 * t + 1, add_tile(2 * t, cnt, one16), one16)

        cnt = lax.fori_loop(0, (i + 1) // 2, body, jnp.zeros((pack, blk), jnp.int16))
        cnt = add_tile(i, cnt, ((i + 1) % 2).astype(jnp.int16))
        return jnp.sum(cnt.astype(jnp.int32).astype(F32), axis=0, keepdims=True)

    def bisect16(src_ref, rank, cnt_all):
        def bit_body(b, st):
            prefix, c_acc, c_rej = st
            cand = prefix + lax.shift_left(jnp.int32(1), 15 - b)
            cnt = count_ge(src_ref, cand)
            ok = cnt >= rank
            return (jnp.where(ok, cand, prefix), jnp.where(ok, cnt, c_acc), jnp.where(ok, c_rej, cnt))
        init = (jnp.full((1, blk), -32768, jnp.int32), cnt_all, jnp.zeros((1, blk), F32))
        return lax.fori_loop(0, 16, bit_body, init)

    ncols = jnp.full((1, blk), ((i + 1) * blk).astype(F32), F32)
    p_hi, c_acc1, c_rej1 = bisect16(hi_ref, jnp.float32(topk), ncols)
    p_hi_b = jnp.broadcast_to(p_hi.astype(jnp.int16), (blk, blk))

    def group_body(j, carry):
        lo_ref[j] = jnp.where(hi_ref[j] == p_hi_b, lo_ref[j], jnp.int16(-32768))
        return carry

    lax.fori_loop(0, i + 1, group_body, 0)
    rank2 = jnp.float32(topk) - c_rej1
    p_lo, c_acc2, c_rej2 = bisect16(lo_ref, rank2, c_acc1 - c_rej1)
    thr = lax.shift_left(p_hi, 16) + (p_lo + 32768)
    need = rank2 - c_rej2
    has_ties = jnp.max((c_acc2 - c_rej2) - need) > 0.5

    m_ref[...] = jnp.full(m_ref.shape, NEG_BIG, F32)
    l_ref[...] = jnp.zeros(l_ref.shape, F32)
    acc_ref[...] = jnp.zeros(acc_ref.shape, F32)
    carry_ref[...] = jnp.zeros(carry_ref.shape, F32)

    def by_parity(x, fn):
        @pl.when(x % 2 == 0)
        def _():
            fn(0)

        @pl.when(x % 2 == 1)
        def _():
            fn(1)

    def qk_dots(j, slot):
        start = pl.multiple_of(j * blk, blk)
        for p in range(ATT_HEADS // 2):
            big_ref[slot, :, 2 * p * blk:(2 * p + 2) * blk] = _dot_nt(
                kn_ref[pl.ds(start, blk), p * LANES:(p + 1) * LANES], qm_ref[2 * p * blk:(2 * p + 2) * blk])

    def tile_mask(j, kind):
        kt = key_ref[j]

        @pl.when(jnp.logical_not(has_ties))
        def _():
            mb = jnp.where(kt >= thr, 0.0, NEG_BIG)
            mb_ref[...] = jnp.where(future, NEG_BIG, mb) if kind == 2 else mb

        @pl.when(has_ties)
        def _():
            eq = jnp.where(kt == thr, 1.0, 0.0)
            lower = jnp.where(krow >= qcol, 1.0, 0.0).astype(BF16)
            seen = _dot(lower, eq.astype(BF16)) + carry_ref[0:1, :]
            keep = jnp.where(seen <= need, 0.0, NEG_BIG)
            mb = jnp.where(kt > thr, 0.0, jnp.where(kt == thr, keep, NEG_BIG))
            mb_ref[...] = jnp.where(future, NEG_BIG, mb) if kind == 2 else mb
            carry_ref[0:1, :] += jnp.sum(eq, axis=0, keepdims=True)

    def softmax_pv(j, slot, kind):
        vt = vt_ref[j]
        for h in range(ATT_HEADS):
            s = big_ref[slot, :, h * blk:(h + 1) * blk] + mb_ref[...]
            if kind > 0:
                s = s + bias_ref[kind - 1, h]
            m_prev = m_ref[h]
            m_new = jnp.maximum(m_prev, jnp.max(s, axis=0, keepdims=True))
            alpha = jnp.exp(m_prev - m_new)
            pexp = jnp.exp(s - m_new[0:1, :])
            l_ref[h] = alpha * l_ref[h] + jnp.sum(pexp, axis=0, keepdims=True)
            m_ref[h] = m_new
            ch = slice(h * ATT_HEAD_DIM, (h + 1) * ATT_HEAD_DIM)
            acc_ref[ch, :] = alpha[0:1, :] * acc_ref[ch, :] + _dot(vt[ch, :], pexp.astype(BF16))

    qk_dots(0, 0)

    def attn_body(j, carry):
        tile_mask(j, 0)

        def step(slot):
            qk_dots(j + 1, 1 - slot)
            softmax_pv(j, slot, 0)
        by_parity(j, step)
        return carry

    lax.fori_loop(0, i - 1, attn_body, 0)

    @pl.when(i > 0)
    def _():
        tile_mask(i - 1, 1)

        def step(slot):
            qk_dots(i, 1 - slot)
            softmax_pv(i - 1, slot, 1)
        by_parity(i - 1, step)

    tile_mask(i, 2)
    by_parity(i, lambda slot: softmax_pv(i, slot, 2))

    outs = []
    for h in range(ATT_HEADS):
        ch = slice(h * ATT_HEAD_DIM, (h + 1) * ATT_HEAD_DIM)
        outs.append(acc_ref[ch, :] / l_ref[h][0:1, :])
    o_ref[...] = jnp.concatenate(outs, axis=0).T.astype(BF16)


def _t5_bucket(dist):
    n = jnp.maximum(dist, 0)
    max_exact = REL_BUCKETS // 2
    large = max_exact + (jnp.log(jnp.maximum(n, max_exact).astype(F32) / max_exact)
                         / math.log(REL_MAX_DIST / max_exact)
                         * (REL_BUCKETS - max_exact)).astype(jnp.int32)
    large = jnp.minimum(large, REL_BUCKETS - 1)
    return jnp.where(n < max_exact, n, large)


def _bias_tiles(rel_bias, blk):
    tab = rel_bias[_t5_bucket(jnp.arange(2 * blk, dtype=jnp.int32))] - rel_bias[REL_BUCKETS - 1]
    r = jnp.arange(blk)[:, None]
    c = jnp.arange(blk)[None, :]
    sub = tab[blk + r - c]
    diag = tab[jnp.maximum(r - c, 0)]
    return jnp.stack([sub, diag]).transpose(0, 3, 2, 1)


def _dsa(qn, qi, proj, kn, vt, ki, bias, *, bsz, seq, blk):
    n = bsz * seq
    nb = seq // blk
    topk = min(TOPK_MAX, seq // 4)
    once = pl.Buffered(1)
    return pl.pallas_call(
        functools.partial(_dsa_kernel, blk=blk, topk=topk),
        out_shape=jax.ShapeDtypeStruct((n, D_ATT), BF16),
        grid=(bsz, nb),
        in_specs=[
            pl.BlockSpec((blk, D_ATT), lambda b, i: (b * nb + i, 0)),
            pl.BlockSpec((blk, D_ATT), lambda b, i: (b * nb + i, 0)),
            pl.BlockSpec((blk, LANES), lambda b, i: (b * nb + i, COL_WIDX // LANES)),
            pl.BlockSpec((seq, D_ATT), lambda b, i: (b, 0), pipeline_mode=once),
            pl.BlockSpec((nb, D_ATT, blk), lambda b, i: (b, 0, 0), pipeline_mode=once),
            pl.BlockSpec((seq, LANES), lambda b, i: (b, 0), pipeline_mode=once),
            pl.BlockSpec((2, ATT_HEADS, blk, blk), lambda b, i: (0, 0, 0, 0), pipeline_mode=once),
        ],
        out_specs=pl.BlockSpec((blk, D_ATT), lambda b, i: (b * nb + i, 0)),
        scratch_shapes=[
            pltpu.VMEM((nb, blk, blk), jnp.int32),
            pltpu.VMEM((nb, blk, blk), jnp.int16),
            pltpu.VMEM((nb, blk, blk), jnp.int16),
            pltpu.VMEM((2, blk, ATT_HEADS * blk), F32),
            pltpu.VMEM((ATT_HEADS * blk, LANES), BF16),
            pltpu.VMEM((IDX_HEADS * blk, LANES), BF16),
            pltpu.VMEM((LANES, blk), F32),
            pltpu.VMEM((blk, blk), F32),
            pltpu.VMEM((SUBLANES, blk), F32),
            pltpu.VMEM((ATT_HEADS, SUBLANES, blk), F32),
            pltpu.VMEM((ATT_HEADS, SUBLANES, blk), F32),
            pltpu.VMEM((D_ATT, blk), F32),
        ],
        compiler_params=_cparams("arbitrary", "arbitrary"),
        name="dsa_attention",
    )(qn, qi, proj, kn, vt, ki, bias)


def _outproj_kernel(x_ref, yssd_ref, yatt_ref, scb_ref, scc_ref, sch_ref, hc_ref, hh_ref,
                    cw_ref, w_ref, o_ref, cat_ref, *, blocks_per_seq):
    i = pl.program_id(0)

    @pl.when(pl.program_id(1) == 0)
    def _():
        prev = jnp.where(i % blocks_per_seq == 0, 0.0, hc_ref[...] * hh_ref[...])
        ysc = scb_ref[...] * _causal_conv(scc_ref[...] * sch_ref[...], prev, cw_ref[...])
        cat_ref[:, :D_SSD] = yssd_ref[...].astype(BF16)
        cat_ref[:, D_SSD:D_SSD + D_ATT] = yatt_ref[...]
        cat_ref[:, D_SSD + D_ATT:] = ysc.astype(BF16)

    o_ref[...] = x_ref[...] + _dot(cat_ref[...], w_ref[...])


def _outproj(x, y_ssd, y_att, proj, sc_w, w_out, *, seq, tm, tn):
    n, d = x.shape
    halo = tm // SUBLANES
    blk = lambda col: pl.BlockSpec((tm, D_SC), lambda i, j: (i, col // D_SC))
    hblk = lambda col: pl.BlockSpec(
        (SUBLANES, D_SC), lambda i, j: (jnp.maximum(i * halo - 1, 0), col // D_SC))
    return pl.pallas_call(
        functools.partial(_outproj_kernel, blocks_per_seq=seq // tm),
        out_shape=jax.ShapeDtypeStruct((n, d), F32),
        grid=(n // tm, d // tn),
        in_specs=[
            pl.BlockSpec((tm, tn), lambda i, j: (i, j)),
            pl.BlockSpec((tm, D_SSD), lambda i, j: (i, 0)),
            pl.BlockSpec((tm, D_ATT), lambda i, j: (i, 0)),
            blk(COL_SCB), blk(COL_SCC), blk(COL_SCH), hblk(COL_SCC), hblk(COL_SCH),
            pl.BlockSpec((SC_CONV, D_SC), lambda i, j: (0, 0)),
            pl.BlockSpec((d, tn), lambda i, j: (0, j)),
        ],
        out_specs=pl.BlockSpec((tm, tn), lambda i, j: (i, j)),
        scratch_shapes=[pltpu.VMEM((tm, d), BF16)],
        compiler_params=_cparams("arbitrary", "arbitrary"),
        name="out_proj",
    )(x, y_ssd, y_att, proj, proj, proj, proj, proj, sc_w, w_out)


def _ffn_kernel(x_ref, g_ref, wg_ref, wu_ref, cw_ref, wd_ref, o_ref, h_ref, acc_ref, halo_ref,
                *, blocks_per_seq):
    i = pl.program_id(0)
    j = pl.program_id(1)
    tm = x_ref.shape[0]

    @pl.when(j == 0)
    def _():
        x = x_ref[...]
        ms = jnp.mean(x * x, axis=-1, keepdims=True)
        h_ref[...] = (x * lax.rsqrt(ms + NORM_EPS) * g_ref[...]).astype(BF16)
        acc_ref[...] = jnp.zeros(acc_ref.shape, F32)

    @pl.when(i % blocks_per_seq == 0)
    def _():
        halo_ref[j] = jnp.zeros(halo_ref.shape[1:], F32)

    h = h_ref[...]
    gate = _dot(h, wg_ref[...])
    up = _dot(h, wu_ref[...])
    prev = halo_ref[j]
    halo_ref[j] = gate[tm - SUBLANES:, :]
    gc = _causal_conv(gate, prev, cw_ref[...])
    act = (gc * _sigmoid(gc) * up).astype(BF16)
    acc_ref[...] += _dot(act, wd_ref[...])

    @pl.when(j == pl.num_programs(1) - 1)
    def _():
        o_ref[...] = x_ref[...] + acc_ref[...]


def _ffn(x, g, wg, wu, cw, wd, *, seq, tm, tf):
    n, d = x.shape
    dff = wg.shape[1]
    return pl.pallas_call(
        functools.partial(_ffn_kernel, blocks_per_seq=seq // tm),
        out_shape=jax.ShapeDtypeStruct((n, d), F32),
        grid=(n // tm, dff // tf),
        in_specs=[
            pl.BlockSpec((tm, d), lambda i, j: (i, 0)),
            pl.BlockSpec((1, d), lambda i, j: (0, 0)),
            pl.BlockSpec((d, tf), lambda i, j: (0, j)),
            pl.BlockSpec((d, tf), lambda i, j: (0, j)),
            pl.BlockSpec((FFN_CONV, tf), lambda i, j: (0, j)),
            pl.BlockSpec((tf, d), lambda i, j: (j, 0)),
        ],
        out_specs=pl.BlockSpec((tm, d), lambda i, j: (i, 0)),
        scratch_shapes=[
            pltpu.VMEM((tm, d), BF16),
            pltpu.VMEM((tm, d), F32),
            pltpu.VMEM((dff // tf, SUBLANES, tf), F32),
        ],
        compiler_params=_cparams("arbitrary", "arbitrary"),
        name="conv_gated_mlp",
    )(x, g, wg, wu, cw, wd)


def _pack_w_in(w_in):
    sizes = (D_SSD, SSD_CONV_DIM, SSD_HEADS, D_ATT, D_ATT, D_ATT,
             IDX_HEADS * IDX_DIM, IDX_DIM, IDX_HEADS, D_SC, D_SC, D_SC)
    offs = [0]
    for s in sizes:
        offs.append(offs[-1] + s)
    z, xbc, dt, q, k, v, qi, ki, wi, scb, scc, sch = (
        w_in[..., offs[t]:offs[t + 1]] for t in range(len(sizes)))
    padto = lambda a, w: jnp.pad(a, ((0, 0), (0, 0), (0, w - a.shape[-1])))
    parts = [xbc, q, z, k, v, qi, scb, scc, sch, padto(dt, LANES),
             jnp.concatenate([ki, ki], axis=-1), padto(wi, LANES)]
    packed = jnp.concatenate(parts, axis=-1)
    return padto(packed, D_PACKED).astype(BF16)


def _forward(x, norm_mix, w_in, ssd_conv_w, ssd_conv_b, ssd_dt_bias, ssd_a_log, ssd_d, ssd_norm,
             att_q_norm, att_k_norm, rel_bias, sc_conv_w, w_out, norm_ffn,
             ffn_w_gate, ffn_w_up, ffn_conv_w, ffn_w_down, *, tm_proj, tn_proj, tm_out, tn_out,
             tm_ffn, tf_ffn, tm_prep, att_blk):
    bsz, seq, d = x.shape
    depth = w_in.shape[0]
    n = bsz * seq
    xf = x.reshape(n, d)
    w_in_p = _pack_w_in(w_in)
    w_out_b = w_out.astype(BF16)
    wg_b = ffn_w_gate.astype(BF16)
    wu_b = ffn_w_up.astype(BF16)
    wd_b = ffn_w_down.astype(BF16)
    bias = _bias_tiles(rel_bias, att_blk)
    for l in range(depth):
        proj = _rms_matmul(xf, norm_mix[l].reshape(1, d), w_in_p[l], tm=tm_proj, tn=tn_proj)
        y_ssd = _ssd(proj, ssd_conv_w[l], ssd_conv_b[l], ssd_dt_bias[l], ssd_a_log[l], ssd_d[l],
                     ssd_norm[l], bsz=bsz, seq=seq)
        qn, kn, vt, qi, ki = _dsa_prep(proj, att_q_norm[l], att_k_norm[l], tm=tm_prep, blk=att_blk)
        y_att = _dsa(qn, qi, proj, kn, vt, ki, bias, bsz=bsz, seq=seq, blk=att_blk)
        xf = _outproj(xf, y_ssd, y_att, proj, sc_conv_w[l], w_out_b[l], seq=seq, tm=tm_out, tn=tn_out)
        xf = _ffn(xf, norm_ffn[l].reshape(1, d), wg_b[l], wu_b[l], ffn_conv_w[l], wd_b[l],
                  seq=seq, tm=tm_ffn, tf=tf_ffn)
    return xf.reshape(bsz, seq, d)


def kernel(x, norm_mix, w_in, ssd_conv_w, ssd_conv_b, ssd_dt_bias, ssd_a_log, ssd_d, ssd_norm, att_q_norm, att_k_norm, rel_bias, sc_conv_w, w_out, norm_ffn, ffn_w_gate, ffn_w_up, ffn_conv_w, ffn_w_down):
    return _forward(x, norm_mix, w_in, ssd_conv_w, ssd_conv_b, ssd_dt_bias, ssd_a_log, ssd_d, ssd_norm,
                    att_q_norm, att_k_norm, rel_bias, sc_conv_w, w_out, norm_ffn,
                    ffn_w_gate, ffn_w_up, ffn_conv_w, ffn_w_down,
                    tm_proj=1024, tn_proj=512, tm_out=512, tn_out=1024,
                    tm_ffn=512, tf_ffn=512, tm_prep=512, att_blk=256)
```

```python
import functools
import math

import jax
import jax.numpy as jnp
from jax import lax
from jax.experimental import pallas as pl
from jax.experimental.pallas import tpu as pltpu

F32 = jnp.float32
BF16 = jnp.bfloat16
HIGHEST = lax.Precision.HIGHEST

LANES = 128
SUBLANES = 8
VMEM_LIMIT_BYTES = 56 * 1024 * 1024

D_MODEL = 2048
D_SSD = 1024
D_ATT = 512
D_SC = 512
SSD_HEAD_DIM = 64
SSD_HEADS = 16
SSD_GROUPS = 2
SSD_STATE = 128
SSD_CONV = 4
SSD_CHUNK = 128
SSD_CONV_DIM = D_SSD + 2 * SSD_GROUPS * SSD_STATE
ATT_HEAD_DIM = 64
ATT_HEADS = 8
IDX_HEADS = 8
IDX_DIM = 64
TOPK_MAX = 256
REL_BUCKETS = 32
REL_MAX_DIST = 128
SC_CONV = 3
D_FF = 5632
FFN_CONV = 3
NORM_EPS = 1e-6

COL_XBC = 0
COL_Q = 1536
COL_Z = 2048
COL_K = 3072
COL_V = 3584
COL_QIDX = 4096
COL_SCB = 4608
COL_SCC = 5120
COL_SCH = 5632
COL_DT = 6144
COL_KIDX = 6272
COL_WIDX = 6400
D_PACKED = 6656

NEG_BIG = -1e30
LOG2E = math.log2(math.e)


def _cparams(*sem):
    return pltpu.CompilerParams(dimension_semantics=sem, vmem_limit_bytes=VMEM_LIMIT_BYTES)


def _dot(a, b):
    return jnp.dot(a, b, preferred_element_type=F32)


def _dot_nt(a, b):
    return lax.dot_general(a, b, (((1,), (1,)), ((), ())), preferred_element_type=F32)


def _dot_exact(a, b):
    return jnp.dot(a, b, preferred_element_type=F32, precision=HIGHEST)


def _sigmoid(x):
    return 1.0 / (1.0 + jnp.exp(-x))


def _shift_rows(x, prev8, s):
    xr = pltpu.roll(x, s, 0)
    pr = pltpu.roll(prev8, s, 0)
    rows = lax.broadcasted_iota(jnp.int32, (SUBLANES, x.shape[1]), 0)
    top = jnp.where(rows < s, pr, xr[:SUBLANES])
    return jnp.concatenate([top, xr[SUBLANES:]], axis=0)


def _causal_conv(x, prev8, w):
    k = w.shape[0]
    y = w[k - 1:k] * x
    for s in range(1, k):
        y = y + w[k - 1 - s:k - s] * _shift_rows(x, prev8, s)
    return y


def _rms_matmul_kernel(x_ref, g_ref, w_ref, o_ref, h_ref):
    @pl.when(pl.program_id(1) == 0)
    def _():
        x = x_ref[...]
        ms = jnp.mean(x * x, axis=-1, keepdims=True)
        h_ref[...] = (x * lax.rsqrt(ms + NORM_EPS) * g_ref[...]).astype(BF16)

    o_ref[...] = _dot(h_ref[...], w_ref[...])


def _rms_matmul(x, g, w, *, tm, tn):
    n, d = x.shape
    dout = w.shape[1]
    return pl.pallas_call(
        _rms_matmul_kernel,
        out_shape=jax.ShapeDtypeStruct((n, dout), F32),
        grid=(n // tm, dout // tn),
        in_specs=[
            pl.BlockSpec((tm, d), lambda i, j: (i, 0)),
            pl.BlockSpec((1, d), lambda i, j: (0, 0)),
            pl.BlockSpec((d, tn), lambda i, j: (0, j)),
        ],
        out_specs=pl.BlockSpec((tm, tn), lambda i, j: (i, j)),
        scratch_shapes=[pltpu.VMEM((tm, d), BF16)],
        compiler_params=_cparams("arbitrary", "arbitrary"),
        name="rms_in_proj",
    )(x, g, w)


def _ssd_kernel(xbc_ref, halo_ref, z_ref, dt_ref, cw_ref, cb_ref, dtb_ref, alog_ref,
                dx_ref, ng_ref, ex_ref, o_ref, h_ref, y_ref):
    c = pl.program_id(1)
    L = SSD_CHUNK

    @pl.when(c == 0)
    def _():
        h_ref[...] = jnp.zeros(h_ref.shape, F32)

    prev = jnp.where(c == 0, 0.0, halo_ref[...])
    xbc = _causal_conv(xbc_ref[...], prev, cw_ref[...]) + cb_ref[...]
    xbc = xbc * _sigmoid(xbc)
    xs = xbc[:, :D_SSD]

    row = lax.broadcasted_iota(jnp.int32, (L, L), 0)
    col = lax.broadcasted_iota(jnp.int32, (L, L), 1)
    tril = (row >= col).astype(F32)

    def softplus(v):
        return jnp.maximum(v, 0.0) + jnp.log1p(jnp.exp(-jnp.abs(v)))

    dt_s = softplus(dt_ref[...] + dtb_ref[...])
    a_s = dt_s * (-jnp.exp(alog_ref[...]))
    acs_s = _dot_exact(tril, a_s)
    acs_t = acs_s.T
    ex = ex_ref[...]
    dt_x = _dot_exact(dt_s, ex)
    e_x = _dot_exact(jnp.exp(acs_s), ex)
    de_x = _dot_exact(jnp.exp(acs_s[L - 1:L, :] - acs_s), ex)
    decay = e_x[L - 1:L, :]
    xdt = xs * dt_x
    xdt_end = (xdt * de_x).astype(BF16)
    xdt = xdt.astype(BF16)

    lane = lax.broadcasted_iota(jnp.int32, (L, LANES), 1)
    lo = lane < SSD_HEAD_DIM
    lower = row >= col

    for g in range(SSD_GROUPS):
        bm = xbc[:, D_SSD + g * SSD_STATE:D_SSD + (g + 1) * SSD_STATE].astype(BF16)
        cm = xbc[:, D_SSD + (SSD_GROUPS + g) * SSD_STATE:
                 D_SSD + (SSD_GROUPS + g + 1) * SSD_STATE].astype(BF16)
        cb = _dot_nt(cm, bm)
        bm_t = bm.T
        for q in range(4):
            pair = g * 4 + q
            sl = slice(pair * LANES, (pair + 1) * LANES)
            gs = []
            for hh in range(2):
                h = 2 * pair + hh
                seg = (jnp.broadcast_to(acs_s[:, h:h + 1], (L, L))
                       - jnp.broadcast_to(acs_t[h:h + 1, :], (L, L)))
                lm = jnp.where(lower, jnp.exp(seg), 0.0)
                gs.append((cb * lm).astype(BF16))
            gpair = jnp.concatenate(gs, axis=1)
            xp = xdt[:, sl]
            zero = jnp.zeros_like(xp)
            x2 = jnp.concatenate([jnp.where(lo, xp, zero), jnp.where(lo, zero, xp)], axis=0)
            y = _dot(gpair, x2)
            hprev = h_ref[pair]
            y = y + e_x[:, sl] * _dot(cm, hprev.astype(BF16))
            y = y + xs[:, sl] * dx_ref[:, sl]
            h_ref[pair] = hprev * decay[:, sl] + _dot(bm_t, xdt_end[:, sl])
            y_ref[:, sl] = y

    z = z_ref[...]
    y = y_ref[...] * (z * _sigmoid(z))
    half = D_SSD // SSD_GROUPS
    outs = []
    for g in range(SSD_GROUPS):
        yg = y[:, g * half:(g + 1) * half]
        ms = jnp.mean(yg * yg, axis=-1, keepdims=True)
        outs.append(yg * lax.rsqrt(ms + NORM_EPS))
    o_ref[...] = (jnp.concatenate(outs, axis=1) * ng_ref[...]).astype(BF16)


def _ssd(proj, conv_w, conv_b, dt_bias, a_log, d_skip, norm_g, *, bsz, seq):
    nc = seq // SSD_CHUNK
    n = bsz * seq
    pad = LANES - SSD_HEADS
    dtb = jnp.pad(dt_bias, (0, pad)).reshape(1, LANES)
    alog = jnp.pad(a_log, (0, pad), constant_values=NEG_BIG).reshape(1, LANES)
    rep = lambda v: jnp.repeat(v, SSD_HEAD_DIM).reshape(1, D_SSD)
    expand = (jnp.arange(LANES)[:, None] == (jnp.arange(D_SSD)[None, :] // SSD_HEAD_DIM)).astype(F32)
    rows_per_halo = SSD_CHUNK // SUBLANES
    const = lambda shape: pl.BlockSpec(shape, lambda b, c: (0,) * len(shape))
    return pl.pallas_call(
        _ssd_kernel,
        out_shape=jax.ShapeDtypeStruct((n, D_SSD), BF16),
        grid=(bsz, nc),
        in_specs=[
            pl.BlockSpec((SSD_CHUNK, SSD_CONV_DIM), lambda b, c: (b * nc + c, COL_XBC // SSD_CONV_DIM)),
            pl.BlockSpec((SUBLANES, SSD_CONV_DIM),
                         lambda b, c: (jnp.maximum((b * nc + c) * rows_per_halo - 1, 0), 0)),
            pl.BlockSpec((SSD_CHUNK, D_SSD), lambda b, c: (b * nc + c, COL_Z // D_SSD)),
            pl.BlockSpec((SSD_CHUNK, LANES), lambda b, c: (b * nc + c, COL_DT // LANES)),
            const((SSD_CONV, SSD_CONV_DIM)),
            const((1, SSD_CONV_DIM)),
            const((1, LANES)),
            const((1, LANES)),
            const((1, D_SSD)),
            const((1, D_SSD)),
            const((LANES, D_SSD)),
        ],
        out_specs=pl.BlockSpec((SSD_CHUNK, D_SSD), lambda b, c: (b * nc + c, 0)),
        scratch_shapes=[pltpu.VMEM((SSD_HEADS // 2, SSD_STATE, LANES), F32),
                        pltpu.VMEM((SSD_CHUNK, D_SSD), F32)],
        compiler_params=_cparams("arbitrary", "arbitrary"),
        name="ssd_scan",
    )(proj, proj, proj, proj, conv_w, conv_b.reshape(1, -1), dtb, alog,
      rep(d_skip), norm_g.reshape(1, -1), expand)


def _dsa_prep_kernel(q_ref, k_ref, v_ref, qi_ref, ki_ref, qg_ref, kg_ref, seg_ref,
                     qn_ref, kn_ref, vt_ref, qib_ref, kib_ref, *, blk):
    def head_norm(x, g):
        ms = _dot_exact(x * x, seg_ref[...]) * (1.0 / ATT_HEAD_DIM)
        return x * lax.rsqrt(ms + NORM_EPS) * g

    qn_ref[...] = (head_norm(q_ref[...], qg_ref[...]) * (LOG2E * ATT_HEAD_DIM ** -0.5)).astype(BF16)
    kn_ref[...] = head_norm(k_ref[...], kg_ref[...]).astype(BF16)
    for c in range(vt_ref.shape[0]):
        vt_ref[c] = v_ref[c * blk:(c + 1) * blk, :].T.astype(BF16)
    qib_ref[...] = qi_ref[...].astype(BF16)
    kib_ref[...] = ki_ref[...].astype(BF16)


def _dsa_prep(proj, q_norm, k_norm, *, tm, blk):
    n = proj.shape[0]
    seg = (jnp.arange(D_ATT)[:, None] // ATT_HEAD_DIM
           == jnp.arange(D_ATT)[None, :] // ATT_HEAD_DIM).astype(F32)
    tile = lambda v: jnp.tile(v, ATT_HEADS).reshape(1, D_ATT)
    col_blk = lambda col, w: pl.BlockSpec((tm, w), lambda i: (i, col // w))
    const = lambda shape: pl.BlockSpec(shape, lambda i: (0,) * len(shape))
    return pl.pallas_call(
        functools.partial(_dsa_prep_kernel, blk=blk),
        out_shape=(
            jax.ShapeDtypeStruct((n, D_ATT), BF16),
            jax.ShapeDtypeStruct((n, D_ATT), BF16),
            jax.ShapeDtypeStruct((n // blk, D_ATT, blk), BF16),
            jax.ShapeDtypeStruct((n, D_ATT), BF16),
            jax.ShapeDtypeStruct((n, LANES), BF16),
        ),
        grid=(n // tm,),
        in_specs=[col_blk(COL_Q, D_ATT), col_blk(COL_K, D_ATT), col_blk(COL_V, D_ATT),
                  col_blk(COL_QIDX, D_ATT), col_blk(COL_KIDX, LANES),
                  const((1, D_ATT)), const((1, D_ATT)), const((D_ATT, D_ATT))],
        out_specs=(
            pl.BlockSpec((tm, D_ATT), lambda i: (i, 0)),
            pl.BlockSpec((tm, D_ATT), lambda i: (i, 0)),
            pl.BlockSpec((tm // blk, D_ATT, blk), lambda i: (i, 0, 0)),
            pl.BlockSpec((tm, D_ATT), lambda i: (i, 0)),
            pl.BlockSpec((tm, LANES), lambda i: (i, 0)),
        ),
        compiler_params=_cparams("arbitrary"),
        name="dsa_prep",
    )(proj, proj, proj, proj, proj, tile(q_norm), tile(k_norm), seg)


def _dsa_kernel(qn_ref, qi_ref, w_ref, kn_ref, vt_ref, ki_ref, bias_ref, o_ref,
                key_ref, hi_ref, lo_ref, big_ref, qm_ref, qim_ref, wt_ref, mb_ref, carry_ref,
                m_ref, l_ref, acc_ref, *, blk, topk):
    i = pl.program_id(1)
    pack = 16
    lane = lax.broadcasted_iota(jnp.int32, (blk, LANES), 1)
    lo_lanes = lane < ATT_HEAD_DIM
    krow = lax.broadcasted_iota(jnp.int32, (blk, blk), 0)
    qcol = lax.broadcasted_iota(jnp.int32, (blk, blk), 1)
    future = krow > qcol

    for p in range(ATT_HEADS // 2):
        sl = slice(p * LANES, (p + 1) * LANES)
        qp = qn_ref[:, sl]
        qip = qi_ref[:, sl]
        zero = jnp.zeros_like(qp)
        qm_ref[2 * p * blk:(2 * p + 1) * blk] = jnp.where(lo_lanes, qp, zero)
        qm_ref[(2 * p + 1) * blk:(2 * p + 2) * blk] = jnp.where(lo_lanes, zero, qp)
        qim_ref[2 * p * blk:(2 * p + 1) * blk] = jnp.where(lo_lanes, qip, zero)
        qim_ref[(2 * p + 1) * blk:(2 * p + 2) * blk] = jnp.where(lo_lanes, zero, qip)
    wt_ref[...] = w_ref[...].T

    def score_tile(j, diag):
        kk = ki_ref[pl.ds(pl.multiple_of(j * blk, blk), blk), :]
        d_all = _dot_nt(kk, qim_ref[...])
        s = jnp.zeros((blk, blk), F32)
        for h in range(IDX_HEADS):
            s = s + jnp.maximum(d_all[:, h * blk:(h + 1) * blk], 0.0) * wt_ref[h:h + 1, :]
        if diag:
            s = jnp.where(future, -jnp.inf, s)
        s = jnp.where(s == 0.0, 0.0, s)
        bits = pltpu.bitcast(s, jnp.int32)
        key = bits ^ ((bits >> 31) & 0x7FFFFFFF)
        key_ref[j] = key
        hi_ref[j] = (key >> 16).astype(jnp.int16)
        lo_ref[j] = ((key & 0xFFFF) - 32768).astype(jnp.int16)

    def score_body(j, carry):
        score_tile(j, False)
        return carry

    lax.fori_loop(0, i, score_body, 0)
    score_tile(i, True)

    one16 = jnp.ones((), jnp.int16)
    zero16 = jnp.zeros((), jnp.int16)

    def count_ge(src_ref, cand):
        cand_b = jnp.broadcast_to(cand.astype(jnp.int16), (blk, blk))

        def add_tile(j, cnt, on):
            ind = jnp.where(src_ref[j] >= cand_b, on, zero16)
            for g in range(blk // pack):
                cnt = cnt + ind[g * pack:(g + 1) * pack]
            return cnt

        def body(t, cnt):
            return add_tile(2 * t + 1, add_tile(2 * t, cnt, one16), one16)

        cnt = lax.fori_loop(0, (i + 1) // 2, body, jnp.zeros((pack, blk), jnp.int16))
        cnt = add_tile(i, cnt, ((i + 1) % 2).astype(jnp.int16))
        return jnp.sum(cnt.astype(jnp.int32).astype(F32), axis=0, keepdims=True)

    def bisect16(src_ref, rank, cnt_all):
        def bit_body(b, st):
            prefix, c_acc, c_rej = st
            cand = prefix + lax.shift_left(jnp.int32(1), 15 - b)
            cnt = count_ge(src_ref, cand)
            ok = cnt >= rank
            return (jnp.where(ok, cand, prefix), jnp.where(ok, cnt, c_acc), jnp.where(ok, c_rej, cnt))
        init = (jnp.full((1, blk), -32768, jnp.int32), cnt_all, jnp.zeros((1, blk), F32))
        return lax.fori_loop(0, 16, bit_body, init)

    ncols = jnp.full((1, blk), ((i + 1) * blk).astype(F32), F32)
    p_hi, c_acc1, c_rej1 = bisect16(hi_ref, jnp.float32(topk), ncols)
    p_hi_b = jnp.broadcast_to(p_hi.astype(jnp.int16), (blk, blk))

    def group_body(j, carry):
        lo_ref[j] = jnp.where(hi_ref[j] == p_hi_b, lo_ref[j], jnp.int16(-32768))
        return carry

    lax.fori_loop(0, i + 1, group_body, 0)
    rank2 = jnp.float32(topk) - c_rej1
    p_lo, c_acc2, c_rej2 = bisect16(lo_ref, rank2, c_acc1 - c_rej1)
    thr = lax.shift_left(p_hi, 16) + (p_lo + 32768)
    need = rank2 - c_rej2
    has_ties = jnp.max((c_acc2 - c_rej2) - need) > 0.5

    m_ref[...] = jnp.full(m_ref.shape, NEG_BIG, F32)
    l_ref[...] = jnp.zeros(l_ref.shape, F32)
    acc_ref[...] = jnp.zeros(acc_ref.shape, F32)
    carry_ref[...] = jnp.zeros(carry_ref.shape, F32)

    def by_parity(x, fn):
        @pl.when(x % 2 == 0)
        def _():
            fn(0)

        @pl.when(x % 2 == 1)
        def _():
            fn(1)

    def qk_dots(j, slot):
        start = pl.multiple_of(j * blk, blk)
        for p in range(ATT_HEADS // 2):
            big_ref[slot, :, 2 * p * blk:(2 * p + 2) * blk] = _dot_nt(
                kn_ref[pl.ds(start, blk), p * LANES:(p + 1) * LANES], qm_ref[2 * p * blk:(2 * p + 2) * blk])

    def tile_mask(j, kind):
        kt = key_ref[j]

        @pl.when(jnp.logical_not(has_ties))
        def _():
            mb = jnp.where(kt >= thr, 0.0, NEG_BIG)
            mb_ref[...] = jnp.where(future, NEG_BIG, mb) if kind == 2 else mb

        @pl.when(has_ties)
        def _():
            eq = jnp.where(kt == thr, 1.0, 0.0)
            lower = jnp.where(krow >= qcol, 1.0, 0.0).astype(BF16)
            seen = _dot(lower, eq.astype(BF16)) + carry_ref[0:1, :]
            keep = jnp.where(seen <= need, 0.0, NEG_BIG)
            mb = jnp.where(kt > thr, 0.0, jnp.where(kt == thr, keep, NEG_BIG))
            mb_ref[...] = jnp.where(future, NEG_BIG, mb) if kind == 2 else mb
            carry_ref[0:1, :] += jnp.sum(eq, axis=0, keepdims=True)

    ones_rows = jnp.ones((pack, blk), BF16)

    def softmax_pv(j, slot, kind):
        vt = vt_ref[j]
        for h in range(ATT_HEADS):
            s = big_ref[slot, :, h * blk:(h + 1) * blk] + mb_ref[...]
            if kind > 0:
                s = s + bias_ref[kind - 1, h]
            m_prev = m_ref[h]
            m_new = jnp.maximum(m_prev, jnp.max(s, axis=0, keepdims=True))
            alpha = jnp.exp2(m_prev - m_new)
            pexp = jnp.exp2(s - m_new[0:1, :]).astype(BF16)
            m_ref[h] = m_new
            ch = slice(h * ATT_HEAD_DIM, (h + 1) * ATT_HEAD_DIM)
            pv = _dot(jnp.concatenate([vt[ch, :], ones_rows], axis=0), pexp)
            l_ref[h] = alpha * l_ref[h] + pv[ATT_HEAD_DIM:ATT_HEAD_DIM + 1, :]
            acc_ref[ch, :] = alpha[0:1, :] * acc_ref[ch, :] + pv[:ATT_HEAD_DIM, :]

    qk_dots(0, 0)

    def attn_body(j, carry):
        tile_mask(j, 0)

        def step(slot):
            qk_dots(j + 1, 1 - slot)
            softmax_pv(j, slot, 0)
        by_parity(j, step)
        return carry

    lax.fori_loop(0, i - 1, attn_body, 0)

    @pl.when(i > 0)
    def _():
        tile_mask(i - 1, 1)

        def step(slot):
            qk_dots(i, 1 - slot)
            softmax_pv(i - 1, slot, 1)
        by_parity(i - 1, step)

    tile_mask(i, 2)
    by_parity(i, lambda slot: softmax_pv(i, slot, 2))

    outs = []
    for h in range(ATT_HEADS):
        ch = slice(h * ATT_HEAD_DIM, (h + 1) * ATT_HEAD_DIM)
        outs.append(acc_ref[ch, :] / l_ref[h][0:1, :])
    o_ref[...] = jnp.concatenate(outs, axis=0).T.astype(BF16)


def _t5_bucket(dist):
    n = jnp.maximum(dist, 0)
    max_exact = REL_BUCKETS // 2
    large = max_exact + (jnp.log(jnp.maximum(n, max_exact).astype(F32) / max_exact)
                         / math.log(REL_MAX_DIST / max_exact)
                         * (REL_BUCKETS - max_exact)).astype(jnp.int32)
    large = jnp.minimum(large, REL_BUCKETS - 1)
    return jnp.where(n < max_exact, n, large)


def _bias_tiles(rel_bias, blk):
    width = 2 * blk
    f = ((rel_bias[_t5_bucket(jnp.arange(width, dtype=jnp.int32))] - rel_bias[REL_BUCKETS - 1]) * LOG2E).T
    h_sub = jnp.concatenate([f[:, blk:], f[:, :blk]], axis=1)
    h = jnp.stack([h_sub, f])
    skew = jnp.tile(h, (1, 1, blk))[:, :, :blk * (width - 1)].reshape(2, ATT_HEADS, blk, width - 1)
    return skew[:, :, :, :blk]


def _dsa(qn, qi, proj, kn, vt, ki, bias, *, bsz, seq, blk):
    n = bsz * seq
    nb = seq // blk
    topk = min(TOPK_MAX, seq // 4)
    once = pl.Buffered(1)
    return pl.pallas_call(
        functools.partial(_dsa_kernel, blk=blk, topk=topk),
        out_shape=jax.ShapeDtypeStruct((n, D_ATT), BF16),
        grid=(bsz, nb),
        in_specs=[
            pl.BlockSpec((blk, D_ATT), lambda b, i: (b * nb + i, 0)),
            pl.BlockSpec((blk, D_ATT), lambda b, i: (b * nb + i, 0)),
            pl.BlockSpec((blk, LANES), lambda b, i: (b * nb + i, COL_WIDX // LANES)),
            pl.BlockSpec((seq, D_ATT), lambda b, i: (b, 0), pipeline_mode=once),
            pl.BlockSpec((nb, D_ATT, blk), lambda b, i: (b, 0, 0), pipeline_mode=once),
            pl.BlockSpec((seq, LANES), lambda b, i: (b, 0), pipeline_mode=once),
            pl.BlockSpec((2, ATT_HEADS, blk, blk), lambda b, i: (0, 0, 0, 0), pipeline_mode=once),
        ],
        out_specs=pl.BlockSpec((blk, D_ATT), lambda b, i: (b * nb + i, 0)),
        scratch_shapes=[
            pltpu.VMEM((nb, blk, blk), jnp.int32),
            pltpu.VMEM((nb, blk, blk), jnp.int16),
            pltpu.VMEM((nb, blk, blk), jnp.int16),
            pltpu.VMEM((2, blk, ATT_HEADS * blk), F32),
            pltpu.VMEM((ATT_HEADS * blk, LANES), BF16),
            pltpu.VMEM((IDX_HEADS * blk, LANES), BF16),
            pltpu.VMEM((LANES, blk), F32),
            pltpu.VMEM((blk, blk), F32),
            pltpu.VMEM((SUBLANES, blk), F32),
            pltpu.VMEM((ATT_HEADS, SUBLANES, blk), F32),
            pltpu.VMEM((ATT_HEADS, SUBLANES, blk), F32),
            pltpu.VMEM((D_ATT, blk), F32),
        ],
        compiler_params=_cparams("arbitrary", "arbitrary"),
        name="dsa_attention",
    )(qn, qi, proj, kn, vt, ki, bias)


def _outproj_kernel(x_ref, yssd_ref, yatt_ref, scb_ref, scc_ref, sch_ref, hc_ref, hh_ref,
                    cw_ref, w_ref, o_ref, cat_ref, *, blocks_per_seq):
    i = pl.program_id(0)

    @pl.when(pl.program_id(1) == 0)
    def _():
        prev = jnp.where(i % blocks_per_seq == 0, 0.0, hc_ref[...] * hh_ref[...])
        ysc = scb_ref[...] * _causal_conv(scc_ref[...] * sch_ref[...], prev, cw_ref[...])
        cat_ref[:, :D_SSD] = yssd_ref[...]
        cat_ref[:, D_SSD:D_SSD + D_ATT] = yatt_ref[...]
        cat_ref[:, D_SSD + D_ATT:] = ysc.astype(BF16)

    o_ref[...] = x_ref[...] + _dot(cat_ref[...], w_ref[...])


def _outproj(x, y_ssd, y_att, proj, sc_w, w_out, *, seq, tm, tn):
    n, d = x.shape
    halo = tm // SUBLANES
    blk = lambda col: pl.BlockSpec((tm, D_SC), lambda i, j: (i, col // D_SC))
    hblk = lambda col: pl.BlockSpec(
        (SUBLANES, D_SC), lambda i, j: (jnp.maximum(i * halo - 1, 0), col // D_SC))
    return pl.pallas_call(
        functools.partial(_outproj_kernel, blocks_per_seq=seq // tm),
        out_shape=jax.ShapeDtypeStruct((n, d), F32),
        grid=(n // tm, d // tn),
        in_specs=[
            pl.BlockSpec((tm, tn), lambda i, j: (i, j)),
            pl.BlockSpec((tm, D_SSD), lambda i, j: (i, 0)),
            pl.BlockSpec((tm, D_ATT), lambda i, j: (i, 0)),
            blk(COL_SCB), blk(COL_SCC), blk(COL_SCH), hblk(COL_SCC), hblk(COL_SCH),
            pl.BlockSpec((SC_CONV, D_SC), lambda i, j: (0, 0)),
            pl.BlockSpec((d, tn), lambda i, j: (0, j)),
        ],
        out_specs=pl.BlockSpec((tm, tn), lambda i, j: (i, j)),
        scratch_shapes=[pltpu.VMEM((tm, d), BF16)],
        compiler_params=_cparams("arbitrary", "arbitrary"),
        name="out_proj",
    )(x, y_ssd, y_att, proj, proj, proj, proj, proj, sc_w, w_out)


def _ffn_kernel(x_ref, g_ref, wg_ref, wu_ref, cw_ref, wd_ref, o_ref, h_ref, acc_ref, halo_ref,
                *, blocks_per_seq):
    i = pl.program_id(0)
    j = pl.program_id(1)
    tm = x_ref.shape[0]

    @pl.when(j == 0)
    def _():
        x = x_ref[...]
        ms = jnp.mean(x * x, axis=-1, keepdims=True)
        h_ref[...] = (x * lax.rsqrt(ms + NORM_EPS) * g_ref[...]).astype(BF16)
        acc_ref[...] = jnp.zeros(acc_ref.shape, F32)

    @pl.when(i % blocks_per_seq == 0)
    def _():
        halo_ref[j] = jnp.zeros(halo_ref.shape[1:], F32)

    h = h_ref[...]
    gate = _dot(h, wg_ref[...])
    up = _dot(h, wu_ref[...])
    prev = halo_ref[j]
    halo_ref[j] = gate[tm - SUBLANES:, :]
    gc = _causal_conv(gate, prev, cw_ref[...])
    act = (gc * _sigmoid(gc) * up).astype(BF16)
    acc_ref[...] += _dot(act, wd_ref[...])

    @pl.when(j == pl.num_programs(1) - 1)
    def _():
        o_ref[...] = x_ref[...] + acc_ref[...]


def _ffn(x, g, wg, wu, cw, wd, *, seq, tm, tf):
    n, d = x.shape
    dff = wg.shape[1]
    return pl.pallas_call(
        functools.partial(_ffn_kernel, blocks_per_seq=seq // tm),
        out_shape=jax.ShapeDtypeStruct((n, d), F32),
        grid=(n // tm, dff // tf),
        in_specs=[
            pl.BlockSpec((tm, d), lambda i, j: (i, 0)),
            pl.BlockSpec((1, d), lambda i, j: (0, 0)),
            pl.BlockSpec((d, tf), lambda i, j: (0, j)),
            pl.BlockSpec((d, tf), lambda i, j: (0, j)),
            pl.BlockSpec((FFN_CONV, tf), lambda i, j: (0, j)),
            pl.BlockSpec((tf, d), lambda i, j: (j, 0)),
        ],
        out_specs=pl.BlockSpec((tm, d), lambda i, j: (i, 0)),
        scratch_shapes=[
            pltpu.VMEM((tm, d), BF16),
            pltpu.VMEM((tm, d), F32),
            pltpu.VMEM((dff // tf, SUBLANES, tf), F32),
        ],
        compiler_params=_cparams("arbitrary", "arbitrary"),
        name="conv_gated_mlp",
    )(x, g, wg, wu, cw, wd)


def _pack_w_in(w_in):
    sizes = (D_SSD, SSD_CONV_DIM, SSD_HEADS, D_ATT, D_ATT, D_ATT,
             IDX_HEADS * IDX_DIM, IDX_DIM, IDX_HEADS, D_SC, D_SC, D_SC)
    offs = [0]
    for s in sizes:
        offs.append(offs[-1] + s)
    w16 = w_in.astype(BF16)
    z, xbc, dt, q, k, v, qi, ki, wi, scb, scc, sch = (
        w16[..., offs[t]:offs[t + 1]] for t in range(len(sizes)))
    padto = lambda a, w: jnp.pad(a, ((0, 0), (0, 0), (0, w - a.shape[-1])))
    parts = [xbc, q, z, k, v, qi, scb, scc, sch, padto(dt, LANES),
             jnp.concatenate([ki, ki], axis=-1), padto(wi, LANES)]
    packed = jnp.concatenate(parts, axis=-1)
    return padto(packed, D_PACKED)


def _forward(x, norm_mix, w_in, ssd_conv_w, ssd_conv_b, ssd_dt_bias, ssd_a_log, ssd_d, ssd_norm,
             att_q_norm, att_k_norm, rel_bias, sc_conv_w, w_out, norm_ffn,
             ffn_w_gate, ffn_w_up, ffn_conv_w, ffn_w_down, *, tm_proj, tn_proj, tm_out, tn_out,
             tm_ffn, tf_ffn, tm_prep, att_blk):
    bsz, seq, d = x.shape
    depth = w_in.shape[0]
    n = bsz * seq
    xf = x.reshape(n, d)
    w_in_p = _pack_w_in(w_in)
    w_out_b = w_out.astype(BF16)
    wg_b = ffn_w_gate.astype(BF16)
    wu_b = ffn_w_up.astype(BF16)
    wd_b = ffn_w_down.astype(BF16)
    bias = _bias_tiles(rel_bias, att_blk)
    for l in range(depth):
        proj = _rms_matmul(xf, norm_mix[l].reshape(1, d), w_in_p[l], tm=tm_proj, tn=tn_proj)
        y_ssd = _ssd(proj, ssd_conv_w[l], ssd_conv_b[l], ssd_dt_bias[l], ssd_a_log[l], ssd_d[l],
                     ssd_norm[l], bsz=bsz, seq=seq)
        qn, kn, vt, qi, ki = _dsa_prep(proj, att_q_norm[l], att_k_norm[l], tm=tm_prep, blk=att_blk)
        y_att = _dsa(qn, qi, proj, kn, vt, ki, bias, bsz=bsz, seq=seq, blk=att_blk)
        xf = _outproj(xf, y_ssd, y_att, proj, sc_conv_w[l], w_out_b[l], seq=seq, tm=tm_out, tn=tn_out)
        xf = _ffn(xf, norm_ffn[l].reshape(1, d), wg_b[l], wu_b[l], ffn_conv_w[l], wd_b[l],
                  seq=seq, tm=tm_ffn, tf=tf_ffn)
    return xf.reshape(bsz, seq, d)


def kernel(x, norm_mix, w_in, ssd_conv_w, ssd_conv_b, ssd_dt_bias, ssd_a_log, ssd_d, ssd_norm, att_q_norm, att_k_norm, rel_bias, sc_conv_w, w_out, norm_ffn, ffn_w_gate, ffn_w_up, ffn_conv_w, ffn_w_down):
    return _forward(x, norm_mix, w_in, ssd_conv_w, ssd_conv_b, ssd_dt_bias, ssd_a_log, ssd_d, ssd_norm,
                    att_q_norm, att_k_norm, rel_bias, sc_conv_w, w_out, norm_ffn,
                    ffn_w_gate, ffn_w_up, ffn_conv_w, ffn_w_down,
                    tm_proj=1024, tn_proj=512, tm_out=512, tn_out=2048,
                    tm_ffn=512, tf_ffn=512, tm_prep=512, att_blk=256)
```

```python
import functools
import math

import jax
import jax.numpy as jnp
from jax import lax
from jax.experimental import pallas as pl
from jax.experimental.pallas import tpu as pltpu

F32 = jnp.float32
BF16 = jnp.bfloat16
HIGHEST = lax.Precision.HIGHEST

LANES = 128
SUBLANES = 8
VMEM_LIMIT_BYTES = 56 * 1024 * 1024

D_MODEL = 2048
D_SSD = 1024
D_ATT = 512
D_SC = 512
SSD_HEAD_DIM = 64
SSD_HEADS = 16
SSD_GROUPS = 2
SSD_STATE = 128
SSD_CONV = 4
SSD_CHUNK = 128
SSD_CONV_DIM = D_SSD + 2 * SSD_GROUPS * SSD_STATE
ATT_HEAD_DIM = 64
ATT_HEADS = 8
IDX_HEADS = 8
IDX_DIM = 64
TOPK_MAX = 256
REL_BUCKETS = 32
REL_MAX_DIST = 128
SC_CONV = 3
D_FF = 5632
FFN_CONV = 3
NORM_EPS = 1e-6

COL_XBC = 0
COL_Q = 1536
COL_Z = 2048
COL_K = 3072
COL_V = 3584
COL_QIDX = 4096
COL_SCB = 4608
COL_SCC = 5120
COL_SCH = 5632
COL_DT = 6144
COL_KIDX = 6272
COL_WIDX = 6400
D_PACKED = 6656

NEG_BIG = -1e30
LOG2E = math.log2(math.e)


def _cparams(*sem):
    return pltpu.CompilerParams(dimension_semantics=sem, vmem_limit_bytes=VMEM_LIMIT_BYTES)


def _dot(a, b):
    return jnp.dot(a, b, preferred_element_type=F32)


def _dot_nt(a, b):
    return lax.dot_general(a, b, (((1,), (1,)), ((), ())), preferred_element_type=F32)


def _dot_exact(a, b):
    return jnp.dot(a, b, preferred_element_type=F32, precision=HIGHEST)


def _expand_exact(x, onehot3):
    x1 = x.astype(BF16)
    r1 = x - x1.astype(F32)
    x2 = r1.astype(BF16)
    x3 = (r1 - x2.astype(F32)).astype(BF16)
    return _dot(jnp.concatenate([x1, x2, x3], axis=1), onehot3)


def _sigmoid(x):
    return 1.0 / (1.0 + jnp.exp(-x))


def _shift_rows(x, prev8, s):
    xr = pltpu.roll(x, s, 0)
    pr = pltpu.roll(prev8, s, 0)
    rows = lax.broadcasted_iota(jnp.int32, (SUBLANES, x.shape[1]), 0)
    top = jnp.where(rows < s, pr, xr[:SUBLANES])
    return jnp.concatenate([top, xr[SUBLANES:]], axis=0)


def _causal_conv(x, prev8, w):
    k = w.shape[0]
    y = w[k - 1:k] * x
    for s in range(1, k):
        y = y + w[k - 1 - s:k - s] * _shift_rows(x, prev8, s)
    return y


def _rms_matmul_kernel(x_ref, g_ref, w_ref, o_ref, h_ref):
    @pl.when(pl.program_id(1) == 0)
    def _():
        x = x_ref[...]
        ms = jnp.mean(x * x, axis=-1, keepdims=True)
        h_ref[...] = (x * lax.rsqrt(ms + NORM_EPS) * g_ref[...]).astype(BF16)

    o_ref[...] = _dot(h_ref[...], w_ref[...])


def _rms_matmul(x, g, w, *, tm, tn):
    n, d = x.shape
    dout = w.shape[1]
    return pl.pallas_call(
        _rms_matmul_kernel,
        out_shape=jax.ShapeDtypeStruct((n, dout), F32),
        grid=(n // tm, dout // tn),
        in_specs=[
            pl.BlockSpec((tm, d), lambda i, j: (i, 0)),
            pl.BlockSpec((1, d), lambda i, j: (0, 0)),
            pl.BlockSpec((d, tn), lambda i, j: (0, j)),
        ],
        out_specs=pl.BlockSpec((tm, tn), lambda i, j: (i, j)),
        scratch_shapes=[pltpu.VMEM((tm, d), BF16)],
        compiler_params=_cparams("arbitrary", "arbitrary"),
        name="rms_in_proj",
    )(x, g, w)


def _ssd_kernel(xbc_ref, halo_ref, z_ref, dt_ref, cw_ref, cb_ref, dtb_ref, alog_ref,
                dx_ref, ng_ref, ex_ref, o_ref, h_ref, y_ref):
    c = pl.program_id(1)
    L = SSD_CHUNK

    @pl.when(c == 0)
    def _():
        h_ref[...] = jnp.zeros(h_ref.shape, F32)

    prev = jnp.where(c == 0, 0.0, halo_ref[...])
    xbc = _causal_conv(xbc_ref[...], prev, cw_ref[...]) + cb_ref[...]
    xbc = xbc * _sigmoid(xbc)
    xs = xbc[:, :D_SSD]

    row = lax.broadcasted_iota(jnp.int32, (L, L), 0)
    col = lax.broadcasted_iota(jnp.int32, (L, L), 1)
    tril = (row >= col).astype(F32)

    def softplus(v):
        return jnp.maximum(v, 0.0) + jnp.log1p(jnp.exp(-jnp.abs(v)))

    dt_s = softplus(dt_ref[...] + dtb_ref[...])
    a_s = dt_s * (-jnp.exp(alog_ref[...]))
    acs_s = _dot_exact(tril, a_s)
    acs_t = acs_s.T
    ex = ex_ref[...]
    dt_x = _expand_exact(dt_s, ex)
    e_x = _expand_exact(jnp.exp(acs_s), ex)
    de_x = _expand_exact(jnp.exp(acs_s[L - 1:L, :] - acs_s), ex)
    decay = e_x[L - 1:L, :]
    xdt = xs * dt_x
    xdt_end = (xdt * de_x).astype(BF16)
    xdt = xdt.astype(BF16)

    lane = lax.broadcasted_iota(jnp.int32, (L, LANES), 1)
    lo = lane < SSD_HEAD_DIM
    lower = row >= col

    for g in range(SSD_GROUPS):
        bm = xbc[:, D_SSD + g * SSD_STATE:D_SSD + (g + 1) * SSD_STATE].astype(BF16)
        cm = xbc[:, D_SSD + (SSD_GROUPS + g) * SSD_STATE:
                 D_SSD + (SSD_GROUPS + g + 1) * SSD_STATE].astype(BF16)
        cb = _dot_nt(cm, bm)
        bm_t = bm.T
        for q in range(4):
            pair = g * 4 + q
            sl = slice(pair * LANES, (pair + 1) * LANES)
            gs = []
            for hh in range(2):
                h = 2 * pair + hh
                seg = (jnp.broadcast_to(acs_s[:, h:h + 1], (L, L))
                       - jnp.broadcast_to(acs_t[h:h + 1, :], (L, L)))
                lm = jnp.where(lower, jnp.exp(seg), 0.0)
                gs.append((cb * lm).astype(BF16))
            gpair = jnp.concatenate(gs, axis=1)
            xp = xdt[:, sl]
            zero = jnp.zeros_like(xp)
            x2 = jnp.concatenate([jnp.where(lo, xp, zero), jnp.where(lo, zero, xp)], axis=0)
            y = _dot(gpair, x2)
            hprev = h_ref[pair]
            y = y + e_x[:, sl] * _dot(cm, hprev.astype(BF16))
            y = y + xs[:, sl] * dx_ref[:, sl]
            h_ref[pair] = hprev * decay[:, sl] + _dot(bm_t, xdt_end[:, sl])
            y_ref[:, sl] = y

    z = z_ref[...]
    y = y_ref[...] * (z * _sigmoid(z))
    half = D_SSD // SSD_GROUPS
    outs = []
    for g in range(SSD_GROUPS):
        yg = y[:, g * half:(g + 1) * half]
        ms = jnp.mean(yg * yg, axis=-1, keepdims=True)
        outs.append(yg * lax.rsqrt(ms + NORM_EPS))
    o_ref[...] = (jnp.concatenate(outs, axis=1) * ng_ref[...]).astype(BF16)


def _ssd(proj, conv_w, conv_b, dt_bias, a_log, d_skip, norm_g, *, bsz, seq):
    nc = seq // SSD_CHUNK
    n = bsz * seq
    pad = LANES - SSD_HEADS
    dtb = jnp.pad(dt_bias, (0, pad)).reshape(1, LANES)
    alog = jnp.pad(a_log, (0, pad), constant_values=NEG_BIG).reshape(1, LANES)
    rep = lambda v: jnp.repeat(v, SSD_HEAD_DIM).reshape(1, D_SSD)
    expand = (jnp.arange(LANES)[:, None] == (jnp.arange(D_SSD)[None, :] // SSD_HEAD_DIM)).astype(BF16)
    expand = jnp.concatenate([expand] * 3, axis=0)
    rows_per_halo = SSD_CHUNK // SUBLANES
    const = lambda shape: pl.BlockSpec(shape, lambda b, c: (0,) * len(shape))
    return pl.pallas_call(
        _ssd_kernel,
        out_shape=jax.ShapeDtypeStruct((n, D_SSD), BF16),
        grid=(bsz, nc),
        in_specs=[
            pl.BlockSpec((SSD_CHUNK, SSD_CONV_DIM), lambda b, c: (b * nc + c, COL_XBC // SSD_CONV_DIM)),
            pl.BlockSpec((SUBLANES, SSD_CONV_DIM),
                         lambda b, c: (jnp.maximum((b * nc + c) * rows_per_halo - 1, 0), 0)),
            pl.BlockSpec((SSD_CHUNK, D_SSD), lambda b, c: (b * nc + c, COL_Z // D_SSD)),
            pl.BlockSpec((SSD_CHUNK, LANES), lambda b, c: (b * nc + c, COL_DT // LANES)),
            const((SSD_CONV, SSD_CONV_DIM)),
            const((1, SSD_CONV_DIM)),
            const((1, LANES)),
            const((1, LANES)),
            const((1, D_SSD)),
            const((1, D_SSD)),
            const((3 * LANES, D_SSD)),
        ],
        out_specs=pl.BlockSpec((SSD_CHUNK, D_SSD), lambda b, c: (b * nc + c, 0)),
        scratch_shapes=[pltpu.VMEM((SSD_HEADS // 2, SSD_STATE, LANES), F32),
                        pltpu.VMEM((SSD_CHUNK, D_SSD), F32)],
        compiler_params=_cparams("arbitrary", "arbitrary"),
        name="ssd_scan",
    )(proj, proj, proj, proj, conv_w, conv_b.reshape(1, -1), dtb, alog,
      rep(d_skip), norm_g.reshape(1, -1), expand)


def _dsa_prep_kernel(q_ref, k_ref, v_ref, qi_ref, ki_ref, qg_ref, kg_ref, seg_ref,
                     qn_ref, kn_ref, vt_ref, qib_ref, kib_ref, *, blk):
    def head_norm(x, g):
        ms = _dot_exact(x * x, seg_ref[...]) * (1.0 / ATT_HEAD_DIM)
        return x * lax.rsqrt(ms + NORM_EPS) * g

    qn_ref[...] = (head_norm(q_ref[...], qg_ref[...]) * (LOG2E * ATT_HEAD_DIM ** -0.5)).astype(BF16)
    kn_ref[...] = head_norm(k_ref[...], kg_ref[...]).astype(BF16)
    for c in range(vt_ref.shape[0]):
        vt_ref[c] = v_ref[c * blk:(c + 1) * blk, :].T.astype(BF16)
    qib_ref[...] = qi_ref[...].astype(BF16)
    kib_ref[...] = ki_ref[...].astype(BF16)


def _dsa_prep(proj, q_norm, k_norm, *, tm, blk):
    n = proj.shape[0]
    seg = (jnp.arange(D_ATT)[:, None] // ATT_HEAD_DIM
           == jnp.arange(D_ATT)[None, :] // ATT_HEAD_DIM).astype(F32)
    tile = lambda v: jnp.tile(v, ATT_HEADS).reshape(1, D_ATT)
    col_blk = lambda col, w: pl.BlockSpec((tm, w), lambda i: (i, col // w))
    const = lambda shape: pl.BlockSpec(shape, lambda i: (0,) * len(shape))
    return pl.pallas_call(
        functools.partial(_dsa_prep_kernel, blk=blk),
        out_shape=(
            jax.ShapeDtypeStruct((n, D_ATT), BF16),
            jax.ShapeDtypeStruct((n, D_ATT), BF16),
            jax.ShapeDtypeStruct((n // blk, D_ATT, blk), BF16),
            jax.ShapeDtypeStruct((n, D_ATT), BF16),
            jax.ShapeDtypeStruct((n, LANES), BF16),
        ),
        grid=(n // tm,),
        in_specs=[col_blk(COL_Q, D_ATT), col_blk(COL_K, D_ATT), col_blk(COL_V, D_ATT),
                  col_blk(COL_QIDX, D_ATT), col_blk(COL_KIDX, LANES),
                  const((1, D_ATT)), const((1, D_ATT)), const((D_ATT, D_ATT))],
        out_specs=(
            pl.BlockSpec((tm, D_ATT), lambda i: (i, 0)),
            pl.BlockSpec((tm, D_ATT), lambda i: (i, 0)),
            pl.BlockSpec((tm // blk, D_ATT, blk), lambda i: (i, 0, 0)),
            pl.BlockSpec((tm, D_ATT), lambda i: (i, 0)),
            pl.BlockSpec((tm, LANES), lambda i: (i, 0)),
        ),
        compiler_params=_cparams("arbitrary"),
        name="dsa_prep",
    )(proj, proj, proj, proj, proj, tile(q_norm), tile(k_norm), seg)


def _dsa_kernel(qn_ref, qi_ref, w_ref, kn_ref, vt_ref, ki_ref, bias_ref, o_ref,
                key_ref, hi_ref, lo_ref, big_ref, qm_ref, qim_ref, wt_ref, mb_ref, carry_ref,
                m_ref, l_ref, acc_ref, *, blk, topk):
    i = pl.program_id(1)
    pack = 16
    lane = lax.broadcasted_iota(jnp.int32, (blk, LANES), 1)
    lo_lanes = lane < ATT_HEAD_DIM
    krow = lax.broadcasted_iota(jnp.int32, (blk, blk), 0)
    qcol = lax.broadcasted_iota(jnp.int32, (blk, blk), 1)
    future = krow > qcol

    for p in range(ATT_HEADS // 2):
        sl = slice(p * LANES, (p + 1) * LANES)
        qp = qn_ref[:, sl]
        qip = qi_ref[:, sl]
        zero = jnp.zeros_like(qp)
        qm_ref[2 * p * blk:(2 * p + 1) * blk] = jnp.where(lo_lanes, qp, zero)
        qm_ref[(2 * p + 1) * blk:(2 * p + 2) * blk] = jnp.where(lo_lanes, zero, qp)
        qim_ref[2 * p * blk:(2 * p + 1) * blk] = jnp.where(lo_lanes, qip, zero)
        qim_ref[(2 * p + 1) * blk:(2 * p + 2) * blk] = jnp.where(lo_lanes, zero, qip)
    wt_ref[...] = w_ref[...].T

    def score_tile(j, diag):
        kk = ki_ref[pl.ds(pl.multiple_of(j * blk, blk), blk), :]
        d_all = _dot_nt(kk, qim_ref[...])
        s = jnp.zeros((blk, blk), F32)
        for h in range(IDX_HEADS):
            s = s + jnp.maximum(d_all[:, h * blk:(h + 1) * blk], 0.0) * wt_ref[h:h + 1, :]
        if diag:
            s = jnp.where(future, -jnp.inf, s)
        s = jnp.where(s == 0.0, 0.0, s)
        bits = pltpu.bitcast(s, jnp.int32)
        key = bits ^ ((bits >> 31) & 0x7FFFFFFF)
        key_ref[j] = key
        hi_ref[j] = (key >> 16).astype(jnp.int16)
        lo_ref[j] = ((key & 0xFFFF) - 32768).astype(jnp.int16)

    def score_body(j, carry):
        score_tile(j, False)
        return carry

    lax.fori_loop(0, i, score_body, 0)
    score_tile(i, True)

    one16 = jnp.ones((), jnp.int16)
    zero16 = jnp.zeros((), jnp.int16)

    def count_ge(src_ref, cand):
        cand_b = jnp.broadcast_to(cand.astype(jnp.int16), (blk, blk))

        def add_tile(j, cnts, on):
            ind = jnp.where(src_ref[j] >= cand_b, on, zero16)
            cnts = list(cnts)
            for g in range(blk // pack):
                cnts[g % len(cnts)] = cnts[g % len(cnts)] + ind[g * pack:(g + 1) * pack]
            return tuple(cnts)

        def body(t, cnts):
            return add_tile(2 * t + 1, add_tile(2 * t, cnts, one16), one16)

        zeros = tuple(jnp.zeros((pack, blk), jnp.int16) for _ in range(4))
        cnts = lax.fori_loop(0, (i + 1) // 2, body, zeros)
        cnts = add_tile(i, cnts, ((i + 1) % 2).astype(jnp.int16))
        cnt = (cnts[0] + cnts[1]) + (cnts[2] + cnts[3])
        return jnp.sum(cnt.astype(jnp.int32).astype(F32), axis=0, keepdims=True)

    def bisect16(src_ref, rank, cnt_all):
        def bit_body(b, st):
            prefix, c_acc, c_rej = st
            cand = prefix + lax.shift_left(jnp.int32(1), 15 - b)
            cnt = count_ge(src_ref, cand)
            ok = cnt >= rank
            return (jnp.where(ok, cand, prefix), jnp.where(ok, cnt, c_acc), jnp.where(ok, c_rej, cnt))
        init = (jnp.full((1, blk), -32768, jnp.int32), cnt_all, jnp.zeros((1, blk), F32))
        return lax.fori_loop(0, 16, bit_body, init)

    ncols = jnp.full((1, blk), ((i + 1) * blk).astype(F32), F32)
    p_hi, c_acc1, c_rej1 = bisect16(hi_ref, jnp.float32(topk), ncols)
    p_hi_b = jnp.broadcast_to(p_hi.astype(jnp.int16), (blk, blk))

    def group_body(j, carry):
        lo_ref[j] = jnp.where(hi_ref[j] == p_hi_b, lo_ref[j], jnp.int16(-32768))
        return carry

    lax.fori_loop(0, i + 1, group_body, 0)
    rank2 = jnp.float32(topk) - c_rej1
    p_lo, c_acc2, c_rej2 = bisect16(lo_ref, rank2, c_acc1 - c_rej1)
    thr = lax.shift_left(p_hi, 16) + (p_lo + 32768)
    need = rank2 - c_rej2
    has_ties = jnp.max((c_acc2 - c_rej2) - need) > 0.5

    m_ref[...] = jnp.full(m_ref.shape, NEG_BIG, F32)
    l_ref[...] = jnp.zeros(l_ref.shape, F32)
    acc_ref[...] = jnp.zeros(acc_ref.shape, F32)
    carry_ref[...] = jnp.zeros(carry_ref.shape, F32)

    def by_parity(x, fn):
        @pl.when(x % 2 == 0)
        def _():
            fn(0)

        @pl.when(x % 2 == 1)
        def _():
            fn(1)

    def qk_dots(j, slot):
        start = pl.multiple_of(j * blk, blk)
        for p in range(ATT_HEADS // 2):
            big_ref[slot, :, 2 * p * blk:(2 * p + 2) * blk] = _dot_nt(
                kn_ref[pl.ds(start, blk), p * LANES:(p + 1) * LANES], qm_ref[2 * p * blk:(2 * p + 2) * blk])

    def tile_mask(j, kind):
        kt = key_ref[j]

        @pl.when(jnp.logical_not(has_ties))
        def _():
            mb = jnp.where(kt >= thr, 0.0, NEG_BIG)
            mb_ref[...] = jnp.where(future, NEG_BIG, mb) if kind == 2 else mb

        @pl.when(has_ties)
        def _():
            eq = jnp.where(kt == thr, 1.0, 0.0)
            lower = jnp.where(krow >= qcol, 1.0, 0.0).astype(BF16)
            seen = _dot(lower, eq.astype(BF16)) + carry_ref[0:1, :]
            keep = jnp.where(seen <= need, 0.0, NEG_BIG)
            mb = jnp.where(kt > thr, 0.0, jnp.where(kt == thr, keep, NEG_BIG))
            mb_ref[...] = jnp.where(future, NEG_BIG, mb) if kind == 2 else mb
            carry_ref[0:1, :] += jnp.sum(eq, axis=0, keepdims=True)

    ones_rows = jnp.ones((pack, blk), BF16)

    def softmax_pv(j, slot, kind):
        vt = vt_ref[j]
        for h in range(ATT_HEADS):
            s = big_ref[slot, :, h * blk:(h + 1) * blk] + mb_ref[...]
            if kind > 0:
                s = s + bias_ref[kind - 1, h]
            m_prev = m_ref[h]
            m_new = jnp.maximum(m_prev, jnp.max(s, axis=0, keepdims=True))
            alpha = jnp.exp2(m_prev - m_new)
            pexp = jnp.exp2(s - m_new[0:1, :]).astype(BF16)
            m_ref[h] = m_new
            ch = slice(h * ATT_HEAD_DIM, (h + 1) * ATT_HEAD_DIM)
            pv = _dot(jnp.concatenate([vt[ch, :], ones_rows], axis=0), pexp)
            l_ref[h] = alpha * l_ref[h] + pv[ATT_HEAD_DIM:ATT_HEAD_DIM + 1, :]
            acc_ref[ch, :] = alpha[0:1, :] * acc_ref[ch, :] + pv[:ATT_HEAD_DIM, :]

    qk_dots(0, 0)

    def attn_body(j, carry):
        tile_mask(j, 0)

        def step(slot):
            qk_dots(j + 1, 1 - slot)
            softmax_pv(j, slot, 0)
        by_parity(j, step)
        return carry

    lax.fori_loop(0, i - 1, attn_body, 0)

    @pl.when(i > 0)
    def _():
        tile_mask(i - 1, 1)

        def step(slot):
            qk_dots(i, 1 - slot)
            softmax_pv(i - 1, slot, 1)
        by_parity(i - 1, step)

    tile_mask(i, 2)
    by_parity(i, lambda slot: softmax_pv(i, slot, 2))

    outs = []
    for h in range(ATT_HEADS):
        ch = slice(h * ATT_HEAD_DIM, (h + 1) * ATT_HEAD_DIM)
        outs.append(acc_ref[ch, :] / l_ref[h][0:1, :])
    o_ref[...] = jnp.concatenate(outs, axis=0).T.astype(BF16)


def _t5_bucket(dist):
    n = jnp.maximum(dist, 0)
    max_exact = REL_BUCKETS // 2
    large = max_exact + (jnp.log(jnp.maximum(n, max_exact).astype(F32) / max_exact)
                         / math.log(REL_MAX_DIST / max_exact)
                         * (REL_BUCKETS - max_exact)).astype(jnp.int32)
    large = jnp.minimum(large, REL_BUCKETS - 1)
    return jnp.where(n < max_exact, n, large)


def _bias_tiles(rel_bias, blk):
    width = 2 * blk
    f = ((rel_bias[_t5_bucket(jnp.arange(width, dtype=jnp.int32))] - rel_bias[REL_BUCKETS - 1]) * LOG2E).T
    h_sub = jnp.concatenate([f[:, blk:], f[:, :blk]], axis=1)
    h = jnp.stack([h_sub, f])
    skew = jnp.tile(h, (1, 1, blk))[:, :, :blk * (width - 1)].reshape(2, ATT_HEADS, blk, width - 1)
    return skew[:, :, :, :blk]


def _dsa(qn, qi, proj, kn, vt, ki, bias, *, bsz, seq, blk):
    n = bsz * seq
    nb = seq // blk
    topk = min(TOPK_MAX, seq // 4)
    once = pl.Buffered(1)
    return pl.pallas_call(
        functools.partial(_dsa_kernel, blk=blk, topk=topk),
        out_shape=jax.ShapeDtypeStruct((n, D_ATT), BF16),
        grid=(bsz, nb),
        in_specs=[
            pl.BlockSpec((blk, D_ATT), lambda b, i: (b * nb + i, 0)),
            pl.BlockSpec((blk, D_ATT), lambda b, i: (b * nb + i, 0)),
            pl.BlockSpec((blk, LANES), lambda b, i: (b * nb + i, COL_WIDX // LANES)),
            pl.BlockSpec((seq, D_ATT), lambda b, i: (b, 0), pipeline_mode=once),
            pl.BlockSpec((nb, D_ATT, blk), lambda b, i: (b, 0, 0), pipeline_mode=once),
            pl.BlockSpec((seq, LANES), lambda b, i: (b, 0), pipeline_mode=once),
            pl.BlockSpec((2, ATT_HEADS, blk, blk), lambda b, i: (0, 0, 0, 0), pipeline_mode=once),
        ],
        out_specs=pl.BlockSpec((blk, D_ATT), lambda b, i: (b * nb + i, 0)),
        scratch_shapes=[
            pltpu.VMEM((nb, blk, blk), jnp.int32),
            pltpu.VMEM((nb, blk, blk), jnp.int16),
            pltpu.VMEM((nb, blk, blk), jnp.int16),
            pltpu.VMEM((2, blk, ATT_HEADS * blk), F32),
            pltpu.VMEM((ATT_HEADS * blk, LANES), BF16),
            pltpu.VMEM((IDX_HEADS * blk, LANES), BF16),
            pltpu.VMEM((LANES, blk), F32),
            pltpu.VMEM((blk, blk), F32),
            pltpu.VMEM((SUBLANES, blk), F32),
            pltpu.VMEM((ATT_HEADS, SUBLANES, blk), F32),
            pltpu.VMEM((ATT_HEADS, SUBLANES, blk), F32),
            pltpu.VMEM((D_ATT, blk), F32),
        ],
        compiler_params=_cparams("arbitrary", "arbitrary"),
        name="dsa_attention",
    )(qn, qi, proj, kn, vt, ki, bias)


def _outproj_kernel(x_ref, yssd_ref, yatt_ref, scb_ref, scc_ref, sch_ref, hc_ref, hh_ref,
                    cw_ref, w_ref, o_ref, cat_ref, *, blocks_per_seq):
    i = pl.program_id(0)

    @pl.when(pl.program_id(1) == 0)
    def _():
        prev = jnp.where(i % blocks_per_seq == 0, 0.0, hc_ref[...] * hh_ref[...])
        ysc = scb_ref[...] * _causal_conv(scc_ref[...] * sch_ref[...], prev, cw_ref[...])
        cat_ref[:, :D_SSD] = yssd_ref[...]
        cat_ref[:, D_SSD:D_SSD + D_ATT] = yatt_ref[...]
        cat_ref[:, D_SSD + D_ATT:] = ysc.astype(BF16)

    o_ref[...] = x_ref[...] + _dot(cat_ref[...], w_ref[...])


def _outproj(x, y_ssd, y_att, proj, sc_w, w_out, *, seq, tm, tn):
    n, d = x.shape
    halo = tm // SUBLANES
    blk = lambda col: pl.BlockSpec((tm, D_SC), lambda i, j: (i, col // D_SC))
    hblk = lambda col: pl.BlockSpec(
        (SUBLANES, D_SC), lambda i, j: (jnp.maximum(i * halo - 1, 0), col // D_SC))
    return pl.pallas_call(
        functools.partial(_outproj_kernel, blocks_per_seq=seq // tm),
        out_shape=jax.ShapeDtypeStruct((n, d), F32),
        grid=(n // tm, d // tn),
        in_specs=[
            pl.BlockSpec((tm, tn), lambda i, j: (i, j)),
            pl.BlockSpec((tm, D_SSD), lambda i, j: (i, 0)),
            pl.BlockSpec((tm, D_ATT), lambda i, j: (i, 0)),
            blk(COL_SCB), blk(COL_SCC), blk(COL_SCH), hblk(COL_SCC), hblk(COL_SCH),
            pl.BlockSpec((SC_CONV, D_SC), lambda i, j: (0, 0)),
            pl.BlockSpec((d, tn), lambda i, j: (0, j)),
        ],
        out_specs=pl.BlockSpec((tm, tn), lambda i, j: (i, j)),
        scratch_shapes=[pltpu.VMEM((tm, d), BF16)],
        compiler_params=_cparams("arbitrary", "arbitrary"),
        name="out_proj",
    )(x, y_ssd, y_att, proj, proj, proj, proj, proj, sc_w, w_out)


def _ffn_kernel(x_ref, g_ref, wg_ref, wu_ref, cw_ref, wd_ref, o_ref, h_ref, acc_ref, halo_ref,
                *, blocks_per_seq):
    i = pl.program_id(0)
    j = pl.program_id(1)
    tm = x_ref.shape[0]

    @pl.when(j == 0)
    def _():
        x = x_ref[...]
        ms = jnp.mean(x * x, axis=-1, keepdims=True)
        h_ref[...] = (x * lax.rsqrt(ms + NORM_EPS) * g_ref[...]).astype(BF16)
        acc_ref[...] = jnp.zeros(acc_ref.shape, F32)

    @pl.when(i % blocks_per_seq == 0)
    def _():
        halo_ref[j] = jnp.zeros(halo_ref.shape[1:], F32)

    h = h_ref[...]
    gate = _dot(h, wg_ref[...])
    up = _dot(h, wu_ref[...])
    prev = halo_ref[j]
    halo_ref[j] = gate[tm - SUBLANES:, :]
    gc = _causal_conv(gate, prev, cw_ref[...])
    act = (gc * _sigmoid(gc) * up).astype(BF16)
    acc_ref[...] += _dot(act, wd_ref[...])

    @pl.when(j == pl.num_programs(1) - 1)
    def _():
        o_ref[...] = x_ref[...] + acc_ref[...]


def _ffn(x, g, wg, wu, cw, wd, *, seq, tm, tf):
    n, d = x.shape
    dff = wg.shape[1]
    return pl.pallas_call(
        functools.partial(_ffn_kernel, blocks_per_seq=seq // tm),
        out_shape=jax.ShapeDtypeStruct((n, d), F32),
        grid=(n // tm, dff // tf),
        in_specs=[
            pl.BlockSpec((tm, d), lambda i, j: (i, 0)),
            pl.BlockSpec((1, d), lambda i, j: (0, 0)),
            pl.BlockSpec((d, tf), lambda i, j: (0, j)),
            pl.BlockSpec((d, tf), lambda i, j: (0, j)),
            pl.BlockSpec((FFN_CONV, tf), lambda i, j: (0, j)),
            pl.BlockSpec((tf, d), lambda i, j: (j, 0)),
        ],
        out_specs=pl.BlockSpec((tm, d), lambda i, j: (i, 0)),
        scratch_shapes=[
            pltpu.VMEM((tm, d), BF16),
            pltpu.VMEM((tm, d), F32),
            pltpu.VMEM((dff // tf, SUBLANES, tf), F32),
        ],
        compiler_params=_cparams("arbitrary", "arbitrary"),
        name="conv_gated_mlp",
    )(x, g, wg, wu, cw, wd)


def _pack_w_in(w_in):
    sizes = (D_SSD, SSD_CONV_DIM, SSD_HEADS, D_ATT, D_ATT, D_ATT,
             IDX_HEADS * IDX_DIM, IDX_DIM, IDX_HEADS, D_SC, D_SC, D_SC)
    offs = [0]
    for s in sizes:
        offs.append(offs[-1] + s)
    w16 = w_in.astype(BF16)
    z, xbc, dt, q, k, v, qi, ki, wi, scb, scc, sch = (
        w16[..., offs[t]:offs[t + 1]] for t in range(len(sizes)))
    padto = lambda a, w: jnp.pad(a, ((0, 0), (0, 0), (0, w - a.shape[-1])))
    parts = [xbc, q, z, k, v, qi, scb, scc, sch, padto(dt, LANES),
             jnp.concatenate([ki, ki], axis=-1), padto(wi, LANES)]
    packed = jnp.concatenate(parts, axis=-1)
    return padto(packed, D_PACKED)


def _forward(x, norm_mix, w_in, ssd_conv_w, ssd_conv_b, ssd_dt_bias, ssd_a_log, ssd_d, ssd_norm,
             att_q_norm, att_k_norm, rel_bias, sc_conv_w, w_out, norm_ffn,
             ffn_w_gate, ffn_w_up, ffn_conv_w, ffn_w_down, *, tm_proj, tn_proj, tm_out, tn_out,
             tm_ffn, tf_ffn, tm_prep, att_blk):
    bsz, seq, d = x.shape
    depth = w_in.shape[0]
    n = bsz * seq
    xf = x.reshape(n, d)
    w_in_p = _pack_w_in(w_in)
    w_out_b = w_out.astype(BF16)
    wg_b = ffn_w_gate.astype(BF16)
    wu_b = ffn_w_up.astype(BF16)
    wd_b = ffn_w_down.astype(BF16)
    bias = _bias_tiles(rel_bias, att_blk)
    for l in range(depth):
        proj = _rms_matmul(xf, norm_mix[l].reshape(1, d), w_in_p[l], tm=tm_proj, tn=tn_proj)
        y_ssd = _ssd(proj, ssd_conv_w[l], ssd_conv_b[l], ssd_dt_bias[l], ssd_a_log[l], ssd_d[l],
                     ssd_norm[l], bsz=bsz, seq=seq)
        qn, kn, vt, qi, ki = _dsa_prep(proj, att_q_norm[l], att_k_norm[l], tm=tm_prep, blk=att_blk)
        y_att = _dsa(qn, qi, proj, kn, vt, ki, bias, bsz=bsz, seq=seq, blk=att_blk)
        xf = _outproj(xf, y_ssd, y_att, proj, sc_conv_w[l], w_out_b[l], seq=seq, tm=tm_out, tn=tn_out)
        xf = _ffn(xf, norm_ffn[l].reshape(1, d), wg_b[l], wu_b[l], ffn_conv_w[l], wd_b[l],
                  seq=seq, tm=tm_ffn, tf=tf_ffn)
    return xf.reshape(bsz, seq, d)


def kernel(x, norm_mix, w_in, ssd_conv_w, ssd_conv_b, ssd_dt_bias, ssd_a_log, ssd_d, ssd_norm, att_q_norm, att_k_norm, rel_bias, sc_conv_w, w_out, norm_ffn, ffn_w_gate, ffn_w_up, ffn_conv_w, ffn_w_down):
    return _forward(x, norm_mix, w_in, ssd_conv_w, ssd_conv_b, ssd_dt_bias, ssd_a_log, ssd_d, ssd_norm,
                    att_q_norm, att_k_norm, rel_bias, sc_conv_w, w_out, norm_ffn,
                    ffn_w_gate, ffn_w_up, ffn_conv_w, ffn_w_down,
                    tm_proj=1024, tn_proj=512, tm_out=512, tn_out=2048,
                    tm_ffn=512, tf_ffn=512, tm_prep=512, att_blk=256)
```

```python
import functools
import math

import jax
import jax.numpy as jnp
from jax import lax
from jax.experimental import pallas as pl
from jax.experimental.pallas import tpu as pltpu

F32 = jnp.float32
BF16 = jnp.bfloat16
HIGHEST = lax.Precision.HIGHEST

LANES = 128
SUBLANES = 8
VMEM_LIMIT_BYTES = 56 * 1024 * 1024

D_MODEL = 2048
D_SSD = 1024
D_ATT = 512
D_SC = 512
SSD_HEAD_DIM = 64
SSD_HEADS = 16
SSD_GROUPS = 2
SSD_STATE = 128
SSD_CONV = 4
SSD_CHUNK = 128
SSD_CONV_DIM = D_SSD + 2 * SSD_GROUPS * SSD_STATE
ATT_HEAD_DIM = 64
ATT_HEADS = 8
IDX_HEADS = 8
IDX_DIM = 64
TOPK_MAX = 256
REL_BUCKETS = 32
REL_MAX_DIST = 128
SC_CONV = 3
D_FF = 5632
FFN_CONV = 3
NORM_EPS = 1e-6

COL_XBC = 0
COL_Q = 1536
COL_Z = 2048
COL_K = 3072
COL_V = 3584
COL_QIDX = 4096
COL_SCB = 4608
COL_SCC = 5120
COL_SCH = 5632
COL_DT = 6144
COL_KIDX = 6272
COL_WIDX = 6400
D_PACKED = 6656

NEG_BIG = -1e30
LOG2E = math.log2(math.e)


def _cparams(*sem):
    return pltpu.CompilerParams(dimension_semantics=sem, vmem_limit_bytes=VMEM_LIMIT_BYTES)


def _dot(a, b):
    return jnp.dot(a, b, preferred_element_type=F32)


def _dot_nt(a, b):
    return lax.dot_general(a, b, (((1,), (1,)), ((), ())), preferred_element_type=F32)


def _dot_exact(a, b):
    return jnp.dot(a, b, preferred_element_type=F32, precision=HIGHEST)


def _expand_exact(x, onehot3):
    x1 = x.astype(BF16)
    r1 = x - x1.astype(F32)
    x2 = r1.astype(BF16)
    x3 = (r1 - x2.astype(F32)).astype(BF16)
    return _dot(jnp.concatenate([x1, x2, x3], axis=1), onehot3)


def _sigmoid(x):
    return 1.0 / (1.0 + jnp.exp(-x))


def _shift_rows(x, prev8, s):
    xr = pltpu.roll(x, s, 0)
    pr = pltpu.roll(prev8, s, 0)
    rows = lax.broadcasted_iota(jnp.int32, (SUBLANES, x.shape[1]), 0)
    top = jnp.where(rows < s, pr, xr[:SUBLANES])
    return jnp.concatenate([top, xr[SUBLANES:]], axis=0)


def _causal_conv(x, prev8, w):
    k = w.shape[0]
    y = w[k - 1:k] * x
    for s in range(1, k):
        y = y + w[k - 1 - s:k - s] * _shift_rows(x, prev8, s)
    return y


def _rms_matmul_kernel(x_ref, g_ref, w_ref, o_ref, h_ref):
    @pl.when(pl.program_id(1) == 0)
    def _():
        x = x_ref[...]
        ms = jnp.mean(x * x, axis=-1, keepdims=True)
        h_ref[...] = (x * lax.rsqrt(ms + NORM_EPS) * g_ref[...]).astype(BF16)

    o_ref[...] = _dot(h_ref[...], w_ref[...])


def _rms_matmul(x, g, w, *, tm, tn):
    n, d = x.shape
    dout = w.shape[1]
    return pl.pallas_call(
        _rms_matmul_kernel,
        out_shape=jax.ShapeDtypeStruct((n, dout), F32),
        grid=(n // tm, dout // tn),
        in_specs=[
            pl.BlockSpec((tm, d), lambda i, j: (i, 0)),
            pl.BlockSpec((1, d), lambda i, j: (0, 0)),
            pl.BlockSpec((d, tn), lambda i, j: (0, j)),
        ],
        out_specs=pl.BlockSpec((tm, tn), lambda i, j: (i, j)),
        scratch_shapes=[pltpu.VMEM((tm, d), BF16)],
        compiler_params=_cparams("arbitrary", "arbitrary"),
        name="rms_in_proj",
    )(x, g, w)


def _ssd_kernel(xbc_ref, halo_ref, z_ref, dt_ref, cw_ref, cb_ref, dtb_ref, alog_ref,
                dx_ref, ng_ref, ex_ref, o_ref, h_ref, y_ref):
    c = pl.program_id(1)
    L = SSD_CHUNK

    @pl.when(c == 0)
    def _():
        h_ref[...] = jnp.zeros(h_ref.shape, F32)

    prev = jnp.where(c == 0, 0.0, halo_ref[...])
    xbc = _causal_conv(xbc_ref[...], prev, cw_ref[...]) + cb_ref[...]
    xbc = xbc * _sigmoid(xbc)
    xs = xbc[:, :D_SSD]

    row = lax.broadcasted_iota(jnp.int32, (L, L), 0)
    col = lax.broadcasted_iota(jnp.int32, (L, L), 1)
    tril = (row >= col).astype(F32)

    def softplus(v):
        return jnp.maximum(v, 0.0) + jnp.log1p(jnp.exp(-jnp.abs(v)))

    dt_s = softplus(dt_ref[...] + dtb_ref[...])
    a_s = dt_s * (-jnp.exp(alog_ref[...]))
    acs_s = _dot_exact(tril, a_s)
    acs_t = acs_s.T
    ex = ex_ref[...]
    dt_x = _expand_exact(dt_s, ex)
    e_x = _expand_exact(jnp.exp(acs_s), ex)
    de_x = _expand_exact(jnp.exp(acs_s[L - 1:L, :] - acs_s), ex)
    decay = e_x[L - 1:L, :]
    xdt = xs * dt_x
    xdt_end = (xdt * de_x).astype(BF16)
    xdt = xdt.astype(BF16)

    lane = lax.broadcasted_iota(jnp.int32, (L, LANES), 1)
    lo = lane < SSD_HEAD_DIM
    lower = row >= col

    for g in range(SSD_GROUPS):
        bm = xbc[:, D_SSD + g * SSD_STATE:D_SSD + (g + 1) * SSD_STATE].astype(BF16)
        cm = xbc[:, D_SSD + (SSD_GROUPS + g) * SSD_STATE:
                 D_SSD + (SSD_GROUPS + g + 1) * SSD_STATE].astype(BF16)
        cb = _dot_nt(cm, bm)
        bm_t = bm.T
        for q in range(4):
            pair = g * 4 + q
            sl = slice(pair * LANES, (pair + 1) * LANES)
            gs = []
            for hh in range(2):
                h = 2 * pair + hh
                seg = (jnp.broadcast_to(acs_s[:, h:h + 1], (L, L))
                       - jnp.broadcast_to(acs_t[h:h + 1, :], (L, L)))
                lm = jnp.where(lower, jnp.exp(seg), 0.0)
                gs.append((cb * lm).astype(BF16))
            gpair = jnp.concatenate(gs, axis=1)
            xp = xdt[:, sl]
            zero = jnp.zeros_like(xp)
            x2 = jnp.concatenate([jnp.where(lo, xp, zero), jnp.where(lo, zero, xp)], axis=0)
            y = _dot(gpair, x2)
            hprev = h_ref[pair]
            y = y + e_x[:, sl] * _dot(cm, hprev.astype(BF16))
            y = y + xs[:, sl] * dx_ref[:, sl]
            h_ref[pair] = hprev * decay[:, sl] + _dot(bm_t, xdt_end[:, sl])
            y_ref[:, sl] = y

    z = z_ref[...]
    y = y_ref[...] * (z * _sigmoid(z))
    half = D_SSD // SSD_GROUPS
    outs = []
    for g in range(SSD_GROUPS):
        yg = y[:, g * half:(g + 1) * half]
        ms = jnp.mean(yg * yg, axis=-1, keepdims=True)
        outs.append(yg * lax.rsqrt(ms + NORM_EPS))
    o_ref[...] = (jnp.concatenate(outs, axis=1) * ng_ref[...]).astype(BF16)


def _ssd(proj, conv_w, conv_b, dt_bias, a_log, d_skip, norm_g, *, bsz, seq):
    nc = seq // SSD_CHUNK
    n = bsz * seq
    pad = LANES - SSD_HEADS
    dtb = jnp.pad(dt_bias, (0, pad)).reshape(1, LANES)
    alog = jnp.pad(a_log, (0, pad), constant_values=NEG_BIG).reshape(1, LANES)
    rep = lambda v: jnp.repeat(v, SSD_HEAD_DIM).reshape(1, D_SSD)
    expand = (jnp.arange(LANES)[:, None] == (jnp.arange(D_SSD)[None, :] // SSD_HEAD_DIM)).astype(BF16)
    expand = jnp.concatenate([expand] * 3, axis=0)
    rows_per_halo = SSD_CHUNK // SUBLANES
    const = lambda shape: pl.BlockSpec(shape, lambda b, c: (0,) * len(shape))
    return pl.pallas_call(
        _ssd_kernel,
        out_shape=jax.ShapeDtypeStruct((n, D_SSD), BF16),
        grid=(bsz, nc),
        in_specs=[
            pl.BlockSpec((SSD_CHUNK, SSD_CONV_DIM), lambda b, c: (b * nc + c, COL_XBC // SSD_CONV_DIM)),
            pl.BlockSpec((SUBLANES, SSD_CONV_DIM),
                         lambda b, c: (jnp.maximum((b * nc + c) * rows_per_halo - 1, 0), 0)),
            pl.BlockSpec((SSD_CHUNK, D_SSD), lambda b, c: (b * nc + c, COL_Z // D_SSD)),
            pl.BlockSpec((SSD_CHUNK, LANES), lambda b, c: (b * nc + c, COL_DT // LANES)),
            const((SSD_CONV, SSD_CONV_DIM)),
            const((1, SSD_CONV_DIM)),
            const((1, LANES)),
            const((1, LANES)),
            const((1, D_SSD)),
            const((1, D_SSD)),
            const((3 * LANES, D_SSD)),
        ],
        out_specs=pl.BlockSpec((SSD_CHUNK, D_SSD), lambda b, c: (b * nc + c, 0)),
        scratch_shapes=[pltpu.VMEM((SSD_HEADS // 2, SSD_STATE, LANES), F32),
                        pltpu.VMEM((SSD_CHUNK, D_SSD), F32)],
        compiler_params=_cparams("arbitrary", "arbitrary"),
        name="ssd_scan",
    )(proj, proj, proj, proj, conv_w, conv_b.reshape(1, -1), dtb, alog,
      rep(d_skip), norm_g.reshape(1, -1), expand)


def _dsa_prep_kernel(q_ref, k_ref, v_ref, qi_ref, ki_ref, qg_ref, kg_ref, seg_ref,
                     qn_ref, kn_ref, vt_ref, qib_ref, kib_ref, *, blk):
    def head_norm(x, g):
        ms = _expand_exact(x * x, seg_ref[...]) * (1.0 / ATT_HEAD_DIM)
        return x * lax.rsqrt(ms + NORM_EPS) * g

    qn_ref[...] = (head_norm(q_ref[...], qg_ref[...]) * (LOG2E * ATT_HEAD_DIM ** -0.5)).astype(BF16)
    kn_ref[...] = head_norm(k_ref[...], kg_ref[...]).astype(BF16)
    for c in range(vt_ref.shape[0]):
        vt_ref[c] = v_ref[c * blk:(c + 1) * blk, :].T.astype(BF16)
    qib_ref[...] = qi_ref[...].astype(BF16)
    kib_ref[...] = ki_ref[...].astype(BF16)


def _dsa_prep(proj, q_norm, k_norm, *, tm, blk):
    n = proj.shape[0]
    seg = (jnp.arange(D_ATT)[:, None] // ATT_HEAD_DIM
           == jnp.arange(D_ATT)[None, :] // ATT_HEAD_DIM).astype(BF16)
    seg = jnp.concatenate([seg] * 3, axis=0)
    tile = lambda v: jnp.tile(v, ATT_HEADS).reshape(1, D_ATT)
    col_blk = lambda col, w: pl.BlockSpec((tm, w), lambda i: (i, col // w))
    const = lambda shape: pl.BlockSpec(shape, lambda i: (0,) * len(shape))
    return pl.pallas_call(
        functools.partial(_dsa_prep_kernel, blk=blk),
        out_shape=(
            jax.ShapeDtypeStruct((n, D_ATT), BF16),
            jax.ShapeDtypeStruct((n, D_ATT), BF16),
            jax.ShapeDtypeStruct((n // blk, D_ATT, blk), BF16),
            jax.ShapeDtypeStruct((n, D_ATT), BF16),
            jax.ShapeDtypeStruct((n, LANES), BF16),
        ),
        grid=(n // tm,),
        in_specs=[col_blk(COL_Q, D_ATT), col_blk(COL_K, D_ATT), col_blk(COL_V, D_ATT),
                  col_blk(COL_QIDX, D_ATT), col_blk(COL_KIDX, LANES),
                  const((1, D_ATT)), const((1, D_ATT)), const((3 * D_ATT, D_ATT))],
        out_specs=(
            pl.BlockSpec((tm, D_ATT), lambda i: (i, 0)),
            pl.BlockSpec((tm, D_ATT), lambda i: (i, 0)),
            pl.BlockSpec((tm // blk, D_ATT, blk), lambda i: (i, 0, 0)),
            pl.BlockSpec((tm, D_ATT), lambda i: (i, 0)),
            pl.BlockSpec((tm, LANES), lambda i: (i, 0)),
        ),
        compiler_params=_cparams("arbitrary"),
        name="dsa_prep",
    )(proj, proj, proj, proj, proj, tile(q_norm), tile(k_norm), seg)


def _dsa_kernel(qn_ref, qi_ref, w_ref, kn_ref, vt_ref, ki_ref, bias_ref, o_ref,
                key_ref, hi_ref, lo_ref, big_ref, qm_ref, qim_ref, wt_ref, mb_ref, carry_ref,
                m_ref, l_ref, acc_ref, *, blk, topk):
    i = pl.program_id(1)
    pack = 16
    lane = lax.broadcasted_iota(jnp.int32, (blk, LANES), 1)
    lo_lanes = lane < ATT_HEAD_DIM
    krow = lax.broadcasted_iota(jnp.int32, (blk, blk), 0)
    qcol = lax.broadcasted_iota(jnp.int32, (blk, blk), 1)
    future = krow > qcol

    for p in range(ATT_HEADS // 2):
        sl = slice(p * LANES, (p + 1) * LANES)
        qp = qn_ref[:, sl]
        qip = qi_ref[:, sl]
        zero = jnp.zeros_like(qp)
        qm_ref[2 * p * blk:(2 * p + 1) * blk] = jnp.where(lo_lanes, qp, zero)
        qm_ref[(2 * p + 1) * blk:(2 * p + 2) * blk] = jnp.where(lo_lanes, zero, qp)
        qim_ref[2 * p * blk:(2 * p + 1) * blk] = jnp.where(lo_lanes, qip, zero)
        qim_ref[(2 * p + 1) * blk:(2 * p + 2) * blk] = jnp.where(lo_lanes, zero, qip)
    wt_ref[...] = w_ref[...].T

    def score_dots(j, slot):
        kk = ki_ref[pl.ds(pl.multiple_of(j * blk, blk), blk), :]
        big_ref[slot] = _dot_nt(kk, qim_ref[...])

    def score_finish(j, slot, diag):
        s = jnp.zeros((blk, blk), F32)
        for h in range(IDX_HEADS):
            s = s + jnp.maximum(big_ref[slot, :, h * blk:(h + 1) * blk], 0.0) * wt_ref[h:h + 1, :]
        if diag:
            s = jnp.where(future, -jnp.inf, s)
        s = jnp.where(s == 0.0, 0.0, s)
        bits = pltpu.bitcast(s, jnp.int32)
        key = bits ^ ((bits >> 31) & 0x7FFFFFFF)
        key_ref[j] = key
        hi_ref[j] = (key >> 16).astype(jnp.int16)
        lo_ref[j] = ((key & 0xFFFF) - 32768).astype(jnp.int16)

    def by_parity(x, fn):
        @pl.when(x % 2 == 0)
        def _():
            fn(0)

        @pl.when(x % 2 == 1)
        def _():
            fn(1)

    score_dots(0, 0)

    def score_pair(t, carry):
        score_dots(2 * t + 1, 1)
        score_finish(2 * t, 0, False)
        score_dots(2 * t + 2, 0)
        score_finish(2 * t + 1, 1, False)
        return carry

    lax.fori_loop(0, i // 2, score_pair, 0)

    @pl.when(i % 2 == 1)
    def _():
        score_dots(i, 1)
        score_finish(i - 1, 0, False)

    by_parity(i, lambda slot: score_finish(i, slot, True))

    one16 = jnp.ones((), jnp.int16)
    zero16 = jnp.zeros((), jnp.int16)

    def count_ge(src_ref, cand):
        cand_b = jnp.broadcast_to(cand.astype(jnp.int16), (blk, blk))

        def add_tile(j, cnts, on):
            ind = jnp.where(src_ref[j] >= cand_b, on, zero16)
            cnts = list(cnts)
            for g in range(blk // pack):
                cnts[g % len(cnts)] = cnts[g % len(cnts)] + ind[g * pack:(g + 1) * pack]
            return tuple(cnts)

        def body(t, cnts):
            return add_tile(2 * t + 1, add_tile(2 * t, cnts, one16), one16)

        zeros = tuple(jnp.zeros((pack, blk), jnp.int16) for _ in range(4))
        cnts = lax.fori_loop(0, (i + 1) // 2, body, zeros)
        cnts = add_tile(i, cnts, ((i + 1) % 2).astype(jnp.int16))
        cnt = (cnts[0] + cnts[1]) + (cnts[2] + cnts[3])
        return jnp.sum(cnt.astype(jnp.int32).astype(F32), axis=0, keepdims=True)

    def bisect16(src_ref, rank, cnt_all):
        def bit_body(b, st):
            prefix, c_acc, c_rej = st
            cand = prefix + lax.shift_left(jnp.int32(1), 15 - b)
            cnt = count_ge(src_ref, cand)
            ok = cnt >= rank
            return (jnp.where(ok, cand, prefix), jnp.where(ok, cnt, c_acc), jnp.where(ok, c_rej, cnt))
        init = (jnp.full((1, blk), -32768, jnp.int32), cnt_all, jnp.zeros((1, blk), F32))
        return lax.fori_loop(0, 16, bit_body, init)

    ncols = jnp.full((1, blk), ((i + 1) * blk).astype(F32), F32)
    p_hi, c_acc1, c_rej1 = bisect16(hi_ref, jnp.float32(topk), ncols)
    p_hi_b = jnp.broadcast_to(p_hi.astype(jnp.int16), (blk, blk))

    def group_body(j, carry):
        lo_ref[j] = jnp.where(hi_ref[j] == p_hi_b, lo_ref[j], jnp.int16(-32768))
        return carry

    lax.fori_loop(0, i + 1, group_body, 0)
    rank2 = jnp.float32(topk) - c_rej1
    p_lo, c_acc2, c_rej2 = bisect16(lo_ref, rank2, c_acc1 - c_rej1)
    thr = lax.shift_left(p_hi, 16) + (p_lo + 32768)
    need = rank2 - c_rej2
    has_ties = jnp.max((c_acc2 - c_rej2) - need) > 0.5

    m_ref[...] = jnp.full(m_ref.shape, NEG_BIG, F32)
    l_ref[...] = jnp.zeros(l_ref.shape, F32)
    acc_ref[...] = jnp.zeros(acc_ref.shape, F32)
    carry_ref[...] = jnp.zeros(carry_ref.shape, F32)

    def qk_dots(j, slot):
        start = pl.multiple_of(j * blk, blk)
        for p in range(ATT_HEADS // 2):
            big_ref[slot, :, 2 * p * blk:(2 * p + 2) * blk] = _dot_nt(
                kn_ref[pl.ds(start, blk), p * LANES:(p + 1) * LANES], qm_ref[2 * p * blk:(2 * p + 2) * blk])

    def tile_mask(j, kind, slot):
        kt = key_ref[j]

        @pl.when(jnp.logical_not(has_ties))
        def _():
            mb = jnp.where(kt >= thr, 0.0, NEG_BIG)
            mb_ref[slot] = jnp.where(future, NEG_BIG, mb) if kind == 2 else mb

        @pl.when(has_ties)
        def _():
            eq = jnp.where(kt == thr, 1.0, 0.0)
            lower = jnp.where(krow >= qcol, 1.0, 0.0).astype(BF16)
            seen = _dot(lower, eq.astype(BF16)) + carry_ref[0:1, :]
            keep = jnp.where(seen <= need, 0.0, NEG_BIG)
            mb = jnp.where(kt > thr, 0.0, jnp.where(kt == thr, keep, NEG_BIG))
            mb_ref[slot] = jnp.where(future, NEG_BIG, mb) if kind == 2 else mb
            carry_ref[0:1, :] += jnp.sum(eq, axis=0, keepdims=True)

    ones_rows = jnp.ones((pack, blk), BF16)

    def softmax_pv(j, slot, kind):
        vt = vt_ref[j]
        for h in range(ATT_HEADS):
            s = big_ref[slot, :, h * blk:(h + 1) * blk] + mb_ref[slot]
            if kind > 0:
                s = s + bias_ref[kind - 1, h]
            m_prev = m_ref[h]
            m_new = jnp.maximum(m_prev, jnp.max(s, axis=0, keepdims=True))
            alpha = jnp.exp2(m_prev - m_new)
            pexp = jnp.exp2(s - m_new[0:1, :]).astype(BF16)
            m_ref[h] = m_new
            ch = slice(h * ATT_HEAD_DIM, (h + 1) * ATT_HEAD_DIM)
            pv = _dot(jnp.concatenate([vt[ch, :], ones_rows], axis=0), pexp)
            l_ref[h] = alpha * l_ref[h] + pv[ATT_HEAD_DIM:ATT_HEAD_DIM + 1, :]
            acc_ref[ch, :] = alpha[0:1, :] * acc_ref[ch, :] + pv[:ATT_HEAD_DIM, :]

    n_far = i - 1
    qk_dots(0, 0)

    def attn_pair(t, carry):
        tile_mask(2 * t, 0, 0)
        tile_mask(2 * t + 1, 0, 1)
        qk_dots(2 * t + 1, 1)
        softmax_pv(2 * t, 0, 0)
        qk_dots(2 * t + 2, 0)
        softmax_pv(2 * t + 1, 1, 0)
        return carry

    lax.fori_loop(0, n_far // 2, attn_pair, 0)

    @pl.when(jnp.logical_and(n_far > 0, n_far % 2 == 1))
    def _():
        tile_mask(n_far - 1, 0, 0)
        qk_dots(n_far, 1)
        softmax_pv(n_far - 1, 0, 0)

    def near_tiles(slot):
        @pl.when(i > 0)
        def _():
            tile_mask(i - 1, 1, slot)
            qk_dots(i, 1 - slot)
            softmax_pv(i - 1, slot, 1)

    by_parity(i - 1, near_tiles)

    def diag_tile(slot):
        tile_mask(i, 2, slot)
        softmax_pv(i, slot, 2)

    by_parity(i, diag_tile)

    outs = []
    for h in range(ATT_HEADS):
        ch = slice(h * ATT_HEAD_DIM, (h + 1) * ATT_HEAD_DIM)
        outs.append(acc_ref[ch, :] / l_ref[h][0:1, :])
    o_ref[...] = jnp.concatenate(outs, axis=0).T.astype(BF16)


def _t5_bucket(dist):
    n = jnp.maximum(dist, 0)
    max_exact = REL_BUCKETS // 2
    large = max_exact + (jnp.log(jnp.maximum(n, max_exact).astype(F32) / max_exact)
                         / math.log(REL_MAX_DIST / max_exact)
                         * (REL_BUCKETS - max_exact)).astype(jnp.int32)
    large = jnp.minimum(large, REL_BUCKETS - 1)
    return jnp.where(n < max_exact, n, large)


def _bias_tiles(rel_bias, blk):
    width = 2 * blk
    f = ((rel_bias[_t5_bucket(jnp.arange(width, dtype=jnp.int32))] - rel_bias[REL_BUCKETS - 1]) * LOG2E).T
    h_sub = jnp.concatenate([f[:, blk:], f[:, :blk]], axis=1)
    h = jnp.stack([h_sub, f])
    skew = jnp.tile(h, (1, 1, blk))[:, :, :blk * (width - 1)].reshape(2, ATT_HEADS, blk, width - 1)
    return skew[:, :, :, :blk]


def _dsa(qn, qi, proj, kn, vt, ki, bias, *, bsz, seq, blk):
    n = bsz * seq
    nb = seq // blk
    topk = min(TOPK_MAX, seq // 4)
    once = pl.Buffered(1)
    return pl.pallas_call(
        functools.partial(_dsa_kernel, blk=blk, topk=topk),
        out_shape=jax.ShapeDtypeStruct((n, D_ATT), BF16),
        grid=(bsz, nb),
        in_specs=[
            pl.BlockSpec((blk, D_ATT), lambda b, i: (b * nb + i, 0)),
            pl.BlockSpec((blk, D_ATT), lambda b, i: (b * nb + i, 0)),
            pl.BlockSpec((blk, LANES), lambda b, i: (b * nb + i, COL_WIDX // LANES)),
            pl.BlockSpec((seq, D_ATT), lambda b, i: (b, 0), pipeline_mode=once),
            pl.BlockSpec((nb, D_ATT, blk), lambda b, i: (b, 0, 0), pipeline_mode=once),
            pl.BlockSpec((seq, LANES), lambda b, i: (b, 0), pipeline_mode=once),
            pl.BlockSpec((2, ATT_HEADS, blk, blk), lambda b, i: (0, 0, 0, 0), pipeline_mode=once),
        ],
        out_specs=pl.BlockSpec((blk, D_ATT), lambda b, i: (b * nb + i, 0)),
        scratch_shapes=[
            pltpu.VMEM((nb, blk, blk), jnp.int32),
            pltpu.VMEM((nb, blk, blk), jnp.int16),
            pltpu.VMEM((nb, blk, blk), jnp.int16),
            pltpu.VMEM((2, blk, ATT_HEADS * blk), F32),
            pltpu.VMEM((ATT_HEADS * blk, LANES), BF16),
            pltpu.VMEM((IDX_HEADS * blk, LANES), BF16),
            pltpu.VMEM((LANES, blk), F32),
            pltpu.VMEM((2, blk, blk), F32),
            pltpu.VMEM((SUBLANES, blk), F32),
            pltpu.VMEM((ATT_HEADS, SUBLANES, blk), F32),
            pltpu.VMEM((ATT_HEADS, SUBLANES, blk), F32),
            pltpu.VMEM((D_ATT, blk), F32),
        ],
        compiler_params=_cparams("arbitrary", "arbitrary"),
        name="dsa_attention",
    )(qn, qi, proj, kn, vt, ki, bias)


def _outproj_kernel(x_ref, yssd_ref, yatt_ref, scb_ref, scc_ref, sch_ref, hc_ref, hh_ref,
                    cw_ref, w_ref, o_ref, cat_ref, *, blocks_per_seq):
    i = pl.program_id(0)

    @pl.when(pl.program_id(1) == 0)
    def _():
        prev = jnp.where(i % blocks_per_seq == 0, 0.0, hc_ref[...] * hh_ref[...])
        ysc = scb_ref[...] * _causal_conv(scc_ref[...] * sch_ref[...], prev, cw_ref[...])
        cat_ref[:, :D_SSD] = yssd_ref[...]
        cat_ref[:, D_SSD:D_SSD + D_ATT] = yatt_ref[...]
        cat_ref[:, D_SSD + D_ATT:] = ysc.astype(BF16)

    o_ref[...] = x_ref[...] + _dot(cat_ref[...], w_ref[...])


def _outproj(x, y_ssd, y_att, proj, sc_w, w_out, *, seq, tm, tn):
    n, d = x.shape
    halo = tm // SUBLANES
    blk = lambda col: pl.BlockSpec((tm, D_SC), lambda i, j: (i, col // D_SC))
    hblk = lambda col: pl.BlockSpec(
        (SUBLANES, D_SC), lambda i, j: (jnp.maximum(i * halo - 1, 0), col // D_SC))
    return pl.pallas_call(
        functools.partial(_outproj_kernel, blocks_per_seq=seq // tm),
        out_shape=jax.ShapeDtypeStruct((n, d), F32),
        grid=(n // tm, d // tn),
        in_specs=[
            pl.BlockSpec((tm, tn), lambda i, j: (i, j)),
            pl.BlockSpec((tm, D_SSD), lambda i, j: (i, 0)),
            pl.BlockSpec((tm, D_ATT), lambda i, j: (i, 0)),
            blk(COL_SCB), blk(COL_SCC), blk(COL_SCH), hblk(COL_SCC), hblk(COL_SCH),
            pl.BlockSpec((SC_CONV, D_SC), lambda i, j: (0, 0)),
            pl.BlockSpec((d, tn), lambda i, j: (0, j)),
        ],
        out_specs=pl.BlockSpec((tm, tn), lambda i, j: (i, j)),
        scratch_shapes=[pltpu.VMEM((tm, d), BF16)],
        compiler_params=_cparams("arbitrary", "arbitrary"),
        name="out_proj",
    )(x, y_ssd, y_att, proj, proj, proj, proj, proj, sc_w, w_out)


def _ffn_kernel(x_ref, g_ref, wg_ref, wu_ref, cw_ref, wd_ref, o_ref, h_ref, acc_ref, halo_ref,
                *, blocks_per_seq):
    i = pl.program_id(0)
    j = pl.program_id(1)
    tm = x_ref.shape[0]

    @pl.when(j == 0)
    def _():
        x = x_ref[...]
        ms = jnp.mean(x * x, axis=-1, keepdims=True)
        h_ref[...] = (x * lax.rsqrt(ms + NORM_EPS) * g_ref[...]).astype(BF16)
        acc_ref[...] = jnp.zeros(acc_ref.shape, F32)

    @pl.when(i % blocks_per_seq == 0)
    def _():
        halo_ref[j] = jnp.zeros(halo_ref.shape[1:], F32)

    h = h_ref[...]
    gate = _dot(h, wg_ref[...])
    up = _dot(h, wu_ref[...])
    prev = halo_ref[j]
    halo_ref[j] = gate[tm - SUBLANES:, :]
    gc = _causal_conv(gate, prev, cw_ref[...])
    act = (gc * _sigmoid(gc) * up).astype(BF16)
    acc_ref[...] += _dot(act, wd_ref[...])

    @pl.when(j == pl.num_programs(1) - 1)
    def _():
        o_ref[...] = x_ref[...] + acc_ref[...]


def _ffn(x, g, wg, wu, cw, wd, *, seq, tm, tf):
    n, d = x.shape
    dff = wg.shape[1]
    return pl.pallas_call(
        functools.partial(_ffn_kernel, blocks_per_seq=seq // tm),
        out_shape=jax.ShapeDtypeStruct((n, d), F32),
        grid=(n // tm, dff // tf),
        in_specs=[
            pl.BlockSpec((tm, d), lambda i, j: (i, 0)),
            pl.BlockSpec((1, d), lambda i, j: (0, 0)),
            pl.BlockSpec((d, tf), lambda i, j: (0, j)),
            pl.BlockSpec((d, tf), lambda i, j: (0, j)),
            pl.BlockSpec((FFN_CONV, tf), lambda i, j: (0, j)),
            pl.BlockSpec((tf, d), lambda i, j: (j, 0)),
        ],
        out_specs=pl.BlockSpec((tm, d), lambda i, j: (i, 0)),
        scratch_shapes=[
            pltpu.VMEM((tm, d), BF16),
            pltpu.VMEM((tm, d), F32),
            pltpu.VMEM((dff // tf, SUBLANES, tf), F32),
        ],
        compiler_params=_cparams("arbitrary", "arbitrary"),
        name="conv_gated_mlp",
    )(x, g, wg, wu, cw, wd)


def _pack_w_in(w_in):
    sizes = (D_SSD, SSD_CONV_DIM, SSD_HEADS, D_ATT, D_ATT, D_ATT,
             IDX_HEADS * IDX_DIM, IDX_DIM, IDX_HEADS, D_SC, D_SC, D_SC)
    offs = [0]
    for s in sizes:
        offs.append(offs[-1] + s)
    w16 = w_in.astype(BF16)
    z, xbc, dt, q, k, v, qi, ki, wi, scb, scc, sch = (
        w16[..., offs[t]:offs[t + 1]] for t in range(len(sizes)))
    padto = lambda a, w: jnp.pad(a, ((0, 0), (0, 0), (0, w - a.shape[-1])))
    parts = [xbc, q, z, k, v, qi, scb, scc, sch, padto(dt, LANES),
             jnp.concatenate([ki, ki], axis=-1), padto(wi, LANES)]
    packed = jnp.concatenate(parts, axis=-1)
    return padto(packed, D_PACKED)


def _forward(x, norm_mix, w_in, ssd_conv_w, ssd_conv_b, ssd_dt_bias, ssd_a_log, ssd_d, ssd_norm,
             att_q_norm, att_k_norm, rel_bias, sc_conv_w, w_out, norm_ffn,
             ffn_w_gate, ffn_w_up, ffn_conv_w, ffn_w_down, *, tm_proj, tn_proj, tm_out, tn_out,
             tm_ffn, tf_ffn, tm_prep, att_blk):
    bsz, seq, d = x.shape
    depth = w_in.shape[0]
    n = bsz * seq
    xf = x.reshape(n, d)
    w_in_p = _pack_w_in(w_in)
    w_out_b = w_out.astype(BF16)
    wg_b = ffn_w_gate.astype(BF16)
    wu_b = ffn_w_up.astype(BF16)
    wd_b = ffn_w_down.astype(BF16)
    bias = _bias_tiles(rel_bias, att_blk)
    for l in range(depth):
        proj = _rms_matmul(xf, norm_mix[l].reshape(1, d), w_in_p[l], tm=tm_proj, tn=tn_proj)
        y_ssd = _ssd(proj, ssd_conv_w[l], ssd_conv_b[l], ssd_dt_bias[l], ssd_a_log[l], ssd_d[l],
                     ssd_norm[l], bsz=bsz, seq=seq)
        qn, kn, vt, qi, ki = _dsa_prep(proj, att_q_norm[l], att_k_norm[l], tm=tm_prep, blk=att_blk)
        y_att = _dsa(qn, qi, proj, kn, vt, ki, bias, bsz=bsz, seq=seq, blk=att_blk)
        xf = _outproj(xf, y_ssd, y_att, proj, sc_conv_w[l], w_out_b[l], seq=seq, tm=tm_out, tn=tn_out)
        xf = _ffn(xf, norm_ffn[l].reshape(1, d), wg_b[l], wu_b[l], ffn_conv_w[l], wd_b[l],
                  seq=seq, tm=tm_ffn, tf=tf_ffn)
    return xf.reshape(bsz, seq, d)


def kernel(x, norm_mix, w_in, ssd_conv_w, ssd_conv_b, ssd_dt_bias, ssd_a_log, ssd_d, ssd_norm, att_q_norm, att_k_norm, rel_bias, sc_conv_w, w_out, norm_ffn, ffn_w_gate, ffn_w_up, ffn_conv_w, ffn_w_down):
    return _forward(x, norm_mix, w_in, ssd_conv_w, ssd_conv_b, ssd_dt_bias, ssd_a_log, ssd_d, ssd_norm,
                    att_q_norm, att_k_norm, rel_bias, sc_conv_w, w_out, norm_ffn,
                    ffn_w_gate, ffn_w_up, ffn_conv_w, ffn_w_down,
                    tm_proj=1024, tn_proj=512, tm_out=512, tn_out=2048,
                    tm_ffn=512, tf_ffn=512, tm_prep=512, att_blk=256)
```

```python
import functools
import math

import jax
import jax.numpy as jnp
from jax import lax
from jax.experimental import pallas as pl
from jax.experimental.pallas import tpu as pltpu

F32 = jnp.float32
BF16 = jnp.bfloat16
HIGHEST = lax.Precision.HIGHEST

LANES = 128
SUBLANES = 8
VMEM_LIMIT_BYTES = 56 * 1024 * 1024

D_MODEL = 2048
D_SSD = 1024
D_ATT = 512
D_SC = 512
SSD_HEAD_DIM = 64
SSD_HEADS = 16
SSD_GROUPS = 2
SSD_STATE = 128
SSD_CONV = 4
SSD_CHUNK = 128
SSD_CONV_DIM = D_SSD + 2 * SSD_GROUPS * SSD_STATE
ATT_HEAD_DIM = 64
ATT_HEADS = 8
IDX_HEADS = 8
IDX_DIM = 64
TOPK_MAX = 256
REL_BUCKETS = 32
REL_MAX_DIST = 128
SC_CONV = 3
D_FF = 5632
FFN_CONV = 3
NORM_EPS = 1e-6

COL_XBC = 0
COL_Q = 1536
COL_Z = 2048
COL_K = 3072
COL_V = 3584
COL_QIDX = 4096
COL_SCB = 4608
COL_SCC = 5120
COL_SCH = 5632
COL_DT = 6144
COL_KIDX = 6272
COL_WIDX = 6400
D_PACKED = 6656

NEG_BIG = -1e30
LOG2E = math.log2(math.e)


def _cparams(*sem):
    return pltpu.CompilerParams(dimension_semantics=sem, vmem_limit_bytes=VMEM_LIMIT_BYTES)


def _dot(a, b):
    return jnp.dot(a, b, preferred_element_type=F32)


def _dot_nt(a, b):
    return lax.dot_general(a, b, (((1,), (1,)), ((), ())), preferred_element_type=F32)


def _dot_exact(a, b):
    return jnp.dot(a, b, preferred_element_type=F32, precision=HIGHEST)


def _expand_exact(x, onehot3):
    x1 = x.astype(BF16)
    r1 = x - x1.astype(F32)
    x2 = r1.astype(BF16)
    x3 = (r1 - x2.astype(F32)).astype(BF16)
    return _dot(jnp.concatenate([x1, x2, x3], axis=1), onehot3)


def _sigmoid(x):
    return 1.0 / (1.0 + jnp.exp(-x))


def _shift_rows(x, prev8, s):
    xr = pltpu.roll(x, s, 0)
    pr = pltpu.roll(prev8, s, 0)
    rows = lax.broadcasted_iota(jnp.int32, (SUBLANES, x.shape[1]), 0)
    top = jnp.where(rows < s, pr, xr[:SUBLANES])
    return jnp.concatenate([top, xr[SUBLANES:]], axis=0)


def _causal_conv(x, prev8, w):
    k = w.shape[0]
    y = w[k - 1:k] * x
    for s in range(1, k):
        y = y + w[k - 1 - s:k - s] * _shift_rows(x, prev8, s)
    return y


def _rms_matmul_kernel(x_ref, g_ref, w_ref, o_ref, h_ref):
    @pl.when(pl.program_id(1) == 0)
    def _():
        x = x_ref[...]
        ms = jnp.mean(x * x, axis=-1, keepdims=True)
        h_ref[...] = (x * lax.rsqrt(ms + NORM_EPS) * g_ref[...]).astype(BF16)

    o_ref[...] = _dot(h_ref[...], w_ref[...])


def _rms_matmul(x, g, w, *, tm, tn):
    n, d = x.shape
    dout = w.shape[1]
    return pl.pallas_call(
        _rms_matmul_kernel,
        out_shape=jax.ShapeDtypeStruct((n, dout), F32),
        grid=(n // tm, dout // tn),
        in_specs=[
            pl.BlockSpec((tm, d), lambda i, j: (i, 0)),
            pl.BlockSpec((1, d), lambda i, j: (0, 0)),
            pl.BlockSpec((d, tn), lambda i, j: (0, j)),
        ],
        out_specs=pl.BlockSpec((tm, tn), lambda i, j: (i, j)),
        scratch_shapes=[pltpu.VMEM((tm, d), BF16)],
        compiler_params=_cparams("arbitrary", "arbitrary"),
        name="rms_in_proj",
    )(x, g, w)


def _ssd_kernel(xbc_ref, halo_ref, z_ref, dt_ref, cw_ref, cb_ref, dtb_ref, alog_ref,
                dx_ref, ng_ref, ex_ref, o_ref, h_ref, y_ref):
    c = pl.program_id(1)
    L = SSD_CHUNK

    @pl.when(c == 0)
    def _():
        h_ref[...] = jnp.zeros(h_ref.shape, F32)

    prev = jnp.where(c == 0, 0.0, halo_ref[...])
    xbc = _causal_conv(xbc_ref[...], prev, cw_ref[...]) + cb_ref[...]
    xbc = xbc * _sigmoid(xbc)
    xs = xbc[:, :D_SSD]

    row = lax.broadcasted_iota(jnp.int32, (L, L), 0)
    col = lax.broadcasted_iota(jnp.int32, (L, L), 1)
    tril = (row >= col).astype(F32)

    def softplus(v):
        return jnp.maximum(v, 0.0) + jnp.log1p(jnp.exp(-jnp.abs(v)))

    dt_s = softplus(dt_ref[...] + dtb_ref[...])
    a_s = dt_s * (-jnp.exp(alog_ref[...]))
    acs_s = _dot_exact(tril, a_s)
    acs_t = acs_s.T
    ex = ex_ref[...]
    dt_x = _expand_exact(dt_s, ex)
    e_x = _expand_exact(jnp.exp(acs_s), ex)
    de_x = _expand_exact(jnp.exp(acs_s[L - 1:L, :] - acs_s), ex)
    decay = e_x[L - 1:L, :]
    xdt = xs * dt_x
    xdt_end = (xdt * de_x).astype(BF16)
    xdt = xdt.astype(BF16)

    lane = lax.broadcasted_iota(jnp.int32, (L, LANES), 1)
    lo = lane < SSD_HEAD_DIM
    lower = row >= col

    for g in range(SSD_GROUPS):
        bm = xbc[:, D_SSD + g * SSD_STATE:D_SSD + (g + 1) * SSD_STATE].astype(BF16)
        cm = xbc[:, D_SSD + (SSD_GROUPS + g) * SSD_STATE:
                 D_SSD + (SSD_GROUPS + g + 1) * SSD_STATE].astype(BF16)
        cb = _dot_nt(cm, bm)
        bm_t = bm.T
        for q in range(4):
            pair = g * 4 + q
            sl = slice(pair * LANES, (pair + 1) * LANES)
            gs = []
            for hh in range(2):
                h = 2 * pair + hh
                seg = (jnp.broadcast_to(acs_s[:, h:h + 1], (L, L))
                       - jnp.broadcast_to(acs_t[h:h + 1, :], (L, L)))
                lm = jnp.where(lower, jnp.exp(seg), 0.0)
                gs.append((cb * lm).astype(BF16))
            gpair = jnp.concatenate(gs, axis=1)
            xp = xdt[:, sl]
            zero = jnp.zeros_like(xp)
            x2 = jnp.concatenate([jnp.where(lo, xp, zero), jnp.where(lo, zero, xp)], axis=0)
            y = _dot(gpair, x2)
            hprev = h_ref[pair]
            y = y + e_x[:, sl] * _dot(cm, hprev.astype(BF16))
            y = y + xs[:, sl] * dx_ref[:, sl]
            h_ref[pair] = hprev * decay[:, sl] + _dot(bm_t, xdt_end[:, sl])
            y_ref[:, sl] = y

    z = z_ref[...]
    y = y_ref[...] * (z * _sigmoid(z))
    half = D_SSD // SSD_GROUPS
    outs = []
    for g in range(SSD_GROUPS):
        yg = y[:, g * half:(g + 1) * half]
        ms = jnp.mean(yg * yg, axis=-1, keepdims=True)
        outs.append(yg * lax.rsqrt(ms + NORM_EPS))
    o_ref[...] = (jnp.concatenate(outs, axis=1) * ng_ref[...]).astype(BF16)


def _ssd(proj, conv_w, conv_b, dt_bias, a_log, d_skip, norm_g, *, bsz, seq):
    nc = seq // SSD_CHUNK
    n = bsz * seq
    pad = LANES - SSD_HEADS
    dtb = jnp.pad(dt_bias, (0, pad)).reshape(1, LANES)
    alog = jnp.pad(a_log, (0, pad), constant_values=NEG_BIG).reshape(1, LANES)
    rep = lambda v: jnp.repeat(v, SSD_HEAD_DIM).reshape(1, D_SSD)
    expand = (jnp.arange(LANES)[:, None] == (jnp.arange(D_SSD)[None, :] // SSD_HEAD_DIM)).astype(BF16)
    expand = jnp.concatenate([expand] * 3, axis=0)
    rows_per_halo = SSD_CHUNK // SUBLANES
    const = lambda shape: pl.BlockSpec(shape, lambda b, c: (0,) * len(shape))
    return pl.pallas_call(
        _ssd_kernel,
        out_shape=jax.ShapeDtypeStruct((n, D_SSD), BF16),
        grid=(bsz, nc),
        in_specs=[
            pl.BlockSpec((SSD_CHUNK, SSD_CONV_DIM), lambda b, c: (b * nc + c, COL_XBC // SSD_CONV_DIM)),
            pl.BlockSpec((SUBLANES, SSD_CONV_DIM),
                         lambda b, c: (jnp.maximum((b * nc + c) * rows_per_halo - 1, 0), 0)),
            pl.BlockSpec((SSD_CHUNK, D_SSD), lambda b, c: (b * nc + c, COL_Z // D_SSD)),
            pl.BlockSpec((SSD_CHUNK, LANES), lambda b, c: (b * nc + c, COL_DT // LANES)),
            const((SSD_CONV, SSD_CONV_DIM)),
            const((1, SSD_CONV_DIM)),
            const((1, LANES)),
            const((1, LANES)),
            const((1, D_SSD)),
            const((1, D_SSD)),
            const((3 * LANES, D_SSD)),
        ],
        out_specs=pl.BlockSpec((SSD_CHUNK, D_SSD), lambda b, c: (b * nc + c, 0)),
        scratch_shapes=[pltpu.VMEM((SSD_HEADS // 2, SSD_STATE, LANES), F32),
                        pltpu.VMEM((SSD_CHUNK, D_SSD), F32)],
        compiler_params=_cparams("arbitrary", "arbitrary"),
        name="ssd_scan",
    )(proj, proj, proj, proj, conv_w, conv_b.reshape(1, -1), dtb, alog,
      rep(d_skip), norm_g.reshape(1, -1), expand)


def _dsa_prep_kernel(q_ref, k_ref, v_ref, qi_ref, ki_ref, qg_ref, kg_ref, seg_ref,
                     qn_ref, kn_ref, vt_ref, qib_ref, kib_ref, *, blk):
    def head_norm(x, g):
        ms = _expand_exact(x * x, seg_ref[...]) * (1.0 / ATT_HEAD_DIM)
        return x * lax.rsqrt(ms + NORM_EPS) * g

    qn_ref[...] = (head_norm(q_ref[...], qg_ref[...]) * (LOG2E * ATT_HEAD_DIM ** -0.5)).astype(BF16)
    kn_ref[...] = head_norm(k_ref[...], kg_ref[...]).astype(BF16)
    for c in range(vt_ref.shape[0]):
        vt_ref[c] = v_ref[c * blk:(c + 1) * blk, :].T.astype(BF16)
    qib_ref[...] = qi_ref[...].astype(BF16)
    kib_ref[...] = ki_ref[...].astype(BF16)


def _dsa_prep(proj, q_norm, k_norm, *, tm, blk):
    n = proj.shape[0]
    seg = (jnp.arange(D_ATT)[:, None] // ATT_HEAD_DIM
           == jnp.arange(D_ATT)[None, :] // ATT_HEAD_DIM).astype(BF16)
    seg = jnp.concatenate([seg] * 3, axis=0)
    tile = lambda v: jnp.tile(v, ATT_HEADS).reshape(1, D_ATT)
    col_blk = lambda col, w: pl.BlockSpec((tm, w), lambda i: (i, col // w))
    const = lambda shape: pl.BlockSpec(shape, lambda i: (0,) * len(shape))
    return pl.pallas_call(
        functools.partial(_dsa_prep_kernel, blk=blk),
        out_shape=(
            jax.ShapeDtypeStruct((n, D_ATT), BF16),
            jax.ShapeDtypeStruct((n, D_ATT), BF16),
            jax.ShapeDtypeStruct((n // blk, D_ATT, blk), BF16),
            jax.ShapeDtypeStruct((n, D_ATT), BF16),
            jax.ShapeDtypeStruct((n, LANES), BF16),
        ),
        grid=(n // tm,),
        in_specs=[col_blk(COL_Q, D_ATT), col_blk(COL_K, D_ATT), col_blk(COL_V, D_ATT),
                  col_blk(COL_QIDX, D_ATT), col_blk(COL_KIDX, LANES),
                  const((1, D_ATT)), const((1, D_ATT)), const((3 * D_ATT, D_ATT))],
        out_specs=(
            pl.BlockSpec((tm, D_ATT), lambda i: (i, 0)),
            pl.BlockSpec((tm, D_ATT), lambda i: (i, 0)),
            pl.BlockSpec((tm // blk, D_ATT, blk), lambda i: (i, 0, 0)),
            pl.BlockSpec((tm, D_ATT), lambda i: (i, 0)),
            pl.BlockSpec((tm, LANES), lambda i: (i, 0)),
        ),
        compiler_params=_cparams("arbitrary"),
        name="dsa_prep",
    )(proj, proj, proj, proj, proj, tile(q_norm), tile(k_norm), seg)


def _dsa_kernel(qn_ref, qi_ref, w_ref, kn_ref, vt_ref, ki_ref, bias_ref, o_ref,
                key_ref, hi_ref, lo_ref, big_ref, qm_ref, qim_ref, wt_ref, mb_ref, carry_ref,
                m_ref, l_ref, acc_ref, *, blk, topk):
    i = pl.program_id(1)
    pack = 16
    lane = lax.broadcasted_iota(jnp.int32, (blk, LANES), 1)
    lo_lanes = lane < ATT_HEAD_DIM
    krow = lax.broadcasted_iota(jnp.int32, (blk, blk), 0)
    qcol = lax.broadcasted_iota(jnp.int32, (blk, blk), 1)
    future = krow > qcol

    for p in range(ATT_HEADS // 2):
        sl = slice(p * LANES, (p + 1) * LANES)
        qp = qn_ref[:, sl]
        qip = qi_ref[:, sl]
        zero = jnp.zeros_like(qp)
        qm_ref[2 * p * blk:(2 * p + 1) * blk] = jnp.where(lo_lanes, qp, zero)
        qm_ref[(2 * p + 1) * blk:(2 * p + 2) * blk] = jnp.where(lo_lanes, zero, qp)
        qim_ref[2 * p * blk:(2 * p + 1) * blk] = jnp.where(lo_lanes, qip, zero)
        qim_ref[(2 * p + 1) * blk:(2 * p + 2) * blk] = jnp.where(lo_lanes, zero, qip)
    wt_ref[...] = w_ref[...].T

    def score_dots(j, slot):
        kk = ki_ref[pl.ds(pl.multiple_of(j * blk, blk), blk), :]
        big_ref[slot] = _dot_nt(kk, qim_ref[...])

    def score_finish(j, slot, diag):
        s = jnp.zeros((blk, blk), F32)
        for h in range(IDX_HEADS):
            s = s + jnp.maximum(big_ref[slot, :, h * blk:(h + 1) * blk], 0.0) * wt_ref[h:h + 1, :]
        if diag:
            s = jnp.where(future, -jnp.inf, s)
        s = jnp.where(s == 0.0, 0.0, s)
        bits = pltpu.bitcast(s, jnp.int32)
        key = bits ^ ((bits >> 31) & 0x7FFFFFFF)
        key_ref[j] = key
        hi_ref[j] = (key >> 16).astype(jnp.int16)
        lo_ref[j] = ((key & 0xFFFF) - 32768).astype(jnp.int16)

    def by_parity(x, fn):
        @pl.when(x % 2 == 0)
        def _():
            fn(0)

        @pl.when(x % 2 == 1)
        def _():
            fn(1)

    score_dots(0, 0)

    def score_pair(t, carry):
        score_dots(2 * t + 1, 1)
        score_finish(2 * t, 0, False)
        score_dots(2 * t + 2, 0)
        score_finish(2 * t + 1, 1, False)
        return carry

    lax.fori_loop(0, i // 2, score_pair, 0)

    @pl.when(i % 2 == 1)
    def _():
        score_dots(i, 1)
        score_finish(i - 1, 0, False)

    by_parity(i, lambda slot: score_finish(i, slot, True))

    one16 = jnp.ones((), jnp.int16)
    zero16 = jnp.zeros((), jnp.int16)

    def count_ge(src_ref, cand):
        cand_b = jnp.broadcast_to(cand.astype(jnp.int16), (blk, blk))

        def add_tile(j, cnts, on):
            ind = jnp.where(src_ref[j] >= cand_b, on, zero16)
            cnts = list(cnts)
            for g in range(blk // pack):
                cnts[g % len(cnts)] = cnts[g % len(cnts)] + ind[g * pack:(g + 1) * pack]
            return tuple(cnts)

        def body(t, cnts):
            return add_tile(2 * t + 1, add_tile(2 * t, cnts, one16), one16)

        zeros = tuple(jnp.zeros((pack, blk), jnp.int16) for _ in range(4))
        cnts = lax.fori_loop(0, (i + 1) // 2, body, zeros)
        cnts = add_tile(i, cnts, ((i + 1) % 2).astype(jnp.int16))
        cnt = (cnts[0] + cnts[1]) + (cnts[2] + cnts[3])
        return jnp.sum(cnt.astype(jnp.int32).astype(F32), axis=0, keepdims=True)

    def bisect16(src_ref, rank, cnt_all):
        def bit_body(b, st):
            prefix, c_acc, c_rej = st
            cand = prefix + lax.shift_left(jnp.int32(1), 15 - b)
            cnt = count_ge(src_ref, cand)
            ok = cnt >= rank
            return (jnp.where(ok, cand, prefix), jnp.where(ok, cnt, c_acc), jnp.where(ok, c_rej, cnt))
        init = (jnp.full((1, blk), -32768, jnp.int32), cnt_all, jnp.zeros((1, blk), F32))
        return lax.fori_loop(0, 16, bit_body, init)

    ncols = jnp.full((1, blk), ((i + 1) * blk).astype(F32), F32)
    p_hi, c_acc1, c_rej1 = bisect16(hi_ref, jnp.float32(topk), ncols)
    p_hi_b = jnp.broadcast_to(p_hi.astype(jnp.int16), (blk, blk))

    def group_body(j, carry):
        lo_ref[j] = jnp.where(hi_ref[j] == p_hi_b, lo_ref[j], jnp.int16(-32768))
        return carry

    lax.fori_loop(0, i + 1, group_body, 0)
    rank2 = jnp.float32(topk) - c_rej1
    p_lo, c_acc2, c_rej2 = bisect16(lo_ref, rank2, c_acc1 - c_rej1)
    thr = lax.shift_left(p_hi, 16) + (p_lo + 32768)
    need = rank2 - c_rej2
    has_ties = jnp.max((c_acc2 - c_rej2) - need) > 0.5

    m_ref[...] = jnp.full(m_ref.shape, NEG_BIG, F32)
    l_ref[...] = jnp.zeros(l_ref.shape, F32)
    acc_ref[...] = jnp.zeros(acc_ref.shape, F32)
    carry_ref[...] = jnp.zeros(carry_ref.shape, F32)

    def qk_dots(j, slot):
        start = pl.multiple_of(j * blk, blk)
        for p in range(ATT_HEADS // 2):
            mb2 = jnp.concatenate([mb_ref[slot], mb_ref[slot]], axis=1)
            big_ref[slot, :, 2 * p * blk:(2 * p + 2) * blk] = _dot_nt(
                kn_ref[pl.ds(start, blk), p * LANES:(p + 1) * LANES], qm_ref[2 * p * blk:(2 * p + 2) * blk]) + mb2

    def tile_mask(j, kind, slot):
        kt = key_ref[j]

        @pl.when(jnp.logical_not(has_ties))
        def _():
            mb = jnp.where(kt >= thr, 0.0, NEG_BIG)
            mb_ref[slot] = jnp.where(future, NEG_BIG, mb) if kind == 2 else mb

        @pl.when(has_ties)
        def _():
            eq = jnp.where(kt == thr, 1.0, 0.0)
            lower = jnp.where(krow >= qcol, 1.0, 0.0).astype(BF16)
            seen = _dot(lower, eq.astype(BF16)) + carry_ref[0:1, :]
            keep = jnp.where(seen <= need, 0.0, NEG_BIG)
            mb = jnp.where(kt > thr, 0.0, jnp.where(kt == thr, keep, NEG_BIG))
            mb_ref[slot] = jnp.where(future, NEG_BIG, mb) if kind == 2 else mb
            carry_ref[0:1, :] += jnp.sum(eq, axis=0, keepdims=True)

    ones_rows = jnp.ones((pack, blk), BF16)

    def softmax_pv(j, slot, kind):
        vt = vt_ref[j]
        for h in range(ATT_HEADS):
            half = blk // 2
            s = big_ref[slot, :, h * blk:(h + 1) * blk]
            if kind > 0:
                s = s + bias_ref[kind - 1, h]
            m_prev = m_ref[h]
            m_new = jnp.maximum(m_prev, jnp.max(s, axis=0, keepdims=True))
            alpha = jnp.exp2(m_prev - m_new)
            if kind > 0:
                pexp = jnp.exp2(s - m_new[0:1, :]).astype(BF16)
            else:
                pexp = jnp.concatenate(
                    [jnp.exp2(big_ref[slot, r * half:(r + 1) * half, h * blk:(h + 1) * blk]
                              - m_new[0:1, :]).astype(BF16) for r in range(2)], axis=0)
            m_ref[h] = m_new
            ch = slice(h * ATT_HEAD_DIM, (h + 1) * ATT_HEAD_DIM)
            pv = _dot(jnp.concatenate([vt[ch, :], ones_rows], axis=0), pexp)
            l_ref[h] = alpha * l_ref[h] + pv[ATT_HEAD_DIM:ATT_HEAD_DIM + 1, :]
            acc_ref[ch, :] = alpha[0:1, :] * acc_ref[ch, :] + pv[:ATT_HEAD_DIM, :]

    n_far = i - 1

    @pl.when(i == 0)
    def _():
        tile_mask(0, 2, 0)

    @pl.when(i > 0)
    def _():
        tile_mask(0, 0, 0)

    qk_dots(0, 0)

    def attn_pair(t, carry):
        tile_mask(2 * t + 1, 0, 1)
        tile_mask(2 * t + 2, 0, 0)
        qk_dots(2 * t + 1, 1)
        softmax_pv(2 * t, 0, 0)
        qk_dots(2 * t + 2, 0)
        softmax_pv(2 * t + 1, 1, 0)
        return carry

    lax.fori_loop(0, n_far // 2, attn_pair, 0)

    @pl.when(jnp.logical_and(n_far > 0, n_far % 2 == 1))
    def _():
        tile_mask(n_far, 0, 1)
        qk_dots(n_far, 1)
        softmax_pv(n_far - 1, 0, 0)

    def near_tiles(slot):
        @pl.when(i > 0)
        def _():
            tile_mask(i, 2, 1 - slot)
            qk_dots(i, 1 - slot)
            softmax_pv(i - 1, slot, 1)

    by_parity(i - 1, near_tiles)
    by_parity(i, lambda slot: softmax_pv(i, slot, 2))

    outs = []
    for h in range(ATT_HEADS):
        ch = slice(h * ATT_HEAD_DIM, (h + 1) * ATT_HEAD_DIM)
        outs.append(acc_ref[ch, :] / l_ref[h][0:1, :])
    o_ref[...] = jnp.concatenate(outs, axis=0).T.astype(BF16)


def _t5_bucket(dist):
    n = jnp.maximum(dist, 0)
    max_exact = REL_BUCKETS // 2
    large = max_exact + (jnp.log(jnp.maximum(n, max_exact).astype(F32) / max_exact)
                         / math.log(REL_MAX_DIST / max_exact)
                         * (REL_BUCKETS - max_exact)).astype(jnp.int32)
    large = jnp.minimum(large, REL_BUCKETS - 1)
    return jnp.where(n < max_exact, n, large)


def _bias_tiles(rel_bias, blk):
    width = 2 * blk
    f = ((rel_bias[_t5_bucket(jnp.arange(width, dtype=jnp.int32))] - rel_bias[REL_BUCKETS - 1]) * LOG2E).T
    h_sub = jnp.concatenate([f[:, blk:], f[:, :blk]], axis=1)
    h = jnp.stack([h_sub, f])
    skew = jnp.tile(h, (1, 1, blk))[:, :, :blk * (width - 1)].reshape(2, ATT_HEADS, blk, width - 1)
    return skew[:, :, :, :blk]


def _dsa(qn, qi, proj, kn, vt, ki, bias, *, bsz, seq, blk):
    n = bsz * seq
    nb = seq // blk
    topk = min(TOPK_MAX, seq // 4)
    once = pl.Buffered(1)
    return pl.pallas_call(
        functools.partial(_dsa_kernel, blk=blk, topk=topk),
        out_shape=jax.ShapeDtypeStruct((n, D_ATT), BF16),
        grid=(bsz, nb),
        in_specs=[
            pl.BlockSpec((blk, D_ATT), lambda b, i: (b * nb + i, 0)),
            pl.BlockSpec((blk, D_ATT), lambda b, i: (b * nb + i, 0)),
            pl.BlockSpec((blk, LANES), lambda b, i: (b * nb + i, COL_WIDX // LANES)),
            pl.BlockSpec((seq, D_ATT), lambda b, i: (b, 0), pipeline_mode=once),
            pl.BlockSpec((nb, D_ATT, blk), lambda b, i: (b, 0, 0), pipeline_mode=once),
            pl.BlockSpec((seq, LANES), lambda b, i: (b, 0), pipeline_mode=once),
            pl.BlockSpec((2, ATT_HEADS, blk, blk), lambda b, i: (0, 0, 0, 0), pipeline_mode=once),
        ],
        out_specs=pl.BlockSpec((blk, D_ATT), lambda b, i: (b * nb + i, 0)),
        scratch_shapes=[
            pltpu.VMEM((nb, blk, blk), jnp.int32),
            pltpu.VMEM((nb, blk, blk), jnp.int16),
            pltpu.VMEM((nb, blk, blk), jnp.int16),
            pltpu.VMEM((2, blk, ATT_HEADS * blk), F32),
            pltpu.VMEM((ATT_HEADS * blk, LANES), BF16),
            pltpu.VMEM((IDX_HEADS * blk, LANES), BF16),
            pltpu.VMEM((LANES, blk), F32),
            pltpu.VMEM((2, blk, blk), F32),
            pltpu.VMEM((SUBLANES, blk), F32),
            pltpu.VMEM((ATT_HEADS, SUBLANES, blk), F32),
            pltpu.VMEM((ATT_HEADS, SUBLANES, blk), F32),
            pltpu.VMEM((D_ATT, blk), F32),
        ],
        compiler_params=_cparams("arbitrary", "arbitrary"),
        name="dsa_attention",
    )(qn, qi, proj, kn, vt, ki, bias)


def _outproj_kernel(x_ref, yssd_ref, yatt_ref, scb_ref, scc_ref, sch_ref, hc_ref, hh_ref,
                    cw_ref, w_ref, o_ref, cat_ref, *, blocks_per_seq):
    i = pl.program_id(0)

    @pl.when(pl.program_id(1) == 0)
    def _():
        prev = jnp.where(i % blocks_per_seq == 0, 0.0, hc_ref[...] * hh_ref[...])
        ysc = scb_ref[...] * _causal_conv(scc_ref[...] * sch_ref[...], prev, cw_ref[...])
        cat_ref[:, :D_SSD] = yssd_ref[...]
        cat_ref[:, D_SSD:D_SSD + D_ATT] = yatt_ref[...]
        cat_ref[:, D_SSD + D_ATT:] = ysc.astype(BF16)

    o_ref[...] = x_ref[...] + _dot(cat_ref[...], w_ref[...])


def _outproj(x, y_ssd, y_att, proj, sc_w, w_out, *, seq, tm, tn):
    n, d = x.shape
    halo = tm // SUBLANES
    blk = lambda col: pl.BlockSpec((tm, D_SC), lambda i, j: (i, col // D_SC))
    hblk = lambda col: pl.BlockSpec(
        (SUBLANES, D_SC), lambda i, j: (jnp.maximum(i * halo - 1, 0), col // D_SC))
    return pl.pallas_call(
        functools.partial(_outproj_kernel, blocks_per_seq=seq // tm),
        out_shape=jax.ShapeDtypeStruct((n, d), F32),
        grid=(n // tm, d // tn),
        in_specs=[
            pl.BlockSpec((tm, tn), lambda i, j: (i, j)),
            pl.BlockSpec((tm, D_SSD), lambda i, j: (i, 0)),
            pl.BlockSpec((tm, D_ATT), lambda i, j: (i, 0)),
            blk(COL_SCB), blk(COL_SCC), blk(COL_SCH), hblk(COL_SCC), hblk(COL_SCH),
            pl.BlockSpec((SC_CONV, D_SC), lambda i, j: (0, 0)),
            pl.BlockSpec((d, tn), lambda i, j: (0, j)),
        ],
        out_specs=pl.BlockSpec((tm, tn), lambda i, j: (i, j)),
        scratch_shapes=[pltpu.VMEM((tm, d), BF16)],
        compiler_params=_cparams("arbitrary", "arbitrary"),
        name="out_proj",
    )(x, y_ssd, y_att, proj, proj, proj, proj, proj, sc_w, w_out)


def _ffn_kernel(x_ref, g_ref, wg_ref, wu_ref, cw_ref, wd_ref, o_ref, h_ref, acc_ref, halo_ref,
                *, blocks_per_seq):
    i = pl.program_id(0)
    j = pl.program_id(1)
    tm = x_ref.shape[0]

    @pl.when(j == 0)
    def _():
        x = x_ref[...]
        ms = jnp.mean(x * x, axis=-1, keepdims=True)
        h_ref[...] = (x * lax.rsqrt(ms + NORM_EPS) * g_ref[...]).astype(BF16)
        acc_ref[...] = jnp.zeros(acc_ref.shape, F32)

    @pl.when(i % blocks_per_seq == 0)
    def _():
        halo_ref[j] = jnp.zeros(halo_ref.shape[1:], F32)

    h = h_ref[...]
    gate = _dot(h, wg_ref[...])
    up = _dot(h, wu_ref[...])
    prev = halo_ref[j]
    halo_ref[j] = gate[tm - SUBLANES:, :]
    gc = _causal_conv(gate, prev, cw_ref[...])
    act = (gc * _sigmoid(gc) * up).astype(BF16)
    acc_ref[...] += _dot(act, wd_ref[...])

    @pl.when(j == pl.num_programs(1) - 1)
    def _():
        o_ref[...] = x_ref[...] + acc_ref[...]


def _ffn(x, g, wg, wu, cw, wd, *, seq, tm, tf):
    n, d = x.shape
    dff = wg.shape[1]
    return pl.pallas_call(
        functools.partial(_ffn_kernel, blocks_per_seq=seq // tm),
        out_shape=jax.ShapeDtypeStruct((n, d), F32),
        grid=(n // tm, dff // tf),
        in_specs=[
            pl.BlockSpec((tm, d), lambda i, j: (i, 0)),
            pl.BlockSpec((1, d), lambda i, j: (0, 0)),
            pl.BlockSpec((d, tf), lambda i, j: (0, j)),
            pl.BlockSpec((d, tf), lambda i, j: (0, j)),
            pl.BlockSpec((FFN_CONV, tf), lambda i, j: (0, j)),
            pl.BlockSpec((tf, d), lambda i, j: (j, 0)),
        ],
        out_specs=pl.BlockSpec((tm, d), lambda i, j: (i, 0)),
        scratch_shapes=[
            pltpu.VMEM((tm, d), BF16),
            pltpu.VMEM((tm, d), F32),
            pltpu.VMEM((dff // tf, SUBLANES, tf), F32),
        ],
        compiler_params=_cparams("arbitrary", "arbitrary"),
        name="conv_gated_mlp",
    )(x, g, wg, wu, cw, wd)


_IN_SIZES = (D_SSD, SSD_CONV_DIM, SSD_HEADS, D_ATT, D_ATT, D_ATT,
             IDX_HEADS * IDX_DIM, IDX_DIM, IDX_HEADS, D_SC, D_SC, D_SC)
_IN_OFFS = tuple(sum(_IN_SIZES[:t]) for t in range(len(_IN_SIZES) + 1))
D_IN_PROJ = _IN_OFFS[-1]
_WIDE_SEGMENTS = ((COL_XBC, 1), (COL_Q, 3), (COL_Z, 0), (COL_K, 4), (COL_V, 5), (COL_QIDX, 6),
                  (COL_SCB, 9), (COL_SCC, 10), (COL_SCH, 11))


def _pack_w_in_kernel(w_ref, o_ref):
    x = w_ref[...]
    tm = x.shape[0]
    for dst, seg in _WIDE_SEGMENTS:
        o_ref[:, dst:dst + _IN_SIZES[seg]] = x[:, _IN_OFFS[seg]:_IN_OFFS[seg + 1]].astype(BF16)
    lane = lax.broadcasted_iota(jnp.int32, (tm, LANES), 1)

    def narrow(seg, copies):
        src, width = _IN_OFFS[seg], _IN_SIZES[seg]
        base = (src // LANES) * LANES
        assert src + width <= base + LANES
        win = x[:, base:base + LANES]
        out = jnp.zeros((tm, LANES), F32)
        for c in range(copies):
            shift = (c * width - (src - base)) % LANES
            moved = pltpu.roll(win, shift, 1) if shift else win
            out = out + jnp.where((lane >= c * width) & (lane < (c + 1) * width), moved, 0.0)
        return out.astype(BF16)

    o_ref[:, COL_DT:COL_DT + LANES] = narrow(2, 1)
    o_ref[:, COL_KIDX:COL_KIDX + LANES] = narrow(7, LANES // IDX_DIM)
    o_ref[:, COL_WIDX:COL_WIDX + LANES] = narrow(8, 1)
    o_ref[:, COL_WIDX + LANES:] = jnp.zeros((tm, D_PACKED - COL_WIDX - LANES), BF16)


def _pack_w_in(w_in, *, tm=128):
    depth, d, _ = w_in.shape
    return pl.pallas_call(
        _pack_w_in_kernel,
        out_shape=jax.ShapeDtypeStruct((depth, d, D_PACKED), BF16),
        grid=(depth, d // tm),
        in_specs=[pl.BlockSpec((None, tm, D_IN_PROJ), lambda l, i: (l, i, 0))],
        out_specs=pl.BlockSpec((None, tm, D_PACKED), lambda l, i: (l, i, 0)),
        compiler_params=_cparams("arbitrary", "arbitrary"),
        name="pack_w_in",
    )(w_in)


def _forward(x, norm_mix, w_in, ssd_conv_w, ssd_conv_b, ssd_dt_bias, ssd_a_log, ssd_d, ssd_norm,
             att_q_norm, att_k_norm, rel_bias, sc_conv_w, w_out, norm_ffn,
             ffn_w_gate, ffn_w_up, ffn_conv_w, ffn_w_down, *, tm_proj, tn_proj, tm_out, tn_out,
             tm_ffn, tf_ffn, tm_prep, att_blk):
    bsz, seq, d = x.shape
    depth = w_in.shape[0]
    n = bsz * seq
    xf = x.reshape(n, d)
    w_in_p = _pack_w_in(w_in)
    w_out_b = w_out.astype(BF16)
    wg_b = ffn_w_gate.astype(BF16)
    wu_b = ffn_w_up.astype(BF16)
    wd_b = ffn_w_down.astype(BF16)
    bias = _bias_tiles(rel_bias, att_blk)
    for l in range(depth):
        proj = _rms_matmul(xf, norm_mix[l].reshape(1, d), w_in_p[l], tm=tm_proj, tn=tn_proj)
        y_ssd = _ssd(proj, ssd_conv_w[l], ssd_conv_b[l], ssd_dt_bias[l], ssd_a_log[l], ssd_d[l],
                     ssd_norm[l], bsz=bsz, seq=seq)
        qn, kn, vt, qi, ki = _dsa_prep(proj, att_q_norm[l], att_k_norm[l], tm=tm_prep, blk=att_blk)
        y_att = _dsa(qn, qi, proj, kn, vt, ki, bias, bsz=bsz, seq=seq, blk=att_blk)
        xf = _outproj(xf, y_ssd, y_att, proj, sc_conv_w[l], w_out_b[l], seq=seq, tm=tm_out, tn=tn_out)
        xf = _ffn(xf, norm_ffn[l].reshape(1, d), wg_b[l], wu_b[l], ffn_conv_w[l], wd_b[l],
                  seq=seq, tm=tm_ffn, tf=tf_ffn)
    return xf.reshape(bsz, seq, d)


def kernel(x, norm_mix, w_in, ssd_conv_w, ssd_conv_b, ssd_dt_bias, ssd_a_log, ssd_d, ssd_norm, att_q_norm, att_k_norm, rel_bias, sc_conv_w, w_out, norm_ffn, ffn_w_gate, ffn_w_up, ffn_conv_w, ffn_w_down):
    return _forward(x, norm_mix, w_in, ssd_conv_w, ssd_conv_b, ssd_dt_bias, ssd_a_log, ssd_d, ssd_norm,
                    att_q_norm, att_k_norm, rel_bias, sc_conv_w, w_out, norm_ffn,
                    ffn_w_gate, ffn_w_up, ffn_conv_w, ffn_w_down,
                    tm_proj=1024, tn_proj=512, tm_out=512, tn_out=2048,
                    tm_ffn=512, tf_ffn=512, tm_prep=512, att_blk=256)
```

```python
import functools
import math

import jax
import jax.numpy as jnp
from jax import lax
from jax.experimental import pallas as pl
from jax.experimental.pallas import tpu as pltpu

F32 = jnp.float32
BF16 = jnp.bfloat16
HIGHEST = lax.Precision.HIGHEST

LANES = 128
SUBLANES = 8
VMEM_LIMIT_BYTES = 56 * 1024 * 1024

D_MODEL = 2048
D_SSD = 1024
D_ATT = 512
D_SC = 512
SSD_HEAD_DIM = 64
SSD_HEADS = 16
SSD_GROUPS = 2
SSD_STATE = 128
SSD_CONV = 4
SSD_CHUNK = 128
SSD_CONV_DIM = D_SSD + 2 * SSD_GROUPS * SSD_STATE
ATT_HEAD_DIM = 64
ATT_HEADS = 8
IDX_HEADS = 8
IDX_DIM = 64
TOPK_MAX = 256
REL_BUCKETS = 32
REL_MAX_DIST = 128
SC_CONV = 3
D_FF = 5632
FFN_CONV = 3
NORM_EPS = 1e-6

COL_XBC = 0
COL_Q = 1536
COL_Z = 2048
COL_K = 3072
COL_V = 3584
COL_QIDX = 4096
COL_SCB = 4608
COL_SCC = 5120
COL_SCH = 5632
COL_DT = 6144
COL_KIDX = 6272
COL_WIDX = 6400
D_PACKED = 6656

NEG_BIG = -1e30
LOG2E = math.log2(math.e)


def _cparams(*sem):
    return pltpu.CompilerParams(dimension_semantics=sem, vmem_limit_bytes=VMEM_LIMIT_BYTES)


def _dot(a, b):
    return jnp.dot(a, b, preferred_element_type=F32)


def _dot_nt(a, b):
    return lax.dot_general(a, b, (((1,), (1,)), ((), ())), preferred_element_type=F32)


def _dot_exact(a, b):
    return jnp.dot(a, b, preferred_element_type=F32, precision=HIGHEST)


def _expand_exact(x, onehot3):
    x1 = x.astype(BF16)
    r1 = x - x1.astype(F32)
    x2 = r1.astype(BF16)
    x3 = (r1 - x2.astype(F32)).astype(BF16)
    return _dot(jnp.concatenate([x1, x2, x3], axis=1), onehot3)


def _sigmoid(x):
    return 1.0 / (1.0 + jnp.exp(-x))


def _shift_rows(x, prev8, s):
    xr = pltpu.roll(x, s, 0)
    pr = pltpu.roll(prev8, s, 0)
    rows = lax.broadcasted_iota(jnp.int32, (SUBLANES, x.shape[1]), 0)
    top = jnp.where(rows < s, pr, xr[:SUBLANES])
    return jnp.concatenate([top, xr[SUBLANES:]], axis=0)


def _causal_conv(x, prev8, w):
    k = w.shape[0]
    y = w[k - 1:k] * x
    for s in range(1, k):
        y = y + w[k - 1 - s:k - s] * _shift_rows(x, prev8, s)
    return y


def _rms_matmul_kernel(x_ref, g_ref, w_ref, o_ref, h_ref):
    @pl.when(pl.program_id(1) == 0)
    def _():
        x = x_ref[...]
        ms = jnp.mean(x * x, axis=-1, keepdims=True)
        h_ref[...] = (x * lax.rsqrt(ms + NORM_EPS) * g_ref[...]).astype(BF16)

    o_ref[...] = _dot_nt(h_ref[...], w_ref[...])


def _rms_matmul(x, g, w_t, layer, *, tm, tn):
    n, d = x.shape
    dout = w_t.shape[1]
    return pl.pallas_call(
        _rms_matmul_kernel,
        out_shape=jax.ShapeDtypeStruct((n, dout), F32),
        grid=(n // tm, dout // tn),
        in_specs=[
            pl.BlockSpec((tm, d), lambda i, j: (i, 0)),
            pl.BlockSpec((1, d), lambda i, j: (0, 0)),
            pl.BlockSpec((None, tn, d), lambda i, j: (layer, j, 0)),
        ],
        out_specs=pl.BlockSpec((tm, tn), lambda i, j: (i, j)),
        scratch_shapes=[pltpu.VMEM((tm, d), BF16)],
        compiler_params=_cparams("arbitrary", "arbitrary"),
        name="rms_in_proj",
    )(x, g, w_t)


def _ssd_kernel(xbc_ref, halo_ref, z_ref, dt_ref, cw_ref, cb_ref, dtb_ref, alog_ref,
                dx_ref, ng_ref, ex_ref, o_ref, h_ref, y_ref):
    c = pl.program_id(1)
    L = SSD_CHUNK

    @pl.when(c == 0)
    def _():
        h_ref[...] = jnp.zeros(h_ref.shape, F32)

    prev = jnp.where(c == 0, 0.0, halo_ref[...])
    xbc = _causal_conv(xbc_ref[...], prev, cw_ref[...]) + cb_ref[...]
    xbc = xbc * _sigmoid(xbc)
    xs = xbc[:, :D_SSD]

    row = lax.broadcasted_iota(jnp.int32, (L, L), 0)
    col = lax.broadcasted_iota(jnp.int32, (L, L), 1)
    tril = (row >= col).astype(F32)

    def softplus(v):
        return jnp.maximum(v, 0.0) + jnp.log1p(jnp.exp(-jnp.abs(v)))

    dt_s = softplus(dt_ref[...] + dtb_ref[...])
    a_s = dt_s * (-jnp.exp(alog_ref[...]))
    acs_s = _dot_exact(tril, a_s)
    acs_t = acs_s.T
    ex = ex_ref[...]
    dt_x = _expand_exact(dt_s, ex)
    e_x = _expand_exact(jnp.exp(acs_s), ex)
    de_x = _expand_exact(jnp.exp(acs_s[L - 1:L, :] - acs_s), ex)
    decay = e_x[L - 1:L, :]
    xdt = xs * dt_x
    xdt_end = (xdt * de_x).astype(BF16)
    xdt = xdt.astype(BF16)

    lane = lax.broadcasted_iota(jnp.int32, (L, LANES), 1)
    lo = lane < SSD_HEAD_DIM
    lower = row >= col

    for g in range(SSD_GROUPS):
        bm = xbc[:, D_SSD + g * SSD_STATE:D_SSD + (g + 1) * SSD_STATE].astype(BF16)
        cm = xbc[:, D_SSD + (SSD_GROUPS + g) * SSD_STATE:
                 D_SSD + (SSD_GROUPS + g + 1) * SSD_STATE].astype(BF16)
        cb = _dot_nt(cm, bm)
        bm_t = bm.T
        for q in range(4):
            pair = g * 4 + q
            sl = slice(pair * LANES, (pair + 1) * LANES)
            gs = []
            for hh in range(2):
                h = 2 * pair + hh
                seg = (jnp.broadcast_to(acs_s[:, h:h + 1], (L, L))
                       - jnp.broadcast_to(acs_t[h:h + 1, :], (L, L)))
                lm = jnp.where(lower, jnp.exp(seg), 0.0)
                gs.append((cb * lm).astype(BF16))
            gpair = jnp.concatenate(gs, axis=1)
            xp = xdt[:, sl]
            zero = jnp.zeros_like(xp)
            x2 = jnp.concatenate([jnp.where(lo, xp, zero), jnp.where(lo, zero, xp)], axis=0)
            y = _dot(gpair, x2)
            hprev = h_ref[pair]
            y = y + e_x[:, sl] * _dot(cm, hprev.astype(BF16))
            y = y + xs[:, sl] * dx_ref[:, sl]
            h_ref[pair] = hprev * decay[:, sl] + _dot(bm_t, xdt_end[:, sl])
            y_ref[:, sl] = y

    z = z_ref[...]
    y = y_ref[...] * (z * _sigmoid(z))
    half = D_SSD // SSD_GROUPS
    outs = []
    for g in range(SSD_GROUPS):
        yg = y[:, g * half:(g + 1) * half]
        ms = jnp.mean(yg * yg, axis=-1, keepdims=True)
        outs.append(yg * lax.rsqrt(ms + NORM_EPS))
    o_ref[...] = (jnp.concatenate(outs, axis=1) * ng_ref[...]).astype(BF16)


def _ssd(proj, conv_w, conv_b, dt_bias, a_log, d_skip, norm_g, *, bsz, seq):
    nc = seq // SSD_CHUNK
    n = bsz * seq
    pad = LANES - SSD_HEADS
    dtb = jnp.pad(dt_bias, (0, pad)).reshape(1, LANES)
    alog = jnp.pad(a_log, (0, pad), constant_values=NEG_BIG).reshape(1, LANES)
    rep = lambda v: jnp.repeat(v, SSD_HEAD_DIM).reshape(1, D_SSD)
    expand = (jnp.arange(LANES)[:, None] == (jnp.arange(D_SSD)[None, :] // SSD_HEAD_DIM)).astype(BF16)
    expand = jnp.concatenate([expand] * 3, axis=0)
    rows_per_halo = SSD_CHUNK // SUBLANES
    const = lambda shape: pl.BlockSpec(shape, lambda b, c: (0,) * len(shape))
    return pl.pallas_call(
        _ssd_kernel,
        out_shape=jax.ShapeDtypeStruct((n, D_SSD), BF16),
        grid=(bsz, nc),
        in_specs=[
            pl.BlockSpec((SSD_CHUNK, SSD_CONV_DIM), lambda b, c: (b * nc + c, COL_XBC // SSD_CONV_DIM)),
            pl.BlockSpec((SUBLANES, SSD_CONV_DIM),
                         lambda b, c: (jnp.maximum((b * nc + c) * rows_per_halo - 1, 0), 0)),
            pl.BlockSpec((SSD_CHUNK, D_SSD), lambda b, c: (b * nc + c, COL_Z // D_SSD)),
            pl.BlockSpec((SSD_CHUNK, LANES), lambda b, c: (b * nc + c, COL_DT // LANES)),
            const((SSD_CONV, SSD_CONV_DIM)),
            const((1, SSD_CONV_DIM)),
            const((1, LANES)),
            const((1, LANES)),
            const((1, D_SSD)),
            const((1, D_SSD)),
            const((3 * LANES, D_SSD)),
        ],
        out_specs=pl.BlockSpec((SSD_CHUNK, D_SSD), lambda b, c: (b * nc + c, 0)),
        scratch_shapes=[pltpu.VMEM((SSD_HEADS // 2, SSD_STATE, LANES), F32),
                        pltpu.VMEM((SSD_CHUNK, D_SSD), F32)],
        compiler_params=_cparams("arbitrary", "arbitrary"),
        name="ssd_scan",
    )(proj, proj, proj, proj, conv_w, conv_b.reshape(1, -1), dtb, alog,
      rep(d_skip), norm_g.reshape(1, -1), expand)


def _dsa_prep_kernel(q_ref, k_ref, v_ref, qi_ref, ki_ref, qg_ref, kg_ref, seg_ref,
                     qn_ref, kn_ref, vt_ref, qib_ref, kib_ref, *, blk):
    def head_norm(x, g):
        ms = _expand_exact(x * x, seg_ref[...]) * (1.0 / ATT_HEAD_DIM)
        return x * lax.rsqrt(ms + NORM_EPS) * g

    qn_ref[...] = (head_norm(q_ref[...], qg_ref[...]) * (LOG2E * ATT_HEAD_DIM ** -0.5)).astype(BF16)
    kn_ref[...] = head_norm(k_ref[...], kg_ref[...]).astype(BF16)
    for c in range(vt_ref.shape[0]):
        vt_ref[c] = v_ref[c * blk:(c + 1) * blk, :].T.astype(BF16)
    qib_ref[...] = qi_ref[...].astype(BF16)
    kib_ref[...] = ki_ref[...].astype(BF16)


def _dsa_prep(proj, q_norm, k_norm, *, tm, blk):
    n = proj.shape[0]
    seg = (jnp.arange(D_ATT)[:, None] // ATT_HEAD_DIM
           == jnp.arange(D_ATT)[None, :] // ATT_HEAD_DIM).astype(BF16)
    seg = jnp.concatenate([seg] * 3, axis=0)
    tile = lambda v: jnp.tile(v, ATT_HEADS).reshape(1, D_ATT)
    col_blk = lambda col, w: pl.BlockSpec((tm, w), lambda i: (i, col // w))
    const = lambda shape: pl.BlockSpec(shape, lambda i: (0,) * len(shape))
    return pl.pallas_call(
        functools.partial(_dsa_prep_kernel, blk=blk),
        out_shape=(
            jax.ShapeDtypeStruct((n, D_ATT), BF16),
            jax.ShapeDtypeStruct((n, D_ATT), BF16),
            jax.ShapeDtypeStruct((n // blk, D_ATT, blk), BF16),
            jax.ShapeDtypeStruct((n, D_ATT), BF16),
            jax.ShapeDtypeStruct((n, LANES), BF16),
        ),
        grid=(n // tm,),
        in_specs=[col_blk(COL_Q, D_ATT), col_blk(COL_K, D_ATT), col_blk(COL_V, D_ATT),
                  col_blk(COL_QIDX, D_ATT), col_blk(COL_KIDX, LANES),
                  const((1, D_ATT)), const((1, D_ATT)), const((3 * D_ATT, D_ATT))],
        out_specs=(
            pl.BlockSpec((tm, D_ATT), lambda i: (i, 0)),
            pl.BlockSpec((tm, D_ATT), lambda i: (i, 0)),
            pl.BlockSpec((tm // blk, D_ATT, blk), lambda i: (i, 0, 0)),
            pl.BlockSpec((tm, D_ATT), lambda i: (i, 0)),
            pl.BlockSpec((tm, LANES), lambda i: (i, 0)),
        ),
        compiler_params=_cparams("arbitrary"),
        name="dsa_prep",
    )(proj, proj, proj, proj, proj, tile(q_norm), tile(k_norm), seg)


def _dsa_kernel(qn_ref, qi_ref, w_ref, kn_ref, vt_ref, ki_ref, bias_ref, o_ref,
                key_ref, hi_ref, lo_ref, big_ref, qm_ref, qim_ref, wt_ref, mb_ref, carry_ref,
                m_ref, l_ref, acc_ref, *, blk, topk):
    i = pl.program_id(1)
    pack = 16
    lane = lax.broadcasted_iota(jnp.int32, (blk, LANES), 1)
    lo_lanes = lane < ATT_HEAD_DIM
    krow = lax.broadcasted_iota(jnp.int32, (blk, blk), 0)
    qcol = lax.broadcasted_iota(jnp.int32, (blk, blk), 1)
    future = krow > qcol

    for p in range(ATT_HEADS // 2):
        sl = slice(p * LANES, (p + 1) * LANES)
        qp = qn_ref[:, sl]
        qip = qi_ref[:, sl]
        zero = jnp.zeros_like(qp)
        qm_ref[2 * p * blk:(2 * p + 1) * blk] = jnp.where(lo_lanes, qp, zero)
        qm_ref[(2 * p + 1) * blk:(2 * p + 2) * blk] = jnp.where(lo_lanes, zero, qp)
        qim_ref[2 * p * blk:(2 * p + 1) * blk] = jnp.where(lo_lanes, qip, zero)
        qim_ref[(2 * p + 1) * blk:(2 * p + 2) * blk] = jnp.where(lo_lanes, zero, qip)
    wt_ref[...] = w_ref[...].T

    def score_dots(j, slot):
        kk = ki_ref[pl.ds(pl.multiple_of(j * blk, blk), blk), :]
        big_ref[slot] = _dot_nt(kk, qim_ref[...])

    def score_finish(j, slot, diag):
        s = jnp.zeros((blk, blk), F32)
        for h in range(IDX_HEADS):
            s = s + jnp.maximum(big_ref[slot, :, h * blk:(h + 1) * blk], 0.0) * wt_ref[h:h + 1, :]
        if diag:
            s = jnp.where(future, -jnp.inf, s)
        s = jnp.where(s == 0.0, 0.0, s)
        bits = pltpu.bitcast(s, jnp.int32)
        key = bits ^ ((bits >> 31) & 0x7FFFFFFF)
        key_ref[j] = key
        hi_ref[j] = (key >> 16).astype(jnp.int16)
        lo_ref[j] = ((key & 0xFFFF) - 32768).astype(jnp.int16)

    def by_parity(x, fn):
        @pl.when(x % 2 == 0)
        def _():
            fn(0)

        @pl.when(x % 2 == 1)
        def _():
            fn(1)

    score_dots(0, 0)

    def score_pair(t, carry):
        score_dots(2 * t + 1, 1)
        score_finish(2 * t, 0, False)
        score_dots(2 * t + 2, 0)
        score_finish(2 * t + 1, 1, False)
        return carry

    lax.fori_loop(0, i // 2, score_pair, 0)

    @pl.when(i % 2 == 1)
    def _():
        score_dots(i, 1)
        score_finish(i - 1, 0, False)

    by_parity(i, lambda slot: score_finish(i, slot, True))

    one16 = jnp.ones((), jnp.int16)
    zero16 = jnp.zeros((), jnp.int16)

    def count_ge(src_ref, cand):
        cand_b = jnp.broadcast_to(cand.astype(jnp.int16), (blk, blk))

        def add_tile(j, cnts, on):
            ind = jnp.where(src_ref[j] >= cand_b, on, zero16)
            cnts = list(cnts)
            for g in range(blk // pack):
                cnts[g % len(cnts)] = cnts[g % len(cnts)] + ind[g * pack:(g + 1) * pack]
            return tuple(cnts)

        def body(t, cnts):
            return add_tile(2 * t + 1, add_tile(2 * t, cnts, one16), one16)

        zeros = tuple(jnp.zeros((pack, blk), jnp.int16) for _ in range(4))
        cnts = lax.fori_loop(0, (i + 1) // 2, body, zeros)
        cnts = add_tile(i, cnts, ((i + 1) % 2).astype(jnp.int16))
        cnt = (cnts[0] + cnts[1]) + (cnts[2] + cnts[3])
        return jnp.sum(cnt.astype(jnp.int32).astype(F32), axis=0, keepdims=True)

    def bisect16(src_ref, rank, cnt_all):
        def bit_body(b, st):
            prefix, c_acc, c_rej = st
            cand = prefix + lax.shift_left(jnp.int32(1), 15 - b)
            cnt = count_ge(src_ref, cand)
            ok = cnt >= rank
            return (jnp.where(ok, cand, prefix), jnp.where(ok, cnt, c_acc), jnp.where(ok, c_rej, cnt))
        init = (jnp.full((1, blk), -32768, jnp.int32), cnt_all, jnp.zeros((1, blk), F32))
        return lax.fori_loop(0, 16, bit_body, init)

    ncols = jnp.full((1, blk), ((i + 1) * blk).astype(F32), F32)
    p_hi, c_acc1, c_rej1 = bisect16(hi_ref, jnp.float32(topk), ncols)
    p_hi_b = jnp.broadcast_to(p_hi.astype(jnp.int16), (blk, blk))

    def group_body(j, carry):
        lo_ref[j] = jnp.where(hi_ref[j] == p_hi_b, lo_ref[j], jnp.int16(-32768))
        return carry

    lax.fori_loop(0, i + 1, group_body, 0)
    rank2 = jnp.float32(topk) - c_rej1
    p_lo, c_acc2, c_rej2 = bisect16(lo_ref, rank2, c_acc1 - c_rej1)
    thr = lax.shift_left(p_hi, 16) + (p_lo + 32768)
    need = rank2 - c_rej2
    has_ties = jnp.max((c_acc2 - c_rej2) - need) > 0.5

    m_ref[...] = jnp.full(m_ref.shape, NEG_BIG, F32)
    l_ref[...] = jnp.zeros(l_ref.shape, F32)
    acc_ref[...] = jnp.zeros(acc_ref.shape, F32)
    carry_ref[...] = jnp.zeros(carry_ref.shape, F32)

    def qk_dots(j, slot):
        start = pl.multiple_of(j * blk, blk)
        for p in range(ATT_HEADS // 2):
            mb2 = jnp.concatenate([mb_ref[slot], mb_ref[slot]], axis=1)
            big_ref[slot, :, 2 * p * blk:(2 * p + 2) * blk] = _dot_nt(
                kn_ref[pl.ds(start, blk), p * LANES:(p + 1) * LANES], qm_ref[2 * p * blk:(2 * p + 2) * blk]) + mb2

    def tile_mask(j, kind, slot):
        kt = key_ref[j]

        @pl.when(jnp.logical_not(has_ties))
        def _():
            mb = jnp.where(kt >= thr, 0.0, NEG_BIG)
            mb_ref[slot] = jnp.where(future, NEG_BIG, mb) if kind == 2 else mb

        @pl.when(has_ties)
        def _():
            eq = jnp.where(kt == thr, 1.0, 0.0)
            lower = jnp.where(krow >= qcol, 1.0, 0.0).astype(BF16)
            seen = _dot(lower, eq.astype(BF16)) + carry_ref[0:1, :]
            keep = jnp.where(seen <= need, 0.0, NEG_BIG)
            mb = jnp.where(kt > thr, 0.0, jnp.where(kt == thr, keep, NEG_BIG))
            mb_ref[slot] = jnp.where(future, NEG_BIG, mb) if kind == 2 else mb
            carry_ref[0:1, :] += jnp.sum(eq, axis=0, keepdims=True)

    ones_rows = jnp.ones((pack, blk), BF16)

    def softmax_pv(j, slot, kind):
        vt = vt_ref[j]
        for h in range(ATT_HEADS):
            half = blk // 2
            s = big_ref[slot, :, h * blk:(h + 1) * blk]
            if kind > 0:
                s = s + bias_ref[kind - 1, h]
            m_prev = m_ref[h]
            m_new = jnp.maximum(m_prev, jnp.max(s, axis=0, keepdims=True))
            alpha = jnp.exp2(m_prev - m_new)
            if kind > 0:
                pexp = jnp.exp2(s - m_new[0:1, :]).astype(BF16)
            else:
                pexp = jnp.concatenate(
                    [jnp.exp2(big_ref[slot, r * half:(r + 1) * half, h * blk:(h + 1) * blk]
                              - m_new[0:1, :]).astype(BF16) for r in range(2)], axis=0)
            m_ref[h] = m_new
            ch = slice(h * ATT_HEAD_DIM, (h + 1) * ATT_HEAD_DIM)
            pv = _dot(jnp.concatenate([vt[ch, :], ones_rows], axis=0), pexp)
            l_ref[h] = alpha * l_ref[h] + pv[ATT_HEAD_DIM:ATT_HEAD_DIM + 1, :]
            acc_ref[ch, :] = alpha[0:1, :] * acc_ref[ch, :] + pv[:ATT_HEAD_DIM, :]

    n_far = i - 1

    @pl.when(i == 0)
    def _():
        tile_mask(0, 2, 0)

    @pl.when(i > 0)
    def _():
        tile_mask(0, 0, 0)

    qk_dots(0, 0)

    def attn_pair(t, carry):
        tile_mask(2 * t + 1, 0, 1)
        tile_mask(2 * t + 2, 0, 0)
        qk_dots(2 * t + 1, 1)
        softmax_pv(2 * t, 0, 0)
        qk_dots(2 * t + 2, 0)
        softmax_pv(2 * t + 1, 1, 0)
        return carry

    lax.fori_loop(0, n_far // 2, attn_pair, 0)

    @pl.when(jnp.logical_and(n_far > 0, n_far % 2 == 1))
    def _():
        tile_mask(n_far, 0, 1)
        qk_dots(n_far, 1)
        softmax_pv(n_far - 1, 0, 0)

    def near_tiles(slot):
        @pl.when(i > 0)
        def _():
            tile_mask(i, 2, 1 - slot)
            qk_dots(i, 1 - slot)
            softmax_pv(i - 1, slot, 1)

    by_parity(i - 1, near_tiles)
    by_parity(i, lambda slot: softmax_pv(i, slot, 2))

    outs = []
    for h in range(ATT_HEADS):
        ch = slice(h * ATT_HEAD_DIM, (h + 1) * ATT_HEAD_DIM)
        outs.append(acc_ref[ch, :] / l_ref[h][0:1, :])
    o_ref[...] = jnp.concatenate(outs, axis=0).T.astype(BF16)


def _t5_bucket(dist):
    n = jnp.maximum(dist, 0)
    max_exact = REL_BUCKETS // 2
    large = max_exact + (jnp.log(jnp.maximum(n, max_exact).astype(F32) / max_exact)
                         / math.log(REL_MAX_DIST / max_exact)
                         * (REL_BUCKETS - max_exact)).astype(jnp.int32)
    large = jnp.minimum(large, REL_BUCKETS - 1)
    return jnp.where(n < max_exact, n, large)


def _bias_tiles(rel_bias, blk):
    width = 2 * blk
    f = ((rel_bias[_t5_bucket(jnp.arange(width, dtype=jnp.int32))] - rel_bias[REL_BUCKETS - 1]) * LOG2E).T
    h_sub = jnp.concatenate([f[:, blk:], f[:, :blk]], axis=1)
    h = jnp.stack([h_sub, f])
    skew = jnp.tile(h, (1, 1, blk))[:, :, :blk * (width - 1)].reshape(2, ATT_HEADS, blk, width - 1)
    return skew[:, :, :, :blk]


def _dsa(qn, qi, proj, kn, vt, ki, bias, *, bsz, seq, blk):
    n = bsz * seq
    nb = seq // blk
    topk = min(TOPK_MAX, seq // 4)
    once = pl.Buffered(1)
    return pl.pallas_call(
        functools.partial(_dsa_kernel, blk=blk, topk=topk),
        out_shape=jax.ShapeDtypeStruct((n, D_ATT), BF16),
        grid=(bsz, nb),
        in_specs=[
            pl.BlockSpec((blk, D_ATT), lambda b, i: (b * nb + i, 0)),
            pl.BlockSpec((blk, D_ATT), lambda b, i: (b * nb + i, 0)),
            pl.BlockSpec((blk, LANES), lambda b, i: (b * nb + i, COL_WIDX // LANES)),
            pl.BlockSpec((seq, D_ATT), lambda b, i: (b, 0), pipeline_mode=once),
            pl.BlockSpec((nb, D_ATT, blk), lambda b, i: (b, 0, 0), pipeline_mode=once),
            pl.BlockSpec((seq, LANES), lambda b, i: (b, 0), pipeline_mode=once),
            pl.BlockSpec((2, ATT_HEADS, blk, blk), lambda b, i: (0, 0, 0, 0), pipeline_mode=once),
        ],
        out_specs=pl.BlockSpec((blk, D_ATT), lambda b, i: (b * nb + i, 0)),
        scratch_shapes=[
            pltpu.VMEM((nb, blk, blk), jnp.int32),
            pltpu.VMEM((nb, blk, blk), jnp.int16),
            pltpu.VMEM((nb, blk, blk), jnp.int16),
            pltpu.VMEM((2, blk, ATT_HEADS * blk), F32),
            pltpu.VMEM((ATT_HEADS * blk, LANES), BF16),
            pltpu.VMEM((IDX_HEADS * blk, LANES), BF16),
            pltpu.VMEM((LANES, blk), F32),
            pltpu.VMEM((2, blk, blk), F32),
            pltpu.VMEM((SUBLANES, blk), F32),
            pltpu.VMEM((ATT_HEADS, SUBLANES, blk), F32),
            pltpu.VMEM((ATT_HEADS, SUBLANES, blk), F32),
            pltpu.VMEM((D_ATT, blk), F32),
        ],
        compiler_params=_cparams("arbitrary", "arbitrary"),
        name="dsa_attention",
    )(qn, qi, proj, kn, vt, ki, bias)


def _outproj_kernel(x_ref, yssd_ref, yatt_ref, scb_ref, scc_ref, sch_ref, hc_ref, hh_ref,
                    cw_ref, w_ref, o_ref, cat_ref, *, blocks_per_seq):
    i = pl.program_id(0)

    @pl.when(pl.program_id(1) == 0)
    def _():
        prev = jnp.where(i % blocks_per_seq == 0, 0.0, hc_ref[...] * hh_ref[...])
        ysc = scb_ref[...] * _causal_conv(scc_ref[...] * sch_ref[...], prev, cw_ref[...])
        cat_ref[:, :D_SSD] = yssd_ref[...]
        cat_ref[:, D_SSD:D_SSD + D_ATT] = yatt_ref[...]
        cat_ref[:, D_SSD + D_ATT:] = ysc.astype(BF16)

    o_ref[...] = x_ref[...] + _dot(cat_ref[...], w_ref[...])


def _outproj(x, y_ssd, y_att, proj, sc_w, w_out, layer, *, seq, tm, tn):
    n, d = x.shape
    halo = tm // SUBLANES
    blk = lambda col: pl.BlockSpec((tm, D_SC), lambda i, j: (i, col // D_SC))
    hblk = lambda col: pl.BlockSpec(
        (SUBLANES, D_SC), lambda i, j: (jnp.maximum(i * halo - 1, 0), col // D_SC))
    return pl.pallas_call(
        functools.partial(_outproj_kernel, blocks_per_seq=seq // tm),
        out_shape=jax.ShapeDtypeStruct((n, d), F32),
        grid=(n // tm, d // tn),
        in_specs=[
            pl.BlockSpec((tm, tn), lambda i, j: (i, j)),
            pl.BlockSpec((tm, D_SSD), lambda i, j: (i, 0)),
            pl.BlockSpec((tm, D_ATT), lambda i, j: (i, 0)),
            blk(COL_SCB), blk(COL_SCC), blk(COL_SCH), hblk(COL_SCC), hblk(COL_SCH),
            pl.BlockSpec((SC_CONV, D_SC), lambda i, j: (0, 0)),
            pl.BlockSpec((None, d, tn), lambda i, j: (layer, 0, j)),
        ],
        out_specs=pl.BlockSpec((tm, tn), lambda i, j: (i, j)),
        scratch_shapes=[pltpu.VMEM((tm, d), BF16)],
        compiler_params=_cparams("arbitrary", "arbitrary"),
        name="out_proj",
    )(x, y_ssd, y_att, proj, proj, proj, proj, proj, sc_w, w_out)


def _ffn_kernel(x_ref, g_ref, wg_ref, wu_ref, cw_ref, wd_ref, o_ref, h_ref, acc_ref, halo_ref,
                *, blocks_per_seq):
    i = pl.program_id(0)
    j = pl.program_id(1)
    tm = x_ref.shape[0]

    @pl.when(j == 0)
    def _():
        x = x_ref[...]
        ms = jnp.mean(x * x, axis=-1, keepdims=True)
        h_ref[...] = (x * lax.rsqrt(ms + NORM_EPS) * g_ref[...]).astype(BF16)
        acc_ref[...] = jnp.zeros(acc_ref.shape, F32)

    @pl.when(i % blocks_per_seq == 0)
    def _():
        halo_ref[j] = jnp.zeros(halo_ref.shape[1:], F32)

    h = h_ref[...]
    gate = _dot(h, wg_ref[...])
    up = _dot(h, wu_ref[...])
    prev = halo_ref[j]
    halo_ref[j] = gate[tm - SUBLANES:, :]
    gc = _causal_conv(gate, prev, cw_ref[...])
    act = (gc * _sigmoid(gc) * up).astype(BF16)
    acc_ref[...] += _dot(act, wd_ref[...])

    @pl.when(j == pl.num_programs(1) - 1)
    def _():
        o_ref[...] = x_ref[...] + acc_ref[...]


def _ffn(x, g, wg, wu, cw, wd, layer, *, seq, tm, tf):
    n, d = x.shape
    dff = wg.shape[2]
    return pl.pallas_call(
        functools.partial(_ffn_kernel, blocks_per_seq=seq // tm),
        out_shape=jax.ShapeDtypeStruct((n, d), F32),
        grid=(n // tm, dff // tf),
        in_specs=[
            pl.BlockSpec((tm, d), lambda i, j: (i, 0)),
            pl.BlockSpec((1, d), lambda i, j: (0, 0)),
            pl.BlockSpec((None, d, tf), lambda i, j: (layer, 0, j)),
            pl.BlockSpec((None, d, tf), lambda i, j: (layer, 0, j)),
            pl.BlockSpec((FFN_CONV, tf), lambda i, j: (0, j)),
            pl.BlockSpec((None, tf, d), lambda i, j: (layer, j, 0)),
        ],
        out_specs=pl.BlockSpec((tm, d), lambda i, j: (i, 0)),
        scratch_shapes=[
            pltpu.VMEM((tm, d), BF16),
            pltpu.VMEM((tm, d), F32),
            pltpu.VMEM((dff // tf, SUBLANES, tf), F32),
        ],
        compiler_params=_cparams("arbitrary", "arbitrary"),
        name="conv_gated_mlp",
    )(x, g, wg, wu, cw, wd)


_IN_SIZES = (D_SSD, SSD_CONV_DIM, SSD_HEADS, D_ATT, D_ATT, D_ATT,
             IDX_HEADS * IDX_DIM, IDX_DIM, IDX_HEADS, D_SC, D_SC, D_SC)
_IN_OFFS = tuple(sum(_IN_SIZES[:t]) for t in range(len(_IN_SIZES) + 1))
D_IN_PROJ = _IN_OFFS[-1]
_WIDE_SEGMENTS = ((COL_XBC, 1), (COL_Q, 3), (COL_Z, 0), (COL_K, 4), (COL_V, 5), (COL_QIDX, 6),
                  (COL_SCB, 9), (COL_SCC, 10), (COL_SCH, 11))


def _pack_w_in_kernel(w_ref, o_ref):
    tc = o_ref.shape[1]
    pack = 16

    def rows(seg):
        return w_ref[_IN_OFFS[seg]:_IN_OFFS[seg + 1], :]

    def put(dst, val, total):
        n = val.shape[0]
        pad = (-n) % pack
        if pad:
            val = jnp.concatenate([val, jnp.zeros((pad, tc), F32)], axis=0)
        o_ref[dst:dst + n + pad, :] = val.astype(BF16)
        if total > n + pad:
            o_ref[dst + n + pad:dst + total, :] = jnp.zeros((total - n - pad, tc), BF16)

    for dst, seg in _WIDE_SEGMENTS:
        put(dst, rows(seg), _IN_SIZES[seg])
    put(COL_DT, rows(2), LANES)
    put(COL_KIDX, jnp.concatenate([rows(7)] * (LANES // IDX_DIM), axis=0), LANES)
    put(COL_WIDX, rows(8), D_PACKED - COL_WIDX)


def _pack_w_in(w_in, *, tc=256):
    depth, d, _ = w_in.shape
    w_t = jnp.swapaxes(w_in, 1, 2)
    return pl.pallas_call(
        _pack_w_in_kernel,
        out_shape=jax.ShapeDtypeStruct((depth, D_PACKED, d), BF16),
        grid=(depth, d // tc),
        in_specs=[pl.BlockSpec((None, D_IN_PROJ, tc), lambda l, i: (l, 0, i))],
        out_specs=pl.BlockSpec((None, D_PACKED, tc), lambda l, i: (l, 0, i)),
        compiler_params=_cparams("arbitrary", "arbitrary"),
        name="pack_w_in",
    )(w_t)


def _forward(x, norm_mix, w_in, ssd_conv_w, ssd_conv_b, ssd_dt_bias, ssd_a_log, ssd_d, ssd_norm,
             att_q_norm, att_k_norm, rel_bias, sc_conv_w, w_out, norm_ffn,
             ffn_w_gate, ffn_w_up, ffn_conv_w, ffn_w_down, *, tm_proj, tn_proj, tm_out, tn_out,
             tm_ffn, tf_ffn, tm_prep, att_blk):
    bsz, seq, d = x.shape
    depth = w_in.shape[0]
    n = bsz * seq
    xf = x.reshape(n, d)
    w_in_p = _pack_w_in(w_in)
    w_out_b = w_out.astype(BF16)
    wg_b = ffn_w_gate.astype(BF16)
    wu_b = ffn_w_up.astype(BF16)
    wd_b = ffn_w_down.astype(BF16)
    bias = _bias_tiles(rel_bias, att_blk)
    for l in range(depth):
        proj = _rms_matmul(xf, norm_mix[l].reshape(1, d), w_in_p, l, tm=tm_proj, tn=tn_proj)
        y_ssd = _ssd(proj, ssd_conv_w[l], ssd_conv_b[l], ssd_dt_bias[l], ssd_a_log[l], ssd_d[l],
                     ssd_norm[l], bsz=bsz, seq=seq)
        qn, kn, vt, qi, ki = _dsa_prep(proj, att_q_norm[l], att_k_norm[l], tm=tm_prep, blk=att_blk)
        y_att = _dsa(qn, qi, proj, kn, vt, ki, bias, bsz=bsz, seq=seq, blk=att_blk)
        xf = _outproj(xf, y_ssd, y_att, proj, sc_conv_w[l], w_out_b, l, seq=seq, tm=tm_out, tn=tn_out)
        xf = _ffn(xf, norm_ffn[l].reshape(1, d), wg_b, wu_b, ffn_conv_w[l], wd_b, l,
                  seq=seq, tm=tm_ffn, tf=tf_ffn)
    return xf.reshape(bsz, seq, d)


def kernel(x, norm_mix, w_in, ssd_conv_w, ssd_conv_b, ssd_dt_bias, ssd_a_log, ssd_d, ssd_norm, att_q_norm, att_k_norm, rel_bias, sc_conv_w, w_out, norm_ffn, ffn_w_gate, ffn_w_up, ffn_conv_w, ffn_w_down):
    return _forward(x, norm_mix, w_in, ssd_conv_w, ssd_conv_b, ssd_dt_bias, ssd_a_log, ssd_d, ssd_norm,
                    att_q_norm, att_k_norm, rel_bias, sc_conv_w, w_out, norm_ffn,
                    ffn_w_gate, ffn_w_up, ffn_conv_w, ffn_w_down,
                    tm_proj=1024, tn_proj=512, tm_out=512, tn_out=2048,
                    tm_ffn=512, tf_ffn=512, tm_prep=512, att_blk=256)
```

```python
import functools
import math

import jax
import jax.numpy as jnp
from jax import lax
from jax.experimental import pallas as pl
from jax.experimental.pallas import tpu as pltpu

F32 = jnp.float32
BF16 = jnp.bfloat16
HIGHEST = lax.Precision.HIGHEST

LANES = 128
SUBLANES = 8
VMEM_LIMIT_BYTES = 56 * 1024 * 1024

D_MODEL = 2048
D_SSD = 1024
D_ATT = 512
D_SC = 512
SSD_HEAD_DIM = 64
SSD_HEADS = 16
SSD_GROUPS = 2
SSD_STATE = 128
SSD_CONV = 4
SSD_CHUNK = 128
SSD_CONV_DIM = D_SSD + 2 * SSD_GROUPS * SSD_STATE
ATT_HEAD_DIM = 64
ATT_HEADS = 8
IDX_HEADS = 8
IDX_DIM = 64
TOPK_MAX = 256
REL_BUCKETS = 32
REL_MAX_DIST = 128
SC_CONV = 3
D_FF = 5632
FFN_CONV = 3
NORM_EPS = 1e-6

COL_XBC = 0
COL_Q = 1536
COL_Z = 2048
COL_K = 3072
COL_V = 3584
COL_QIDX = 4096
COL_SCB = 4608
COL_SCC = 5120
COL_SCH = 5632
COL_DT = 6144
COL_KIDX = 6272
COL_WIDX = 6400
D_PACKED = 6656

NEG_BIG = -1e30
LOG2E = math.log2(math.e)


def _cparams(*sem):
    return pltpu.CompilerParams(dimension_semantics=sem, vmem_limit_bytes=VMEM_LIMIT_BYTES)


def _dot(a, b):
    return jnp.dot(a, b, preferred_element_type=F32)


def _dot_nt(a, b):
    return lax.dot_general(a, b, (((1,), (1,)), ((), ())), preferred_element_type=F32)


def _dot_exact(a, b):
    return jnp.dot(a, b, preferred_element_type=F32, precision=HIGHEST)


def _expand_exact(x, onehot3):
    x1 = x.astype(BF16)
    r1 = x - x1.astype(F32)
    x2 = r1.astype(BF16)
    x3 = (r1 - x2.astype(F32)).astype(BF16)
    return _dot(jnp.concatenate([x1, x2, x3], axis=1), onehot3)


def _sigmoid(x):
    return 1.0 / (1.0 + jnp.exp(-x))


def _shift_rows(x, prev8, s):
    xr = pltpu.roll(x, s, 0)
    pr = pltpu.roll(prev8, s, 0)
    rows = lax.broadcasted_iota(jnp.int32, (SUBLANES, x.shape[1]), 0)
    top = jnp.where(rows < s, pr, xr[:SUBLANES])
    return jnp.concatenate([top, xr[SUBLANES:]], axis=0)


def _causal_conv(x, prev8, w):
    k = w.shape[0]
    y = w[k - 1:k] * x
    for s in range(1, k):
        y = y + w[k - 1 - s:k - s] * _shift_rows(x, prev8, s)
    return y


def _rms_matmul_kernel(x_ref, g_ref, w_ref, o_ref, h_ref):
    @pl.when(pl.program_id(1) == 0)
    def _():
        x = x_ref[...]
        ms = jnp.mean(x * x, axis=-1, keepdims=True)
        h_ref[...] = (x * lax.rsqrt(ms + NORM_EPS) * g_ref[...]).astype(BF16)

    o_ref[...] = _dot_nt(h_ref[...], w_ref[...])


def _rms_matmul(x, g, w_t, layer, *, tm, tn):
    n, d = x.shape
    dout = w_t.shape[1]
    return pl.pallas_call(
        _rms_matmul_kernel,
        out_shape=jax.ShapeDtypeStruct((n, dout), F32),
        grid=(n // tm, dout // tn),
        in_specs=[
            pl.BlockSpec((tm, d), lambda i, j: (i, 0)),
            pl.BlockSpec((1, d), lambda i, j: (0, 0)),
            pl.BlockSpec((None, tn, d), lambda i, j: (layer, j, 0)),
        ],
        out_specs=pl.BlockSpec((tm, tn), lambda i, j: (i, j)),
        scratch_shapes=[pltpu.VMEM((tm, d), BF16)],
        compiler_params=_cparams("arbitrary", "arbitrary"),
        name="rms_in_proj",
    )(x, g, w_t)


def _ssd_kernel(xbc_ref, halo_ref, z_ref, dt_ref, cw_ref, cb_ref, dtb_ref, alog_ref,
                dx_ref, ng_ref, ex_ref, o_ref, h_ref, y_ref):
    c = pl.program_id(1)
    L = SSD_CHUNK

    @pl.when(c == 0)
    def _():
        h_ref[...] = jnp.zeros(h_ref.shape, F32)

    prev = jnp.where(c == 0, 0.0, halo_ref[...])
    xbc = _causal_conv(xbc_ref[...], prev, cw_ref[...]) + cb_ref[...]
    xbc = xbc * _sigmoid(xbc)
    xs = xbc[:, :D_SSD]

    row = lax.broadcasted_iota(jnp.int32, (L, L), 0)
    col = lax.broadcasted_iota(jnp.int32, (L, L), 1)
    tril = (row >= col).astype(F32)

    def softplus(v):
        return jnp.maximum(v, 0.0) + jnp.log1p(jnp.exp(-jnp.abs(v)))

    dt_s = softplus(dt_ref[...] + dtb_ref[...])
    a_s = dt_s * (-jnp.exp(alog_ref[...]))
    acs_s = _dot_exact(tril, a_s)
    acs_t = acs_s.T
    ex = ex_ref[...]
    dt_x = _expand_exact(dt_s, ex)
    e_x = _expand_exact(jnp.exp(acs_s), ex)
    de_x = _expand_exact(jnp.exp(acs_s[L - 1:L, :] - acs_s), ex)
    decay = e_x[L - 1:L, :]
    xdt = xs * dt_x
    xdt_end = (xdt * de_x).astype(BF16)
    xdt = xdt.astype(BF16)

    lane = lax.broadcasted_iota(jnp.int32, (L, LANES), 1)
    lo = lane < SSD_HEAD_DIM
    lower = row >= col

    for g in range(SSD_GROUPS):
        bm = xbc[:, D_SSD + g * SSD_STATE:D_SSD + (g + 1) * SSD_STATE].astype(BF16)
        cm = xbc[:, D_SSD + (SSD_GROUPS + g) * SSD_STATE:
                 D_SSD + (SSD_GROUPS + g + 1) * SSD_STATE].astype(BF16)
        cb = _dot_nt(cm, bm)
        bm_t = bm.T
        for q in range(4):
            pair = g * 4 + q
            sl = slice(pair * LANES, (pair + 1) * LANES)
            gs = []
            for hh in range(2):
                h = 2 * pair + hh
                seg = (jnp.broadcast_to(acs_s[:, h:h + 1], (L, L))
                       - jnp.broadcast_to(acs_t[h:h + 1, :], (L, L)))
                lm = jnp.where(lower, jnp.exp(seg), 0.0)
                gs.append((cb * lm).astype(BF16))
            gpair = jnp.concatenate(gs, axis=1)
            xp = xdt[:, sl]
            zero = jnp.zeros_like(xp)
            x2 = jnp.concatenate([jnp.where(lo, xp, zero), jnp.where(lo, zero, xp)], axis=0)
            y = _dot(gpair, x2)
            hprev = h_ref[pair]
            y = y + e_x[:, sl] * _dot(cm, hprev.astype(BF16))
            y = y + xs[:, sl] * dx_ref[:, sl]
            h_ref[pair] = hprev * decay[:, sl] + _dot(bm_t, xdt_end[:, sl])
            y_ref[:, sl] = y

    z = z_ref[...]
    y = y_ref[...] * (z * _sigmoid(z))
    half = D_SSD // SSD_GROUPS
    outs = []
    for g in range(SSD_GROUPS):
        yg = y[:, g * half:(g + 1) * half]
        ms = jnp.mean(yg * yg, axis=-1, keepdims=True)
        outs.append(yg * lax.rsqrt(ms + NORM_EPS))
    o_ref[...] = (jnp.concatenate(outs, axis=1) * ng_ref[...]).astype(BF16)


def _ssd(proj, conv_w, conv_b, dt_bias, a_log, d_skip, norm_g, *, bsz, seq):
    nc = seq // SSD_CHUNK
    n = bsz * seq
    pad = LANES - SSD_HEADS
    dtb = jnp.pad(dt_bias, (0, pad)).reshape(1, LANES)
    alog = jnp.pad(a_log, (0, pad), constant_values=NEG_BIG).reshape(1, LANES)
    rep = lambda v: jnp.repeat(v, SSD_HEAD_DIM).reshape(1, D_SSD)
    expand = (jnp.arange(LANES)[:, None] == (jnp.arange(D_SSD)[None, :] // SSD_HEAD_DIM)).astype(BF16)
    expand = jnp.concatenate([expand] * 3, axis=0)
    rows_per_halo = SSD_CHUNK // SUBLANES
    const = lambda shape: pl.BlockSpec(shape, lambda b, c: (0,) * len(shape))
    return pl.pallas_call(
        _ssd_kernel,
        out_shape=jax.ShapeDtypeStruct((n, D_SSD), BF16),
        grid=(bsz, nc),
        in_specs=[
            pl.BlockSpec((SSD_CHUNK, SSD_CONV_DIM), lambda b, c: (b * nc + c, COL_XBC // SSD_CONV_DIM)),
            pl.BlockSpec((SUBLANES, SSD_CONV_DIM),
                         lambda b, c: (jnp.maximum((b * nc + c) * rows_per_halo - 1, 0), 0)),
            pl.BlockSpec((SSD_CHUNK, D_SSD), lambda b, c: (b * nc + c, COL_Z // D_SSD)),
            pl.BlockSpec((SSD_CHUNK, LANES), lambda b, c: (b * nc + c, COL_DT // LANES)),
            const((SSD_CONV, SSD_CONV_DIM)),
            const((1, SSD_CONV_DIM)),
            const((1, LANES)),
            const((1, LANES)),
            const((1, D_SSD)),
            const((1, D_SSD)),
            const((3 * LANES, D_SSD)),
        ],
        out_specs=pl.BlockSpec((SSD_CHUNK, D_SSD), lambda b, c: (b * nc + c, 0)),
        scratch_shapes=[pltpu.VMEM((SSD_HEADS // 2, SSD_STATE, LANES), F32),
                        pltpu.VMEM((SSD_CHUNK, D_SSD), F32)],
        compiler_params=_cparams("arbitrary", "arbitrary"),
        name="ssd_scan",
    )(proj, proj, proj, proj, conv_w, conv_b.reshape(1, -1), dtb, alog,
      rep(d_skip), norm_g.reshape(1, -1), expand)


def _dsa_prep_kernel(q_ref, k_ref, v_ref, qi_ref, ki_ref, qg_ref, kg_ref, seg_ref,
                     qn_ref, kn_ref, vt_ref, qib_ref, kib_ref, *, blk):
    def head_norm(x, g):
        ms = _expand_exact(x * x, seg_ref[...]) * (1.0 / ATT_HEAD_DIM)
        return x * lax.rsqrt(ms + NORM_EPS) * g

    qn_ref[...] = (head_norm(q_ref[...], qg_ref[...]) * (LOG2E * ATT_HEAD_DIM ** -0.5)).astype(BF16)
    kn_ref[...] = head_norm(k_ref[...], kg_ref[...]).astype(BF16)
    for c in range(vt_ref.shape[0]):
        vt_ref[c] = v_ref[c * blk:(c + 1) * blk, :].T.astype(BF16)
    qib_ref[...] = qi_ref[...].astype(BF16)
    kib_ref[...] = ki_ref[...].astype(BF16)


def _dsa_prep(proj, q_norm, k_norm, *, tm, blk):
    n = proj.shape[0]
    seg = (jnp.arange(D_ATT)[:, None] // ATT_HEAD_DIM
           == jnp.arange(D_ATT)[None, :] // ATT_HEAD_DIM).astype(BF16)
    seg = jnp.concatenate([seg] * 3, axis=0)
    tile = lambda v: jnp.tile(v, ATT_HEADS).reshape(1, D_ATT)
    col_blk = lambda col, w: pl.BlockSpec((tm, w), lambda i: (i, col // w))
    const = lambda shape: pl.BlockSpec(shape, lambda i: (0,) * len(shape))
    return pl.pallas_call(
        functools.partial(_dsa_prep_kernel, blk=blk),
        out_shape=(
            jax.ShapeDtypeStruct((n, D_ATT), BF16),
            jax.ShapeDtypeStruct((n, D_ATT), BF16),
            jax.ShapeDtypeStruct((n // blk, D_ATT, blk), BF16),
            jax.ShapeDtypeStruct((n, D_ATT), BF16),
            jax.ShapeDtypeStruct((n, LANES), BF16),
        ),
        grid=(n // tm,),
        in_specs=[col_blk(COL_Q, D_ATT), col_blk(COL_K, D_ATT), col_blk(COL_V, D_ATT),
                  col_blk(COL_QIDX, D_ATT), col_blk(COL_KIDX, LANES),
                  const((1, D_ATT)), const((1, D_ATT)), const((3 * D_ATT, D_ATT))],
        out_specs=(
            pl.BlockSpec((tm, D_ATT), lambda i: (i, 0)),
            pl.BlockSpec((tm, D_ATT), lambda i: (i, 0)),
            pl.BlockSpec((tm // blk, D_ATT, blk), lambda i: (i, 0, 0)),
            pl.BlockSpec((tm, D_ATT), lambda i: (i, 0)),
            pl.BlockSpec((tm, LANES), lambda i: (i, 0)),
        ),
        compiler_params=_cparams("arbitrary"),
        name="dsa_prep",
    )(proj, proj, proj, proj, proj, tile(q_norm), tile(k_norm), seg)


def _dsa_kernel(qn_ref, qi_ref, w_ref, kn_ref, vt_ref, ki_ref, bias_ref, o_ref,
                key_ref, hi_ref, lo_ref, big_ref, qm_ref, qim_ref, wt_ref, mb_ref, carry_ref,
                m_ref, l_ref, acc_ref, *, blk, topk):
    i = pl.program_id(1)
    pack = 16
    lane = lax.broadcasted_iota(jnp.int32, (blk, LANES), 1)
    lo_lanes = lane < ATT_HEAD_DIM
    krow = lax.broadcasted_iota(jnp.int32, (blk, blk), 0)
    qcol = lax.broadcasted_iota(jnp.int32, (blk, blk), 1)
    future = krow > qcol

    for p in range(ATT_HEADS // 2):
        sl = slice(p * LANES, (p + 1) * LANES)
        qp = qn_ref[:, sl]
        qip = qi_ref[:, sl]
        zero = jnp.zeros_like(qp)
        qm_ref[2 * p * blk:(2 * p + 1) * blk] = jnp.where(lo_lanes, qp, zero)
        qm_ref[(2 * p + 1) * blk:(2 * p + 2) * blk] = jnp.where(lo_lanes, zero, qp)
        qim_ref[2 * p * blk:(2 * p + 1) * blk] = jnp.where(lo_lanes, qip, zero)
        qim_ref[(2 * p + 1) * blk:(2 * p + 2) * blk] = jnp.where(lo_lanes, zero, qip)
    wt_ref[...] = w_ref[...].T

    def score_dots(j, slot):
        kk = ki_ref[pl.ds(pl.multiple_of(j * blk, blk), blk), :]
        big_ref[slot] = _dot_nt(kk, qim_ref[...])

    def score_finish(j, slot, diag):
        s = jnp.zeros((blk, blk), F32)
        for h in range(IDX_HEADS):
            s = s + jnp.maximum(big_ref[slot, :, h * blk:(h + 1) * blk], 0.0) * wt_ref[h:h + 1, :]
        if diag:
            s = jnp.where(future, -jnp.inf, s)
        s = jnp.where(s == 0.0, 0.0, s)
        bits = pltpu.bitcast(s, jnp.int32)
        key = bits ^ ((bits >> 31) & 0x7FFFFFFF)
        key_ref[j] = key
        hi_ref[j] = (key >> 16).astype(jnp.int16)
        lo_ref[j] = ((key & 0xFFFF) - 32768).astype(jnp.int16)

    def by_parity(x, fn):
        @pl.when(x % 2 == 0)
        def _():
            fn(0)

        @pl.when(x % 2 == 1)
        def _():
            fn(1)

    score_dots(0, 0)

    def score_pair(t, carry):
        score_dots(2 * t + 1, 1)
        score_finish(2 * t, 0, False)
        score_dots(2 * t + 2, 0)
        score_finish(2 * t + 1, 1, False)
        return carry

    lax.fori_loop(0, i // 2, score_pair, 0)

    @pl.when(i % 2 == 1)
    def _():
        score_dots(i, 1)
        score_finish(i - 1, 0, False)

    by_parity(i, lambda slot: score_finish(i, slot, True))

    one16 = jnp.ones((), jnp.int16)
    zero16 = jnp.zeros((), jnp.int16)

    def count_ge(src_ref, cand):
        cand_b = jnp.broadcast_to(cand.astype(jnp.int16), (blk, blk))

        def add_tile(j, cnts, on):
            ind = jnp.where(src_ref[j] >= cand_b, on, zero16)
            cnts = list(cnts)
            for g in range(blk // pack):
                cnts[g % len(cnts)] = cnts[g % len(cnts)] + ind[g * pack:(g + 1) * pack]
            return tuple(cnts)

        def body(t, cnts):
            return add_tile(2 * t + 1, add_tile(2 * t, cnts, one16), one16)

        zeros = tuple(jnp.zeros((pack, blk), jnp.int16) for _ in range(4))
        cnts = lax.fori_loop(0, (i + 1) // 2, body, zeros)
        cnts = add_tile(i, cnts, ((i + 1) % 2).astype(jnp.int16))
        cnt = (cnts[0] + cnts[1]) + (cnts[2] + cnts[3])
        return jnp.sum(cnt.astype(jnp.int32).astype(F32), axis=0, keepdims=True)

    def bisect16(src_ref, rank, cnt_all):
        def bit_body(b, st):
            prefix, c_acc, c_rej = st
            cand = prefix + lax.shift_left(jnp.int32(1), 15 - b)
            cnt = count_ge(src_ref, cand)
            ok = cnt >= rank
            return (jnp.where(ok, cand, prefix), jnp.where(ok, cnt, c_acc), jnp.where(ok, c_rej, cnt))
        init = (jnp.full((1, blk), -32768, jnp.int32), cnt_all, jnp.zeros((1, blk), F32))
        return lax.fori_loop(0, 16, bit_body, init)

    ncols = jnp.full((1, blk), ((i + 1) * blk).astype(F32), F32)
    p_hi, c_acc1, c_rej1 = bisect16(hi_ref, jnp.float32(topk), ncols)
    p_hi_b = jnp.broadcast_to(p_hi.astype(jnp.int16), (blk, blk))

    def group_body(j, carry):
        lo_ref[j] = jnp.where(hi_ref[j] == p_hi_b, lo_ref[j], jnp.int16(-32768))
        return carry

    lax.fori_loop(0, i + 1, group_body, 0)
    rank2 = jnp.float32(topk) - c_rej1
    p_lo, c_acc2, c_rej2 = bisect16(lo_ref, rank2, c_acc1 - c_rej1)
    thr = lax.shift_left(p_hi, 16) + (p_lo + 32768)
    need = rank2 - c_rej2
    has_ties = jnp.max((c_acc2 - c_rej2) - need) > 0.5

    m_ref[...] = jnp.full(m_ref.shape, NEG_BIG, F32)
    l_ref[...] = jnp.zeros(l_ref.shape, F32)
    acc_ref[...] = jnp.zeros(acc_ref.shape, F32)
    carry_ref[...] = jnp.zeros(carry_ref.shape, F32)

    def qk_dots(j, slot):
        start = pl.multiple_of(j * blk, blk)
        for p in range(ATT_HEADS // 2):
            mb2 = jnp.concatenate([mb_ref[slot], mb_ref[slot]], axis=1)
            big_ref[slot, :, 2 * p * blk:(2 * p + 2) * blk] = _dot_nt(
                kn_ref[pl.ds(start, blk), p * LANES:(p + 1) * LANES], qm_ref[2 * p * blk:(2 * p + 2) * blk]) + mb2

    def tile_mask(j, kind, slot):
        kt = key_ref[j]

        @pl.when(jnp.logical_not(has_ties))
        def _():
            mb = jnp.where(kt >= thr, 0.0, NEG_BIG)
            mb_ref[slot] = jnp.where(future, NEG_BIG, mb) if kind == 2 else mb

        @pl.when(has_ties)
        def _():
            eq = jnp.where(kt == thr, 1.0, 0.0)
            lower = jnp.where(krow >= qcol, 1.0, 0.0).astype(BF16)
            seen = _dot(lower, eq.astype(BF16)) + carry_ref[0:1, :]
            keep = jnp.where(seen <= need, 0.0, NEG_BIG)
            mb = jnp.where(kt > thr, 0.0, jnp.where(kt == thr, keep, NEG_BIG))
            mb_ref[slot] = jnp.where(future, NEG_BIG, mb) if kind == 2 else mb
            carry_ref[0:1, :] += jnp.sum(eq, axis=0, keepdims=True)

    ones_rows = jnp.ones((pack, blk), BF16)

    def softmax_pv(j, slot, kind):
        vt = vt_ref[j]
        for h in range(ATT_HEADS):
            half = blk // 2
            s = big_ref[slot, :, h * blk:(h + 1) * blk]
            if kind > 0:
                s = s + bias_ref[kind - 1, h]
            m_prev = m_ref[h]
            m_new = jnp.maximum(m_prev, jnp.max(s, axis=0, keepdims=True))
            alpha = jnp.exp2(m_prev - m_new)
            if kind > 0:
                pexp = jnp.exp2(s - m_new[0:1, :]).astype(BF16)
            else:
                pexp = jnp.concatenate(
                    [jnp.exp2(big_ref[slot, r * half:(r + 1) * half, h * blk:(h + 1) * blk]
                              - m_new[0:1, :]).astype(BF16) for r in range(2)], axis=0)
            m_ref[h] = m_new
            ch = slice(h * ATT_HEAD_DIM, (h + 1) * ATT_HEAD_DIM)
            pv = _dot(jnp.concatenate([vt[ch, :], ones_rows], axis=0), pexp)
            l_ref[h] = alpha * l_ref[h] + pv[ATT_HEAD_DIM:ATT_HEAD_DIM + 1, :]
            acc_ref[ch, :] = alpha[0:1, :] * acc_ref[ch, :] + pv[:ATT_HEAD_DIM, :]

    n_far = i - 1

    @pl.when(i == 0)
    def _():
        tile_mask(0, 2, 0)

    @pl.when(i > 0)
    def _():
        tile_mask(0, 0, 0)

    qk_dots(0, 0)

    def attn_pair(t, carry):
        tile_mask(2 * t + 1, 0, 1)
        tile_mask(2 * t + 2, 0, 0)
        qk_dots(2 * t + 1, 1)
        softmax_pv(2 * t, 0, 0)
        qk_dots(2 * t + 2, 0)
        softmax_pv(2 * t + 1, 1, 0)
        return carry

    lax.fori_loop(0, n_far // 2, attn_pair, 0)

    @pl.when(jnp.logical_and(n_far > 0, n_far % 2 == 1))
    def _():
        tile_mask(n_far, 0, 1)
        qk_dots(n_far, 1)
        softmax_pv(n_far - 1, 0, 0)

    def near_tiles(slot):
        tile_mask(i, 2, 1 - slot)
        qk_dots(i, 1 - slot)
        softmax_pv(i - 1, slot, 1)
        softmax_pv(i, 1 - slot, 2)

    @pl.when(i > 0)
    def _():
        by_parity(i - 1, near_tiles)

    @pl.when(i == 0)
    def _():
        softmax_pv(0, 0, 2)

    outs = []
    for h in range(ATT_HEADS):
        ch = slice(h * ATT_HEAD_DIM, (h + 1) * ATT_HEAD_DIM)
        outs.append(acc_ref[ch, :] / l_ref[h][0:1, :])
    o_ref[...] = jnp.concatenate(outs, axis=0).T.astype(BF16)


def _t5_bucket(dist):
    n = jnp.maximum(dist, 0)
    max_exact = REL_BUCKETS // 2
    large = max_exact + (jnp.log(jnp.maximum(n, max_exact).astype(F32) / max_exact)
                         / math.log(REL_MAX_DIST / max_exact)
                         * (REL_BUCKETS - max_exact)).astype(jnp.int32)
    large = jnp.minimum(large, REL_BUCKETS - 1)
    return jnp.where(n < max_exact, n, large)


def _bias_tiles(rel_bias, blk):
    width = 2 * blk
    f = ((rel_bias[_t5_bucket(jnp.arange(width, dtype=jnp.int32))] - rel_bias[REL_BUCKETS - 1]) * LOG2E).T
    h_sub = jnp.concatenate([f[:, blk:], f[:, :blk]], axis=1)
    h = jnp.stack([h_sub, f])
    skew = jnp.tile(h, (1, 1, blk))[:, :, :blk * (width - 1)].reshape(2, ATT_HEADS, blk, width - 1)
    return skew[:, :, :, :blk]


def _dsa(qn, qi, proj, kn, vt, ki, bias, *, bsz, seq, blk):
    n = bsz * seq
    nb = seq // blk
    topk = min(TOPK_MAX, seq // 4)
    once = pl.Buffered(1)
    return pl.pallas_call(
        functools.partial(_dsa_kernel, blk=blk, topk=topk),
        out_shape=jax.ShapeDtypeStruct((n, D_ATT), BF16),
        grid=(bsz, nb),
        in_specs=[
            pl.BlockSpec((blk, D_ATT), lambda b, i: (b * nb + i, 0)),
            pl.BlockSpec((blk, D_ATT), lambda b, i: (b * nb + i, 0)),
            pl.BlockSpec((blk, LANES), lambda b, i: (b * nb + i, COL_WIDX // LANES)),
            pl.BlockSpec((seq, D_ATT), lambda b, i: (b, 0), pipeline_mode=once),
            pl.BlockSpec((nb, D_ATT, blk), lambda b, i: (b, 0, 0), pipeline_mode=once),
            pl.BlockSpec((seq, LANES), lambda b, i: (b, 0), pipeline_mode=once),
            pl.BlockSpec((2, ATT_HEADS, blk, blk), lambda b, i: (0, 0, 0, 0), pipeline_mode=once),
        ],
        out_specs=pl.BlockSpec((blk, D_ATT), lambda b, i: (b * nb + i, 0)),
        scratch_shapes=[
            pltpu.VMEM((nb, blk, blk), jnp.int32),
            pltpu.VMEM((nb, blk, blk), jnp.int16),
            pltpu.VMEM((nb, blk, blk), jnp.int16),
            pltpu.VMEM((2, blk, ATT_HEADS * blk), F32),
            pltpu.VMEM((ATT_HEADS * blk, LANES), BF16),
            pltpu.VMEM((IDX_HEADS * blk, LANES), BF16),
            pltpu.VMEM((LANES, blk), F32),
            pltpu.VMEM((2, blk, blk), F32),
            pltpu.VMEM((SUBLANES, blk), F32),
            pltpu.VMEM((ATT_HEADS, SUBLANES, blk), F32),
            pltpu.VMEM((ATT_HEADS, SUBLANES, blk), F32),
            pltpu.VMEM((D_ATT, blk), F32),
        ],
        compiler_params=_cparams("arbitrary", "arbitrary"),
        name="dsa_attention",
    )(qn, qi, proj, kn, vt, ki, bias)


def _outproj_kernel(x_ref, yssd_ref, yatt_ref, scb_ref, scc_ref, sch_ref, hc_ref, hh_ref,
                    cw_ref, w_ref, o_ref, cat_ref, *, blocks_per_seq):
    i = pl.program_id(0)

    @pl.when(pl.program_id(1) == 0)
    def _():
        prev = jnp.where(i % blocks_per_seq == 0, 0.0, hc_ref[...] * hh_ref[...])
        ysc = scb_ref[...] * _causal_conv(scc_ref[...] * sch_ref[...], prev, cw_ref[...])
        cat_ref[:, :D_SSD] = yssd_ref[...]
        cat_ref[:, D_SSD:D_SSD + D_ATT] = yatt_ref[...]
        cat_ref[:, D_SSD + D_ATT:] = ysc.astype(BF16)

    o_ref[...] = x_ref[...] + _dot(cat_ref[...], w_ref[...])


def _outproj(x, y_ssd, y_att, proj, sc_w, w_out, layer, *, seq, tm, tn):
    n, d = x.shape
    halo = tm // SUBLANES
    blk = lambda col: pl.BlockSpec((tm, D_SC), lambda i, j: (i, col // D_SC))
    hblk = lambda col: pl.BlockSpec(
        (SUBLANES, D_SC), lambda i, j: (jnp.maximum(i * halo - 1, 0), col // D_SC))
    return pl.pallas_call(
        functools.partial(_outproj_kernel, blocks_per_seq=seq // tm),
        out_shape=jax.ShapeDtypeStruct((n, d), F32),
        grid=(n // tm, d // tn),
        in_specs=[
            pl.BlockSpec((tm, tn), lambda i, j: (i, j)),
            pl.BlockSpec((tm, D_SSD), lambda i, j: (i, 0)),
            pl.BlockSpec((tm, D_ATT), lambda i, j: (i, 0)),
            blk(COL_SCB), blk(COL_SCC), blk(COL_SCH), hblk(COL_SCC), hblk(COL_SCH),
            pl.BlockSpec((SC_CONV, D_SC), lambda i, j: (0, 0)),
            pl.BlockSpec((None, d, tn), lambda i, j: (layer, 0, j)),
        ],
        out_specs=pl.BlockSpec((tm, tn), lambda i, j: (i, j)),
        scratch_shapes=[pltpu.VMEM((tm, d), BF16)],
        compiler_params=_cparams("arbitrary", "arbitrary"),
        name="out_proj",
    )(x, y_ssd, y_att, proj, proj, proj, proj, proj, sc_w, w_out)


def _ffn_kernel(x_ref, g_ref, wg_ref, wu_ref, cw_ref, wd_ref, o_ref, h_ref, acc_ref, halo_ref,
                *, blocks_per_seq):
    i = pl.program_id(0)
    j = pl.program_id(1)
    tm = x_ref.shape[0]

    @pl.when(j == 0)
    def _():
        x = x_ref[...]
        ms = jnp.mean(x * x, axis=-1, keepdims=True)
        h_ref[...] = (x * lax.rsqrt(ms + NORM_EPS) * g_ref[...]).astype(BF16)
        acc_ref[...] = jnp.zeros(acc_ref.shape, F32)

    @pl.when(i % blocks_per_seq == 0)
    def _():
        halo_ref[j] = jnp.zeros(halo_ref.shape[1:], F32)

    h = h_ref[...]
    gate = _dot(h, wg_ref[...])
    up = _dot(h, wu_ref[...])
    prev = halo_ref[j]
    halo_ref[j] = gate[tm - SUBLANES:, :]
    gc = _causal_conv(gate, prev, cw_ref[...])
    act = (gc * _sigmoid(gc) * up).astype(BF16)
    acc_ref[...] += _dot(act, wd_ref[...])

    @pl.when(j == pl.num_programs(1) - 1)
    def _():
        o_ref[...] = x_ref[...] + acc_ref[...]


def _ffn(x, g, wg, wu, cw, wd, layer, *, seq, tm, tf):
    n, d = x.shape
    dff = wg.shape[2]
    return pl.pallas_call(
        functools.partial(_ffn_kernel, blocks_per_seq=seq // tm),
        out_shape=jax.ShapeDtypeStruct((n, d), F32),
        grid=(n // tm, dff // tf),
        in_specs=[
            pl.BlockSpec((tm, d), lambda i, j: (i, 0)),
            pl.BlockSpec((1, d), lambda i, j: (0, 0)),
            pl.BlockSpec((None, d, tf), lambda i, j: (layer, 0, j)),
            pl.BlockSpec((None, d, tf), lambda i, j: (layer, 0, j)),
            pl.BlockSpec((FFN_CONV, tf), lambda i, j: (0, j)),
            pl.BlockSpec((None, tf, d), lambda i, j: (layer, j, 0)),
        ],
        out_specs=pl.BlockSpec((tm, d), lambda i, j: (i, 0)),
        scratch_shapes=[
            pltpu.VMEM((tm, d), BF16),
            pltpu.VMEM((tm, d), F32),
            pltpu.VMEM((dff // tf, SUBLANES, tf), F32),
        ],
        compiler_params=_cparams("arbitrary", "arbitrary"),
        name="conv_gated_mlp",
    )(x, g, wg, wu, cw, wd)


_IN_SIZES = (D_SSD, SSD_CONV_DIM, SSD_HEADS, D_ATT, D_ATT, D_ATT,
             IDX_HEADS * IDX_DIM, IDX_DIM, IDX_HEADS, D_SC, D_SC, D_SC)
_IN_OFFS = tuple(sum(_IN_SIZES[:t]) for t in range(len(_IN_SIZES) + 1))
D_IN_PROJ = _IN_OFFS[-1]
_WIDE_SEGMENTS = ((COL_XBC, 1), (COL_Q, 3), (COL_Z, 0), (COL_K, 4), (COL_V, 5), (COL_QIDX, 6),
                  (COL_SCB, 9), (COL_SCC, 10), (COL_SCH, 11))


def _pack_w_in_kernel(w_ref, o_ref):
    tc = o_ref.shape[1]
    pack = 16

    def rows(seg):
        return w_ref[_IN_OFFS[seg]:_IN_OFFS[seg + 1], :]

    def put(dst, val, total):
        n = val.shape[0]
        pad = (-n) % pack
        if pad:
            val = jnp.concatenate([val, jnp.zeros((pad, tc), F32)], axis=0)
        o_ref[dst:dst + n + pad, :] = val.astype(BF16)
        if total > n + pad:
            o_ref[dst + n + pad:dst + total, :] = jnp.zeros((total - n - pad, tc), BF16)

    for dst, seg in _WIDE_SEGMENTS:
        put(dst, rows(seg), _IN_SIZES[seg])
    put(COL_DT, rows(2), LANES)
    put(COL_KIDX, jnp.concatenate([rows(7)] * (LANES // IDX_DIM), axis=0), LANES)
    put(COL_WIDX, rows(8), D_PACKED - COL_WIDX)


def _pack_w_in(w_in, *, tc=256):
    depth, d, _ = w_in.shape
    w_t = jnp.swapaxes(w_in, 1, 2)
    return pl.pallas_call(
        _pack_w_in_kernel,
        out_shape=jax.ShapeDtypeStruct((depth, D_PACKED, d), BF16),
        grid=(depth, d // tc),
        in_specs=[pl.BlockSpec((None, D_IN_PROJ, tc), lambda l, i: (l, 0, i))],
        out_specs=pl.BlockSpec((None, D_PACKED, tc), lambda l, i: (l, 0, i)),
        compiler_params=_cparams("arbitrary", "arbitrary"),
        name="pack_w_in",
    )(w_t)


def _forward(x, norm_mix, w_in, ssd_conv_w, ssd_conv_b, ssd_dt_bias, ssd_a_log, ssd_d, ssd_norm,
             att_q_norm, att_k_norm, rel_bias, sc_conv_w, w_out, norm_ffn,
             ffn_w_gate, ffn_w_up, ffn_conv_w, ffn_w_down, *, tm_proj, tn_proj, tm_out, tn_out,
             tm_ffn, tf_ffn, tm_prep, att_blk):
    bsz, seq, d = x.shape
    depth = w_in.shape[0]
    n = bsz * seq
    xf = x.reshape(n, d)
    w_in_p = _pack_w_in(w_in)
    w_out_b = w_out.astype(BF16)
    wg_b = ffn_w_gate.astype(BF16)
    wu_b = ffn_w_up.astype(BF16)
    wd_b = ffn_w_down.astype(BF16)
    bias = _bias_tiles(rel_bias, att_blk)
    for l in range(depth):
        proj = _rms_matmul(xf, norm_mix[l].reshape(1, d), w_in_p, l, tm=tm_proj, tn=tn_proj)
        y_ssd = _ssd(proj, ssd_conv_w[l], ssd_conv_b[l], ssd_dt_bias[l], ssd_a_log[l], ssd_d[l],
                     ssd_norm[l], bsz=bsz, seq=seq)
        qn, kn, vt, qi, ki = _dsa_prep(proj, att_q_norm[l], att_k_norm[l], tm=tm_prep, blk=att_blk)
        y_att = _dsa(qn, qi, proj, kn, vt, ki, bias, bsz=bsz, seq=seq, blk=att_blk)
        xf = _outproj(xf, y_ssd, y_att, proj, sc_conv_w[l], w_out_b, l, seq=seq, tm=tm_out, tn=tn_out)
        xf = _ffn(xf, norm_ffn[l].reshape(1, d), wg_b, wu_b, ffn_conv_w[l], wd_b, l,
                  seq=seq, tm=tm_ffn, tf=tf_ffn)
    return xf.reshape(bsz, seq, d)


def kernel(x, norm_mix, w_in, ssd_conv_w, ssd_conv_b, ssd_dt_bias, ssd_a_log, ssd_d, ssd_norm, att_q_norm, att_k_norm, rel_bias, sc_conv_w, w_out, norm_ffn, ffn_w_gate, ffn_w_up, ffn_conv_w, ffn_w_down):
    return _forward(x, norm_mix, w_in, ssd_conv_w, ssd_conv_b, ssd_dt_bias, ssd_a_log, ssd_d, ssd_norm,
                    att_q_norm, att_k_norm, rel_bias, sc_conv_w, w_out, norm_ffn,
                    ffn_w_gate, ffn_w_up, ffn_conv_w, ffn_w_down,
                    tm_proj=1024, tn_proj=512, tm_out=512, tn_out=2048,
                    tm_ffn=512, tf_ffn=512, tm_prep=1024, att_blk=256)
```

```python
import functools
import math

import jax
import jax.numpy as jnp
from jax import lax
from jax.experimental import pallas as pl
from jax.experimental.pallas import tpu as pltpu

F32 = jnp.float32
BF16 = jnp.bfloat16
HIGHEST = lax.Precision.HIGHEST

LANES = 128
SUBLANES = 8
VMEM_LIMIT_BYTES = 56 * 1024 * 1024

D_MODEL = 2048
D_SSD = 1024
D_ATT = 512
D_SC = 512
SSD_HEAD_DIM = 64
SSD_HEADS = 16
SSD_GROUPS = 2
SSD_STATE = 128
SSD_CONV = 4
SSD_CHUNK = 128
SSD_CONV_DIM = D_SSD + 2 * SSD_GROUPS * SSD_STATE
ATT_HEAD_DIM = 64
ATT_HEADS = 8
IDX_HEADS = 8
IDX_DIM = 64
TOPK_MAX = 256
REL_BUCKETS = 32
REL_MAX_DIST = 128
SC_CONV = 3
D_FF = 5632
FFN_CONV = 3
NORM_EPS = 1e-6

COL_XBC = 0
COL_Q = 1536
COL_Z = 2048
COL_K = 3072
COL_V = 3584
COL_QIDX = 4096
COL_SCB = 4608
COL_SCC = 5120
COL_SCH = 5632
COL_DT = 6144
COL_KIDX = 6272
COL_WIDX = 6400
D_PACKED = 6656

NEG_BIG = -1e30
LOG2E = math.log2(math.e)


def _cparams(*sem):
    return pltpu.CompilerParams(dimension_semantics=sem, vmem_limit_bytes=VMEM_LIMIT_BYTES)


def _dot(a, b):
    return jnp.dot(a, b, preferred_element_type=F32)


def _dot_nt(a, b):
    return lax.dot_general(a, b, (((1,), (1,)), ((), ())), preferred_element_type=F32)


def _dot_exact(a, b):
    return jnp.dot(a, b, preferred_element_type=F32, precision=HIGHEST)


def _expand_exact(x, onehot3):
    x1 = x.astype(BF16)
    r1 = x - x1.astype(F32)
    x2 = r1.astype(BF16)
    x3 = (r1 - x2.astype(F32)).astype(BF16)
    return _dot(jnp.concatenate([x1, x2, x3], axis=1), onehot3)


def _sigmoid(x):
    return 1.0 / (1.0 + jnp.exp(-x))


def _shift_rows(x, prev8, s):
    xr = pltpu.roll(x, s, 0)
    pr = pltpu.roll(prev8, s, 0)
    rows = lax.broadcasted_iota(jnp.int32, (SUBLANES, x.shape[1]), 0)
    top = jnp.where(rows < s, pr, xr[:SUBLANES])
    return jnp.concatenate([top, xr[SUBLANES:]], axis=0)


def _causal_conv(x, prev8, w):
    k = w.shape[0]
    y = w[k - 1:k] * x
    for s in range(1, k):
        y = y + w[k - 1 - s:k - s] * _shift_rows(x, prev8, s)
    return y


def _rms_matmul_kernel(x_ref, g_ref, w_ref, o_ref, h_ref):
    @pl.when(pl.program_id(1) == 0)
    def _():
        x = x_ref[...]
        ms = jnp.mean(x * x, axis=-1, keepdims=True)
        h_ref[...] = (x * lax.rsqrt(ms + NORM_EPS) * g_ref[...]).astype(BF16)

    o_ref[...] = _dot_nt(h_ref[...], w_ref[...])


def _rms_matmul(x, g, w_t, layer, *, tm, tn):
    n, d = x.shape
    dout = w_t.shape[1]
    return pl.pallas_call(
        _rms_matmul_kernel,
        out_shape=jax.ShapeDtypeStruct((n, dout), F32),
        grid=(n // tm, dout // tn),
        in_specs=[
            pl.BlockSpec((tm, d), lambda i, j: (i, 0)),
            pl.BlockSpec((1, d), lambda i, j: (0, 0)),
            pl.BlockSpec((None, tn, d), lambda i, j: (layer, j, 0)),
        ],
        out_specs=pl.BlockSpec((tm, tn), lambda i, j: (i, j)),
        scratch_shapes=[pltpu.VMEM((tm, d), BF16)],
        compiler_params=_cparams("arbitrary", "arbitrary"),
        name="rms_in_proj",
    )(x, g, w_t)


def _ssd_kernel(xbc_ref, halo_ref, z_ref, dt_ref, cw_ref, cb_ref, dtb_ref, alog_ref,
                dx_ref, ng_ref, ex_ref, o_ref, h_ref, y_ref):
    c = pl.program_id(1)
    L = SSD_CHUNK

    @pl.when(c == 0)
    def _():
        h_ref[...] = jnp.zeros(h_ref.shape, F32)

    prev = jnp.where(c == 0, 0.0, halo_ref[...])
    xbc = _causal_conv(xbc_ref[...], prev, cw_ref[...]) + cb_ref[...]
    xbc = xbc * _sigmoid(xbc)
    xs = xbc[:, :D_SSD]

    row = lax.broadcasted_iota(jnp.int32, (L, L), 0)
    col = lax.broadcasted_iota(jnp.int32, (L, L), 1)
    tril = (row >= col).astype(F32)

    def softplus(v):
        return jnp.maximum(v, 0.0) + jnp.log1p(jnp.exp(-jnp.abs(v)))

    dt_s = softplus(dt_ref[...] + dtb_ref[...])
    a_s = dt_s * (-jnp.exp(alog_ref[...]))
    acs_s = _dot_exact(tril, a_s)
    acs_t = acs_s.T
    ex = ex_ref[...]
    dt_x = _expand_exact(dt_s, ex)
    e_x = _expand_exact(jnp.exp(acs_s), ex)
    de_x = _expand_exact(jnp.exp(acs_s[L - 1:L, :] - acs_s), ex)
    decay = e_x[L - 1:L, :]
    xdt = xs * dt_x
    xdt_end = (xdt * de_x).astype(BF16)
    xdt = xdt.astype(BF16)

    lane = lax.broadcasted_iota(jnp.int32, (L, LANES), 1)
    lo = lane < SSD_HEAD_DIM
    lower = row >= col

    for g in range(SSD_GROUPS):
        bm = xbc[:, D_SSD + g * SSD_STATE:D_SSD + (g + 1) * SSD_STATE].astype(BF16)
        cm = xbc[:, D_SSD + (SSD_GROUPS + g) * SSD_STATE:
                 D_SSD + (SSD_GROUPS + g + 1) * SSD_STATE].astype(BF16)
        cb = _dot_nt(cm, bm)
        bm_t = bm.T
        for q in range(4):
            pair = g * 4 + q
            sl = slice(pair * LANES, (pair + 1) * LANES)
            gs = []
            for hh in range(2):
                h = 2 * pair + hh
                seg = (jnp.broadcast_to(acs_s[:, h:h + 1], (L, L))
                       - jnp.broadcast_to(acs_t[h:h + 1, :], (L, L)))
                lm = jnp.where(lower, jnp.exp(seg), 0.0)
                gs.append((cb * lm).astype(BF16))
            gpair = jnp.concatenate(gs, axis=1)
            xp = xdt[:, sl]
            zero = jnp.zeros_like(xp)
            x2 = jnp.concatenate([jnp.where(lo, xp, zero), jnp.where(lo, zero, xp)], axis=0)
            y = _dot(gpair, x2)
            hprev = h_ref[pair]
            y = y + e_x[:, sl] * _dot(cm, hprev.astype(BF16))
            y = y + xs[:, sl] * dx_ref[:, sl]
            h_ref[pair] = hprev * decay[:, sl] + _dot(bm_t, xdt_end[:, sl])
            y_ref[:, sl] = y

    z = z_ref[...]
    y = y_ref[...] * (z * _sigmoid(z))
    half = D_SSD // SSD_GROUPS
    outs = []
    for g in range(SSD_GROUPS):
        yg = y[:, g * half:(g + 1) * half]
        ms = jnp.mean(yg * yg, axis=-1, keepdims=True)
        outs.append(yg * lax.rsqrt(ms + NORM_EPS))
    o_ref[...] = (jnp.concatenate(outs, axis=1) * ng_ref[...]).astype(BF16)


def _ssd(proj, conv_w, conv_b, dt_bias, a_log, d_skip, norm_g, *, bsz, seq):
    nc = seq // SSD_CHUNK
    n = bsz * seq
    pad = LANES - SSD_HEADS
    dtb = jnp.pad(dt_bias, (0, pad)).reshape(1, LANES)
    alog = jnp.pad(a_log, (0, pad), constant_values=NEG_BIG).reshape(1, LANES)
    rep = lambda v: jnp.repeat(v, SSD_HEAD_DIM).reshape(1, D_SSD)
    expand = (jnp.arange(LANES)[:, None] == (jnp.arange(D_SSD)[None, :] // SSD_HEAD_DIM)).astype(BF16)
    expand = jnp.concatenate([expand] * 3, axis=0)
    rows_per_halo = SSD_CHUNK // SUBLANES
    const = lambda shape: pl.BlockSpec(shape, lambda b, c: (0,) * len(shape))
    return pl.pallas_call(
        _ssd_kernel,
        out_shape=jax.ShapeDtypeStruct((n, D_SSD), BF16),
        grid=(bsz, nc),
        in_specs=[
            pl.BlockSpec((SSD_CHUNK, SSD_CONV_DIM), lambda b, c: (b * nc + c, COL_XBC // SSD_CONV_DIM)),
            pl.BlockSpec((SUBLANES, SSD_CONV_DIM),
                         lambda b, c: (jnp.maximum((b * nc + c) * rows_per_halo - 1, 0), 0)),
            pl.BlockSpec((SSD_CHUNK, D_SSD), lambda b, c: (b * nc + c, COL_Z // D_SSD)),
            pl.BlockSpec((SSD_CHUNK, LANES), lambda b, c: (b * nc + c, COL_DT // LANES)),
            const((SSD_CONV, SSD_CONV_DIM)),
            const((1, SSD_CONV_DIM)),
            const((1, LANES)),
            const((1, LANES)),
            const((1, D_SSD)),
            const((1, D_SSD)),
            const((3 * LANES, D_SSD)),
        ],
        out_specs=pl.BlockSpec((SSD_CHUNK, D_SSD), lambda b, c: (b * nc + c, 0)),
        scratch_shapes=[pltpu.VMEM((SSD_HEADS // 2, SSD_STATE, LANES), F32),
                        pltpu.VMEM((SSD_CHUNK, D_SSD), F32)],
        compiler_params=_cparams("arbitrary", "arbitrary"),
        name="ssd_scan",
    )(proj, proj, proj, proj, conv_w, conv_b.reshape(1, -1), dtb, alog,
      rep(d_skip), norm_g.reshape(1, -1), expand)


def _dsa_prep_kernel(q_ref, k_ref, v_ref, qi_ref, ki_ref, qg_ref, kg_ref, seg_ref,
                     qn_ref, kn_ref, vt_ref, qib_ref, kib_ref, *, blk):
    def head_norm(x, g):
        ms = _expand_exact(x * x, seg_ref[...]) * (1.0 / ATT_HEAD_DIM)
        return x * lax.rsqrt(ms + NORM_EPS) * g

    qn_ref[...] = (head_norm(q_ref[...], qg_ref[...]) * (LOG2E * ATT_HEAD_DIM ** -0.5)).astype(BF16)
    kn_ref[...] = head_norm(k_ref[...], kg_ref[...]).astype(BF16)
    for c in range(vt_ref.shape[0]):
        vt_ref[c] = v_ref[c * blk:(c + 1) * blk, :].T.astype(BF16)
    qib_ref[...] = qi_ref[...].astype(BF16)
    kib_ref[...] = ki_ref[...].astype(BF16)


def _dsa_prep(proj, q_norm, k_norm, *, tm, blk):
    n = proj.shape[0]
    seg = (jnp.arange(D_ATT)[:, None] // ATT_HEAD_DIM
           == jnp.arange(D_ATT)[None, :] // ATT_HEAD_DIM).astype(BF16)
    seg = jnp.concatenate([seg] * 3, axis=0)
    tile = lambda v: jnp.tile(v, ATT_HEADS).reshape(1, D_ATT)
    col_blk = lambda col, w: pl.BlockSpec((tm, w), lambda i: (i, col // w))
    const = lambda shape: pl.BlockSpec(shape, lambda i: (0,) * len(shape))
    return pl.pallas_call(
        functools.partial(_dsa_prep_kernel, blk=blk),
        out_shape=(
            jax.ShapeDtypeStruct((n, D_ATT), BF16),
            jax.ShapeDtypeStruct((n, D_ATT), BF16),
            jax.ShapeDtypeStruct((n // blk, D_ATT, blk), BF16),
            jax.ShapeDtypeStruct((n, D_ATT), BF16),
            jax.ShapeDtypeStruct((n, LANES), BF16),
        ),
        grid=(n // tm,),
        in_specs=[col_blk(COL_Q, D_ATT), col_blk(COL_K, D_ATT), col_blk(COL_V, D_ATT),
                  col_blk(COL_QIDX, D_ATT), col_blk(COL_KIDX, LANES),
                  const((1, D_ATT)), const((1, D_ATT)), const((3 * D_ATT, D_ATT))],
        out_specs=(
            pl.BlockSpec((tm, D_ATT), lambda i: (i, 0)),
            pl.BlockSpec((tm, D_ATT), lambda i: (i, 0)),
            pl.BlockSpec((tm // blk, D_ATT, blk), lambda i: (i, 0, 0)),
            pl.BlockSpec((tm, D_ATT), lambda i: (i, 0)),
            pl.BlockSpec((tm, LANES), lambda i: (i, 0)),
        ),
        compiler_params=_cparams("arbitrary"),
        name="dsa_prep",
    )(proj, proj, proj, proj, proj, tile(q_norm), tile(k_norm), seg)


def _dsa_kernel(qn_ref, qi_ref, w_ref, kn_ref, vt_ref, ki_ref, bias_ref, o_ref,
                key_ref, hi_ref, lo_ref, big_ref, qm_ref, qim_ref, wt_ref, mb_ref, carry_ref,
                m_ref, l_ref, acc_ref, *, blk, topk):
    i = pl.program_id(1)
    pack = 16
    lane = lax.broadcasted_iota(jnp.int32, (blk, LANES), 1)
    lo_lanes = lane < ATT_HEAD_DIM
    krow = lax.broadcasted_iota(jnp.int32, (blk, blk), 0)
    qcol = lax.broadcasted_iota(jnp.int32, (blk, blk), 1)
    future = krow > qcol

    for p in range(ATT_HEADS // 2):
        sl = slice(p * LANES, (p + 1) * LANES)
        qp = qn_ref[:, sl]
        qip = qi_ref[:, sl]
        zero = jnp.zeros_like(qp)
        qm_ref[2 * p * blk:(2 * p + 1) * blk] = jnp.where(lo_lanes, qp, zero)
        qm_ref[(2 * p + 1) * blk:(2 * p + 2) * blk] = jnp.where(lo_lanes, zero, qp)
        qim_ref[2 * p * blk:(2 * p + 1) * blk] = jnp.where(lo_lanes, qip, zero)
        qim_ref[(2 * p + 1) * blk:(2 * p + 2) * blk] = jnp.where(lo_lanes, zero, qip)
    wt_ref[...] = w_ref[...].T

    def score_dots(j, slot):
        kk = ki_ref[pl.ds(pl.multiple_of(j * blk, blk), blk), :]
        big_ref[slot] = _dot_nt(kk, qim_ref[...])

    def score_finish(j, slot, diag):
        s = jnp.zeros((blk, blk), F32)
        for h in range(IDX_HEADS):
            s = s + jnp.maximum(big_ref[slot, :, h * blk:(h + 1) * blk], 0.0) * wt_ref[h:h + 1, :]
        if diag:
            s = jnp.where(future, -jnp.inf, s)
        s = jnp.where(s == 0.0, 0.0, s)
        bits = pltpu.bitcast(s, jnp.int32)
        key = bits ^ ((bits >> 31) & 0x7FFFFFFF)
        key_ref[j] = key
        hi_ref[j] = (key >> 16).astype(jnp.int16)
        lo_ref[j] = ((key & 0xFFFF) - 32768).astype(jnp.int16)

    def by_parity(x, fn):
        @pl.when(x % 2 == 0)
        def _():
            fn(0)

        @pl.when(x % 2 == 1)
        def _():
            fn(1)

    score_dots(0, 0)

    def score_block(base, count):
        for u in range(count):
            score_dots(base + u + 1, (u + 1) % 2)
            score_finish(base + u, u % 2, False)

    def score_quad(t, carry):
        score_block(4 * t, 4)
        return carry

    lax.fori_loop(0, i // 4, score_quad, 0)

    @pl.when(i % 4 >= 2)
    def _():
        score_block(4 * (i // 4), 2)

    @pl.when(i % 2 == 1)
    def _():
        score_block(i - 1, 1)

    by_parity(i, lambda slot: score_finish(i, slot, True))

    one16 = jnp.ones((), jnp.int16)
    zero16 = jnp.zeros((), jnp.int16)

    def count_ge(src_ref, cand):
        cand_b = jnp.broadcast_to(cand.astype(jnp.int16), (blk, blk))

        def add_tile(j, cnts, on):
            ind = jnp.where(src_ref[j] >= cand_b, on, zero16)
            cnts = list(cnts)
            for g in range(blk // pack):
                cnts[g % len(cnts)] = cnts[g % len(cnts)] + ind[g * pack:(g + 1) * pack]
            return tuple(cnts)

        def body(t, cnts):
            return add_tile(2 * t + 1, add_tile(2 * t, cnts, one16), one16)

        zeros = tuple(jnp.zeros((pack, blk), jnp.int16) for _ in range(4))
        cnts = lax.fori_loop(0, (i + 1) // 2, body, zeros)
        cnts = add_tile(i, cnts, ((i + 1) % 2).astype(jnp.int16))
        cnt = (cnts[0] + cnts[1]) + (cnts[2] + cnts[3])
        return jnp.sum(cnt.astype(jnp.int32).astype(F32), axis=0, keepdims=True)

    def bisect16(src_ref, rank, cnt_all):
        def bit_body(b, st):
            prefix, c_acc, c_rej = st
            cand = prefix + lax.shift_left(jnp.int32(1), 15 - b)
            cnt = count_ge(src_ref, cand)
            ok = cnt >= rank
            return (jnp.where(ok, cand, prefix), jnp.where(ok, cnt, c_acc), jnp.where(ok, c_rej, cnt))
        init = (jnp.full((1, blk), -32768, jnp.int32), cnt_all, jnp.zeros((1, blk), F32))
        return lax.fori_loop(0, 16, bit_body, init)

    ncols = jnp.full((1, blk), ((i + 1) * blk).astype(F32), F32)
    p_hi, c_acc1, c_rej1 = bisect16(hi_ref, jnp.float32(topk), ncols)
    p_hi_b = jnp.broadcast_to(p_hi.astype(jnp.int16), (blk, blk))

    def group_body(j, carry):
        lo_ref[j] = jnp.where(hi_ref[j] == p_hi_b, lo_ref[j], jnp.int16(-32768))
        return carry

    lax.fori_loop(0, i + 1, group_body, 0)
    rank2 = jnp.float32(topk) - c_rej1
    p_lo, c_acc2, c_rej2 = bisect16(lo_ref, rank2, c_acc1 - c_rej1)
    thr = lax.shift_left(p_hi, 16) + (p_lo + 32768)
    need = rank2 - c_rej2
    has_ties = jnp.max((c_acc2 - c_rej2) - need) > 0.5

    m_ref[...] = jnp.full(m_ref.shape, NEG_BIG, F32)
    l_ref[...] = jnp.zeros(l_ref.shape, F32)
    acc_ref[...] = jnp.zeros(acc_ref.shape, F32)
    carry_ref[...] = jnp.zeros(carry_ref.shape, F32)

    def qk_dots(j, slot, mask_slot):
        start = pl.multiple_of(j * blk, blk)
        for p in range(ATT_HEADS // 2):
            mb2 = jnp.concatenate([mb_ref[mask_slot], mb_ref[mask_slot]], axis=1)
            big_ref[slot, :, 2 * p * blk:(2 * p + 2) * blk] = _dot_nt(
                kn_ref[pl.ds(start, blk), p * LANES:(p + 1) * LANES], qm_ref[2 * p * blk:(2 * p + 2) * blk]) + mb2

    def tile_mask(j, kind, slot):
        kt = key_ref[j]

        @pl.when(jnp.logical_not(has_ties))
        def _():
            mb = jnp.where(kt >= thr, 0.0, NEG_BIG)
            mb_ref[slot] = jnp.where(future, NEG_BIG, mb) if kind == 2 else mb

        @pl.when(has_ties)
        def _():
            eq = jnp.where(kt == thr, 1.0, 0.0)
            lower = jnp.where(krow >= qcol, 1.0, 0.0).astype(BF16)
            seen = _dot(lower, eq.astype(BF16)) + carry_ref[0:1, :]
            keep = jnp.where(seen <= need, 0.0, NEG_BIG)
            mb = jnp.where(kt > thr, 0.0, jnp.where(kt == thr, keep, NEG_BIG))
            mb_ref[slot] = jnp.where(future, NEG_BIG, mb) if kind == 2 else mb
            carry_ref[0:1, :] += jnp.sum(eq, axis=0, keepdims=True)

    ones_rows = jnp.ones((pack, blk), BF16)

    def softmax_pv(j, slot, kind):
        vt = vt_ref[j]
        for h in range(ATT_HEADS):
            half = blk // 2
            s = big_ref[slot, :, h * blk:(h + 1) * blk]
            if kind > 0:
                s = s + bias_ref[kind - 1, h]
            m_prev = m_ref[h]
            m_new = jnp.maximum(m_prev, jnp.max(s, axis=0, keepdims=True))
            alpha = jnp.exp2(m_prev - m_new)
            if kind > 0:
                pexp = jnp.exp2(s - m_new[0:1, :]).astype(BF16)
            else:
                pexp = jnp.concatenate(
                    [jnp.exp2(big_ref[slot, r * half:(r + 1) * half, h * blk:(h + 1) * blk]
                              - m_new[0:1, :]).astype(BF16) for r in range(2)], axis=0)
            m_ref[h] = m_new
            ch = slice(h * ATT_HEAD_DIM, (h + 1) * ATT_HEAD_DIM)
            pv = _dot(jnp.concatenate([vt[ch, :], ones_rows], axis=0), pexp)
            l_ref[h] = alpha * l_ref[h] + pv[ATT_HEAD_DIM:ATT_HEAD_DIM + 1, :]
            acc_ref[ch, :] = alpha[0:1, :] * acc_ref[ch, :] + pv[:ATT_HEAD_DIM, :]

    n_far = i - 1

    @pl.when(i == 0)
    def _():
        tile_mask(0, 2, 0)

    @pl.when(i > 0)
    def _():
        tile_mask(0, 0, 0)

    qk_dots(0, 0, 0)

    def attn_block(base, count):
        for u in range(count):
            tile_mask(base + u + 1, 0, (u + 1) % 4)
        for u in range(count):
            qk_dots(base + u + 1, (u + 1) % 2, (u + 1) % 4)
            softmax_pv(base + u, u % 2, 0)

    def attn_quad(t, carry):
        attn_block(4 * t, 4)
        return carry

    lax.fori_loop(0, n_far // 4, attn_quad, 0)

    @pl.when(jnp.logical_and(n_far > 0, n_far % 4 >= 2))
    def _():
        attn_block(4 * (n_far // 4), 2)

    @pl.when(jnp.logical_and(n_far > 0, n_far % 2 == 1))
    def _():
        attn_block(n_far - 1, 1)

    def near_tiles(slot):
        tile_mask(i, 2, 1 - slot)
        qk_dots(i, 1 - slot, 1 - slot)
        softmax_pv(i - 1, slot, 1)
        softmax_pv(i, 1 - slot, 2)

    @pl.when(i > 0)
    def _():
        by_parity(i - 1, near_tiles)

    @pl.when(i == 0)
    def _():
        softmax_pv(0, 0, 2)

    outs = []
    for h in range(ATT_HEADS):
        ch = slice(h * ATT_HEAD_DIM, (h + 1) * ATT_HEAD_DIM)
        outs.append(acc_ref[ch, :] / l_ref[h][0:1, :])
    o_ref[...] = jnp.concatenate(outs, axis=0).T.astype(BF16)


def _t5_bucket(dist):
    n = jnp.maximum(dist, 0)
    max_exact = REL_BUCKETS // 2
    large = max_exact + (jnp.log(jnp.maximum(n, max_exact).astype(F32) / max_exact)
                         / math.log(REL_MAX_DIST / max_exact)
                         * (REL_BUCKETS - max_exact)).astype(jnp.int32)
    large = jnp.minimum(large, REL_BUCKETS - 1)
    return jnp.where(n < max_exact, n, large)


def _bias_tiles(rel_bias, blk):
    width = 2 * blk
    f = ((rel_bias[_t5_bucket(jnp.arange(width, dtype=jnp.int32))] - rel_bias[REL_BUCKETS - 1]) * LOG2E).T
    h_sub = jnp.concatenate([f[:, blk:], f[:, :blk]], axis=1)
    h = jnp.stack([h_sub, f])
    skew = jnp.tile(h, (1, 1, blk))[:, :, :blk * (width - 1)].reshape(2, ATT_HEADS, blk, width - 1)
    return skew[:, :, :, :blk]


def _dsa(qn, qi, proj, kn, vt, ki, bias, *, bsz, seq, blk):
    n = bsz * seq
    nb = seq // blk
    topk = min(TOPK_MAX, seq // 4)
    once = pl.Buffered(1)
    return pl.pallas_call(
        functools.partial(_dsa_kernel, blk=blk, topk=topk),
        out_shape=jax.ShapeDtypeStruct((n, D_ATT), BF16),
        grid=(bsz, nb),
        in_specs=[
            pl.BlockSpec((blk, D_ATT), lambda b, i: (b * nb + i, 0)),
            pl.BlockSpec((blk, D_ATT), lambda b, i: (b * nb + i, 0)),
            pl.BlockSpec((blk, LANES), lambda b, i: (b * nb + i, COL_WIDX // LANES)),
            pl.BlockSpec((seq, D_ATT), lambda b, i: (b, 0), pipeline_mode=once),
            pl.BlockSpec((nb, D_ATT, blk), lambda b, i: (b, 0, 0), pipeline_mode=once),
            pl.BlockSpec((seq, LANES), lambda b, i: (b, 0), pipeline_mode=once),
            pl.BlockSpec((2, ATT_HEADS, blk, blk), lambda b, i: (0, 0, 0, 0), pipeline_mode=once),
        ],
        out_specs=pl.BlockSpec((blk, D_ATT), lambda b, i: (b * nb + i, 0)),
        scratch_shapes=[
            pltpu.VMEM((nb, blk, blk), jnp.int32),
            pltpu.VMEM((nb, blk, blk), jnp.int16),
            pltpu.VMEM((nb, blk, blk), jnp.int16),
            pltpu.VMEM((2, blk, ATT_HEADS * blk), F32),
            pltpu.VMEM((ATT_HEADS * blk, LANES), BF16),
            pltpu.VMEM((IDX_HEADS * blk, LANES), BF16),
            pltpu.VMEM((LANES, blk), F32),
            pltpu.VMEM((4, blk, blk), F32),
            pltpu.VMEM((SUBLANES, blk), F32),
            pltpu.VMEM((ATT_HEADS, SUBLANES, blk), F32),
            pltpu.VMEM((ATT_HEADS, SUBLANES, blk), F32),
            pltpu.VMEM((D_ATT, blk), F32),
        ],
        compiler_params=_cparams("arbitrary", "arbitrary"),
        name="dsa_attention",
    )(qn, qi, proj, kn, vt, ki, bias)


def _outproj_kernel(x_ref, yssd_ref, yatt_ref, scb_ref, scc_ref, sch_ref, hc_ref, hh_ref,
                    cw_ref, w_ref, o_ref, cat_ref, *, blocks_per_seq):
    i = pl.program_id(0)

    @pl.when(pl.program_id(1) == 0)
    def _():
        prev = jnp.where(i % blocks_per_seq == 0, 0.0, hc_ref[...] * hh_ref[...])
        ysc = scb_ref[...] * _causal_conv(scc_ref[...] * sch_ref[...], prev, cw_ref[...])
        cat_ref[:, :D_SSD] = yssd_ref[...]
        cat_ref[:, D_SSD:D_SSD + D_ATT] = yatt_ref[...]
        cat_ref[:, D_SSD + D_ATT:] = ysc.astype(BF16)

    o_ref[...] = x_ref[...] + _dot(cat_ref[...], w_ref[...])


def _outproj(x, y_ssd, y_att, proj, sc_w, w_out, layer, *, seq, tm, tn):
    n, d = x.shape
    halo = tm // SUBLANES
    blk = lambda col: pl.BlockSpec((tm, D_SC), lambda i, j: (i, col // D_SC))
    hblk = lambda col: pl.BlockSpec(
        (SUBLANES, D_SC), lambda i, j: (jnp.maximum(i * halo - 1, 0), col // D_SC))
    return pl.pallas_call(
        functools.partial(_outproj_kernel, blocks_per_seq=seq // tm),
        out_shape=jax.ShapeDtypeStruct((n, d), F32),
        grid=(n // tm, d // tn),
        in_specs=[
            pl.BlockSpec((tm, tn), lambda i, j: (i, j)),
            pl.BlockSpec((tm, D_SSD), lambda i, j: (i, 0)),
            pl.BlockSpec((tm, D_ATT), lambda i, j: (i, 0)),
            blk(COL_SCB), blk(COL_SCC), blk(COL_SCH), hblk(COL_SCC), hblk(COL_SCH),
            pl.BlockSpec((SC_CONV, D_SC), lambda i, j: (0, 0)),
            pl.BlockSpec((None, d, tn), lambda i, j: (layer, 0, j)),
        ],
        out_specs=pl.BlockSpec((tm, tn), lambda i, j: (i, j)),
        scratch_shapes=[pltpu.VMEM((tm, d), BF16)],
        compiler_params=_cparams("arbitrary", "arbitrary"),
        name="out_proj",
    )(x, y_ssd, y_att, proj, proj, proj, proj, proj, sc_w, w_out)


def _ffn_kernel(x_ref, g_ref, wg_ref, wu_ref, cw_ref, wd_ref, o_ref, h_ref, acc_ref, halo_ref,
                *, blocks_per_seq):
    i = pl.program_id(0)
    j = pl.program_id(1)
    tm = x_ref.shape[0]

    @pl.when(j == 0)
    def _():
        x = x_ref[...]
        ms = jnp.mean(x * x, axis=-1, keepdims=True)
        h_ref[...] = (x * lax.rsqrt(ms + NORM_EPS) * g_ref[...]).astype(BF16)
        acc_ref[...] = jnp.zeros(acc_ref.shape, F32)

    @pl.when(i % blocks_per_seq == 0)
    def _():
        halo_ref[j] = jnp.zeros(halo_ref.shape[1:], F32)

    h = h_ref[...]
    gate = _dot(h, wg_ref[...])
    up = _dot(h, wu_ref[...])
    prev = halo_ref[j]
    halo_ref[j] = gate[tm - SUBLANES:, :]
    gc = _causal_conv(gate, prev, cw_ref[...])
    act = (gc * _sigmoid(gc) * up).astype(BF16)
    acc_ref[...] += _dot(act, wd_ref[...])

    @pl.when(j == pl.num_programs(1) - 1)
    def _():
        o_ref[...] = x_ref[...] + acc_ref[...]


def _ffn(x, g, wg, wu, cw, wd, layer, *, seq, tm, tf):
    n, d = x.shape
    dff = wg.shape[2]
    return pl.pallas_call(
        functools.partial(_ffn_kernel, blocks_per_seq=seq // tm),
        out_shape=jax.ShapeDtypeStruct((n, d), F32),
        grid=(n // tm, dff // tf),
        in_specs=[
            pl.BlockSpec((tm, d), lambda i, j: (i, 0)),
            pl.BlockSpec((1, d), lambda i, j: (0, 0)),
            pl.BlockSpec((None, d, tf), lambda i, j: (layer, 0, j)),
            pl.BlockSpec((None, d, tf), lambda i, j: (layer, 0, j)),
            pl.BlockSpec((FFN_CONV, tf), lambda i, j: (0, j)),
            pl.BlockSpec((None, tf, d), lambda i, j: (layer, j, 0)),
        ],
        out_specs=pl.BlockSpec((tm, d), lambda i, j: (i, 0)),
        scratch_shapes=[
            pltpu.VMEM((tm, d), BF16),
            pltpu.VMEM((tm, d), F32),
            pltpu.VMEM((dff // tf, SUBLANES, tf), F32),
        ],
        compiler_params=_cparams("arbitrary", "arbitrary"),
        name="conv_gated_mlp",
    )(x, g, wg, wu, cw, wd)


_IN_SIZES = (D_SSD, SSD_CONV_DIM, SSD_HEADS, D_ATT, D_ATT, D_ATT,
             IDX_HEADS * IDX_DIM, IDX_DIM, IDX_HEADS, D_SC, D_SC, D_SC)
_IN_OFFS = tuple(sum(_IN_SIZES[:t]) for t in range(len(_IN_SIZES) + 1))
D_IN_PROJ = _IN_OFFS[-1]
_WIDE_SEGMENTS = ((COL_XBC, 1), (COL_Q, 3), (COL_Z, 0), (COL_K, 4), (COL_V, 5), (COL_QIDX, 6),
                  (COL_SCB, 9), (COL_SCC, 10), (COL_SCH, 11))


def _pack_w_in_kernel(w_ref, o_ref):
    tc = o_ref.shape[1]
    pack = 16

    def rows(seg):
        return w_ref[_IN_OFFS[seg]:_IN_OFFS[seg + 1], :]

    def put(dst, val, total):
        n = val.shape[0]
        pad = (-n) % pack
        if pad:
            val = jnp.concatenate([val, jnp.zeros((pad, tc), F32)], axis=0)
        o_ref[dst:dst + n + pad, :] = val.astype(BF16)
        if total > n + pad:
            o_ref[dst + n + pad:dst + total, :] = jnp.zeros((total - n - pad, tc), BF16)

    for dst, seg in _WIDE_SEGMENTS:
        put(dst, rows(seg), _IN_SIZES[seg])
    put(COL_DT, rows(2), LANES)
    put(COL_KIDX, jnp.concatenate([rows(7)] * (LANES // IDX_DIM), axis=0), LANES)
    put(COL_WIDX, rows(8), D_PACKED - COL_WIDX)


def _pack_w_in(w_in, *, tc=256):
    depth, d, _ = w_in.shape
    w_t = jnp.swapaxes(w_in, 1, 2)
    return pl.pallas_call(
        _pack_w_in_kernel,
        out_shape=jax.ShapeDtypeStruct((depth, D_PACKED, d), BF16),
        grid=(depth, d // tc),
        in_specs=[pl.BlockSpec((None, D_IN_PROJ, tc), lambda l, i: (l, 0, i))],
        out_specs=pl.BlockSpec((None, D_PACKED, tc), lambda l, i: (l, 0, i)),
        compiler_params=_cparams("arbitrary", "arbitrary"),
        name="pack_w_in",
    )(w_t)


def _forward(x, norm_mix, w_in, ssd_conv_w, ssd_conv_b, ssd_dt_bias, ssd_a_log, ssd_d, ssd_norm,
             att_q_norm, att_k_norm, rel_bias, sc_conv_w, w_out, norm_ffn,
             ffn_w_gate, ffn_w_up, ffn_conv_w, ffn_w_down, *, tm_proj, tn_proj, tm_out, tn_out,
             tm_ffn, tf_ffn, tm_prep, att_blk):
    bsz, seq, d = x.shape
    depth = w_in.shape[0]
    n = bsz * seq
    xf = x.reshape(n, d)
    w_in_p = _pack_w_in(w_in)
    w_out_b = w_out.astype(BF16)
    wg_b = ffn_w_gate.astype(BF16)
    wu_b = ffn_w_up.astype(BF16)
    wd_b = ffn_w_down.astype(BF16)
    bias = _bias_tiles(rel_bias, att_blk)
    for l in range(depth):
        proj = _rms_matmul(xf, norm_mix[l].reshape(1, d), w_in_p, l, tm=tm_proj, tn=tn_proj)
        y_ssd = _ssd(proj, ssd_conv_w[l], ssd_conv_b[l], ssd_dt_bias[l], ssd_a_log[l], ssd_d[l],
                     ssd_norm[l], bsz=bsz, seq=seq)
        qn, kn, vt, qi, ki = _dsa_prep(proj, att_q_norm[l], att_k_norm[l], tm=tm_prep, blk=att_blk)
        y_att = _dsa(qn, qi, proj, kn, vt, ki, bias, bsz=bsz, seq=seq, blk=att_blk)
        xf = _outproj(xf, y_ssd, y_att, proj, sc_conv_w[l], w_out_b, l, seq=seq, tm=tm_out, tn=tn_out)
        xf = _ffn(xf, norm_ffn[l].reshape(1, d), wg_b, wu_b, ffn_conv_w[l], wd_b, l,
                  seq=seq, tm=tm_ffn, tf=tf_ffn)
    return xf.reshape(bsz, seq, d)


def kernel(x, norm_mix, w_in, ssd_conv_w, ssd_conv_b, ssd_dt_bias, ssd_a_log, ssd_d, ssd_norm, att_q_norm, att_k_norm, rel_bias, sc_conv_w, w_out, norm_ffn, ffn_w_gate, ffn_w_up, ffn_conv_w, ffn_w_down):
    return _forward(x, norm_mix, w_in, ssd_conv_w, ssd_conv_b, ssd_dt_bias, ssd_a_log, ssd_d, ssd_norm,
                    att_q_norm, att_k_norm, rel_bias, sc_conv_w, w_out, norm_ffn,
                    ffn_w_gate, ffn_w_up, ffn_conv_w, ffn_w_down,
                    tm_proj=1024, tn_proj=512, tm_out=512, tn_out=2048,
                    tm_ffn=512, tf_ffn=512, tm_prep=1024, att_blk=256)
```

```python
import functools
import math

import jax
import jax.numpy as jnp
from jax import lax
from jax.experimental import pallas as pl
from jax.experimental.pallas import tpu as pltpu

F32 = jnp.float32
BF16 = jnp.bfloat16
HIGHEST = lax.Precision.HIGHEST

LANES = 128
SUBLANES = 8
VMEM_LIMIT_BYTES = 56 * 1024 * 1024

D_MODEL = 2048
D_SSD = 1024
D_ATT = 512
D_SC = 512
SSD_HEAD_DIM = 64
SSD_HEADS = 16
SSD_GROUPS = 2
SSD_STATE = 128
SSD_CONV = 4
SSD_CHUNK = 128
SSD_CONV_DIM = D_SSD + 2 * SSD_GROUPS * SSD_STATE
ATT_HEAD_DIM = 64
ATT_HEADS = 8
IDX_HEADS = 8
IDX_DIM = 64
TOPK_MAX = 256
REL_BUCKETS = 32
REL_MAX_DIST = 128
SC_CONV = 3
D_FF = 5632
FFN_CONV = 3
NORM_EPS = 1e-6

COL_XBC = 0
COL_Q = 1536
COL_Z = 2048
COL_K = 3072
COL_V = 3584
COL_QIDX = 4096
COL_SCB = 4608
COL_SCC = 5120
COL_SCH = 5632
COL_DT = 6144
COL_KIDX = 6272
COL_WIDX = 6400
D_PACKED = 6656

NEG_BIG = -1e30
LOG2E = math.log2(math.e)


def _cparams(*sem):
    return pltpu.CompilerParams(dimension_semantics=sem, vmem_limit_bytes=VMEM_LIMIT_BYTES)


def _dot(a, b):
    return jnp.dot(a, b, preferred_element_type=F32)


def _dot_nt(a, b):
    return lax.dot_general(a, b, (((1,), (1,)), ((), ())), preferred_element_type=F32)


def _dot_exact(a, b):
    return jnp.dot(a, b, preferred_element_type=F32, precision=HIGHEST)


def _expand_exact(x, onehot3):
    x1 = x.astype(BF16)
    r1 = x - x1.astype(F32)
    x2 = r1.astype(BF16)
    x3 = (r1 - x2.astype(F32)).astype(BF16)
    return _dot(jnp.concatenate([x1, x2, x3], axis=1), onehot3)


def _sigmoid(x):
    return 1.0 / (1.0 + jnp.exp(-x))


def _shift_rows(x, prev8, s):
    xr = pltpu.roll(x, s, 0)
    pr = pltpu.roll(prev8, s, 0)
    rows = lax.broadcasted_iota(jnp.int32, (SUBLANES, x.shape[1]), 0)
    top = jnp.where(rows < s, pr, xr[:SUBLANES])
    return jnp.concatenate([top, xr[SUBLANES:]], axis=0)


def _causal_conv(x, prev8, w):
    k = w.shape[0]
    y = w[k - 1:k] * x
    for s in range(1, k):
        y = y + w[k - 1 - s:k - s] * _shift_rows(x, prev8, s)
    return y


def _rms_matmul_kernel(x_ref, g_ref, w_ref, o_ref, h_ref):
    @pl.when(pl.program_id(1) == 0)
    def _():
        x = x_ref[...]
        ms = jnp.mean(x * x, axis=-1, keepdims=True)
        h = (x * lax.rsqrt(ms + NORM_EPS) * g_ref[...]).astype(BF16)
        h_ref[...] = h
        o_ref[...] = _dot_nt(h, w_ref[...])

    @pl.when(pl.program_id(1) > 0)
    def _():
        o_ref[...] = _dot_nt(h_ref[...], w_ref[...])


def _rms_matmul(x, g, w_t, layer, *, tm, tn):
    n, d = x.shape
    dout = w_t.shape[1]
    return pl.pallas_call(
        _rms_matmul_kernel,
        out_shape=jax.ShapeDtypeStruct((n, dout), F32),
        grid=(n // tm, dout // tn),
        in_specs=[
            pl.BlockSpec((tm, d), lambda i, j: (i, 0)),
            pl.BlockSpec((1, d), lambda i, j: (0, 0)),
            pl.BlockSpec((None, tn, d), lambda i, j: (layer, j, 0)),
        ],
        out_specs=pl.BlockSpec((tm, tn), lambda i, j: (i, j)),
        scratch_shapes=[pltpu.VMEM((tm, d), BF16)],
        compiler_params=_cparams("arbitrary", "arbitrary"),
        name="rms_in_proj",
    )(x, g, w_t)


def _ssd_kernel(xbc_ref, halo_ref, z_ref, dt_ref, cw_ref, cb_ref, dtb_ref, alog_ref,
                dx_ref, ng_ref, ex_ref, o_ref, h_ref, y_ref):
    c = pl.program_id(1)
    L = SSD_CHUNK

    @pl.when(c == 0)
    def _():
        h_ref[...] = jnp.zeros(h_ref.shape, F32)

    prev = jnp.where(c == 0, 0.0, halo_ref[...])
    xbc = _causal_conv(xbc_ref[...], prev, cw_ref[...]) + cb_ref[...]
    xbc = xbc * _sigmoid(xbc)
    xs = xbc[:, :D_SSD]

    row = lax.broadcasted_iota(jnp.int32, (L, L), 0)
    col = lax.broadcasted_iota(jnp.int32, (L, L), 1)
    tril = (row >= col).astype(F32)

    def softplus(v):
        return jnp.maximum(v, 0.0) + jnp.log1p(jnp.exp(-jnp.abs(v)))

    dt_s = softplus(dt_ref[...] + dtb_ref[...])
    a_s = dt_s * (-jnp.exp(alog_ref[...]))
    acs_s = _dot_exact(tril, a_s)
    acs_t = acs_s.T
    ex = ex_ref[...]
    dt_x = _expand_exact(dt_s, ex)
    e_x = _expand_exact(jnp.exp(acs_s), ex)
    de_x = _expand_exact(jnp.exp(acs_s[L - 1:L, :] - acs_s), ex)
    decay = e_x[L - 1:L, :]
    xdt = xs * dt_x
    xdt_end = (xdt * de_x).astype(BF16)
    xdt = xdt.astype(BF16)

    lane = lax.broadcasted_iota(jnp.int32, (L, LANES), 1)
    lo = lane < SSD_HEAD_DIM
    lower = row >= col

    for g in range(SSD_GROUPS):
        bm = xbc[:, D_SSD + g * SSD_STATE:D_SSD + (g + 1) * SSD_STATE].astype(BF16)
        cm = xbc[:, D_SSD + (SSD_GROUPS + g) * SSD_STATE:
                 D_SSD + (SSD_GROUPS + g + 1) * SSD_STATE].astype(BF16)
        cb = _dot_nt(cm, bm)
        bm_t = bm.T
        for q in range(4):
            pair = g * 4 + q
            sl = slice(pair * LANES, (pair + 1) * LANES)
            gs = []
            for hh in range(2):
                h = 2 * pair + hh
                seg = (jnp.broadcast_to(acs_s[:, h:h + 1], (L, L))
                       - jnp.broadcast_to(acs_t[h:h + 1, :], (L, L)))
                lm = jnp.where(lower, jnp.exp(seg), 0.0)
                gs.append((cb * lm).astype(BF16))
            gpair = jnp.concatenate(gs, axis=1)
            xp = xdt[:, sl]
            zero = jnp.zeros_like(xp)
            x2 = jnp.concatenate([jnp.where(lo, xp, zero), jnp.where(lo, zero, xp)], axis=0)
            y = _dot(gpair, x2)
            hprev = h_ref[pair]
            y = y + e_x[:, sl] * _dot(cm, hprev.astype(BF16))
            y = y + xs[:, sl] * dx_ref[:, sl]
            h_ref[pair] = hprev * decay[:, sl] + _dot(bm_t, xdt_end[:, sl])
            y_ref[:, sl] = y

    z = z_ref[...]
    y = y_ref[...] * (z * _sigmoid(z))
    half = D_SSD // SSD_GROUPS
    outs = []
    for g in range(SSD_GROUPS):
        yg = y[:, g * half:(g + 1) * half]
        ms = jnp.mean(yg * yg, axis=-1, keepdims=True)
        outs.append(yg * lax.rsqrt(ms + NORM_EPS))
    o_ref[...] = (jnp.concatenate(outs, axis=1) * ng_ref[...]).astype(BF16)


def _ssd(proj, conv_w, conv_b, dt_bias, a_log, d_skip, norm_g, *, bsz, seq):
    nc = seq // SSD_CHUNK
    n = bsz * seq
    pad = LANES - SSD_HEADS
    dtb = jnp.pad(dt_bias, (0, pad)).reshape(1, LANES)
    alog = jnp.pad(a_log, (0, pad), constant_values=NEG_BIG).reshape(1, LANES)
    rep = lambda v: jnp.repeat(v, SSD_HEAD_DIM).reshape(1, D_SSD)
    expand = (jnp.arange(LANES)[:, None] == (jnp.arange(D_SSD)[None, :] // SSD_HEAD_DIM)).astype(BF16)
    expand = jnp.concatenate([expand] * 3, axis=0)
    rows_per_halo = SSD_CHUNK // SUBLANES
    const = lambda shape: pl.BlockSpec(shape, lambda b, c: (0,) * len(shape))
    return pl.pallas_call(
        _ssd_kernel,
        out_shape=jax.ShapeDtypeStruct((n, D_SSD), BF16),
        grid=(bsz, nc),
        in_specs=[
            pl.BlockSpec((SSD_CHUNK, SSD_CONV_DIM), lambda b, c: (b * nc + c, COL_XBC // SSD_CONV_DIM)),
            pl.BlockSpec((SUBLANES, SSD_CONV_DIM),
                         lambda b, c: (jnp.maximum((b * nc + c) * rows_per_halo - 1, 0), 0)),
            pl.BlockSpec((SSD_CHUNK, D_SSD), lambda b, c: (b * nc + c, COL_Z // D_SSD)),
            pl.BlockSpec((SSD_CHUNK, LANES), lambda b, c: (b * nc + c, COL_DT // LANES)),
            const((SSD_CONV, SSD_CONV_DIM)),
            const((1, SSD_CONV_DIM)),
            const((1, LANES)),
            const((1, LANES)),
            const((1, D_SSD)),
            const((1, D_SSD)),
            const((3 * LANES, D_SSD)),
        ],
        out_specs=pl.BlockSpec((SSD_CHUNK, D_SSD), lambda b, c: (b * nc + c, 0)),
        scratch_shapes=[pltpu.VMEM((SSD_HEADS // 2, SSD_STATE, LANES), F32),
                        pltpu.VMEM((SSD_CHUNK, D_SSD), F32)],
        compiler_params=_cparams("arbitrary", "arbitrary"),
        name="ssd_scan",
    )(proj, proj, proj, proj, conv_w, conv_b.reshape(1, -1), dtb, alog,
      rep(d_skip), norm_g.reshape(1, -1), expand)


def _dsa_prep_kernel(q_ref, k_ref, v_ref, qi_ref, ki_ref, qg_ref, kg_ref, seg_ref,
                     qn_ref, kn_ref, vt_ref, qib_ref, kib_ref, *, blk):
    def head_norm(x, g):
        ms = _expand_exact(x * x, seg_ref[...]) * (1.0 / ATT_HEAD_DIM)
        return x * lax.rsqrt(ms + NORM_EPS) * g

    qn_ref[...] = (head_norm(q_ref[...], qg_ref[...]) * (LOG2E * ATT_HEAD_DIM ** -0.5)).astype(BF16)
    kn_ref[...] = head_norm(k_ref[...], kg_ref[...]).astype(BF16)
    for c in range(vt_ref.shape[0]):
        vt_ref[c] = v_ref[c * blk:(c + 1) * blk, :].T.astype(BF16)
    qib_ref[...] = qi_ref[...].astype(BF16)
    kib_ref[...] = ki_ref[...].astype(BF16)


def _dsa_prep(proj, q_norm, k_norm, *, tm, blk):
    n = proj.shape[0]
    seg = (jnp.arange(D_ATT)[:, None] // ATT_HEAD_DIM
           == jnp.arange(D_ATT)[None, :] // ATT_HEAD_DIM).astype(BF16)
    seg = jnp.concatenate([seg] * 3, axis=0)
    tile = lambda v: jnp.tile(v, ATT_HEADS).reshape(1, D_ATT)
    col_blk = lambda col, w: pl.BlockSpec((tm, w), lambda i: (i, col // w))
    const = lambda shape: pl.BlockSpec(shape, lambda i: (0,) * len(shape))
    return pl.pallas_call(
        functools.partial(_dsa_prep_kernel, blk=blk),
        out_shape=(
            jax.ShapeDtypeStruct((n, D_ATT), BF16),
            jax.ShapeDtypeStruct((n, D_ATT), BF16),
            jax.ShapeDtypeStruct((n // blk, D_ATT, blk), BF16),
            jax.ShapeDtypeStruct((n, D_ATT), BF16),
            jax.ShapeDtypeStruct((n, LANES), BF16),
        ),
        grid=(n // tm,),
        in_specs=[col_blk(COL_Q, D_ATT), col_blk(COL_K, D_ATT), col_blk(COL_V, D_ATT),
                  col_blk(COL_QIDX, D_ATT), col_blk(COL_KIDX, LANES),
                  const((1, D_ATT)), const((1, D_ATT)), const((3 * D_ATT, D_ATT))],
        out_specs=(
            pl.BlockSpec((tm, D_ATT), lambda i: (i, 0)),
            pl.BlockSpec((tm, D_ATT), lambda i: (i, 0)),
            pl.BlockSpec((tm // blk, D_ATT, blk), lambda i: (i, 0, 0)),
            pl.BlockSpec((tm, D_ATT), lambda i: (i, 0)),
            pl.BlockSpec((tm, LANES), lambda i: (i, 0)),
        ),
        compiler_params=_cparams("arbitrary"),
        name="dsa_prep",
    )(proj, proj, proj, proj, proj, tile(q_norm), tile(k_norm), seg)


def _dsa_kernel(qn_ref, qi_ref, w_ref, kn_ref, vt_ref, ki_ref, bias_ref, o_ref,
                key_ref, hi_ref, lo_ref, big_ref, qm_ref, qim_ref, wt_ref, mb_ref, carry_ref,
                m_ref, l_ref, acc_ref, *, blk, topk):
    i = pl.program_id(1)
    pack = 16
    lane = lax.broadcasted_iota(jnp.int32, (blk, LANES), 1)
    lo_lanes = lane < ATT_HEAD_DIM
    krow = lax.broadcasted_iota(jnp.int32, (blk, blk), 0)
    qcol = lax.broadcasted_iota(jnp.int32, (blk, blk), 1)
    future = krow > qcol

    for p in range(ATT_HEADS // 2):
        sl = slice(p * LANES, (p + 1) * LANES)
        qp = qn_ref[:, sl]
        qip = qi_ref[:, sl]
        zero = jnp.zeros_like(qp)
        qm_ref[2 * p * blk:(2 * p + 1) * blk] = jnp.where(lo_lanes, qp, zero)
        qm_ref[(2 * p + 1) * blk:(2 * p + 2) * blk] = jnp.where(lo_lanes, zero, qp)
        qim_ref[2 * p * blk:(2 * p + 1) * blk] = jnp.where(lo_lanes, qip, zero)
        qim_ref[(2 * p + 1) * blk:(2 * p + 2) * blk] = jnp.where(lo_lanes, zero, qip)
    wt_ref[...] = w_ref[...].T

    def score_dots(j, slot):
        kk = ki_ref[pl.ds(pl.multiple_of(j * blk, blk), blk), :]
        big_ref[slot] = _dot_nt(kk, qim_ref[...])

    def score_finish(j, slot, diag):
        s = jnp.zeros((blk, blk), F32)
        for h in range(IDX_HEADS):
            s = s + jnp.maximum(big_ref[slot, :, h * blk:(h + 1) * blk], 0.0) * wt_ref[h:h + 1, :]
        if diag:
            s = jnp.where(future, -jnp.inf, s)
        s = jnp.where(s == 0.0, 0.0, s)
        bits = pltpu.bitcast(s, jnp.int32)
        key = bits ^ ((bits >> 31) & 0x7FFFFFFF)
        key_ref[j] = key
        hi_ref[j] = (key >> 16).astype(jnp.int16)
        lo_ref[j] = ((key & 0xFFFF) - 32768).astype(jnp.int16)

    def by_parity(x, fn):
        @pl.when(x % 2 == 0)
        def _():
            fn(0)

        @pl.when(x % 2 == 1)
        def _():
            fn(1)

    score_dots(0, 0)

    def score_block(base, count):
        for u in range(count):
            score_dots(base + u + 1, (u + 1) % 2)
            score_finish(base + u, u % 2, False)

    def score_quad(t, carry):
        score_block(4 * t, 4)
        return carry

    lax.fori_loop(0, i // 4, score_quad, 0)

    @pl.when(i % 4 >= 2)
    def _():
        score_block(4 * (i // 4), 2)

    @pl.when(i % 2 == 1)
    def _():
        score_block(i - 1, 1)

    by_parity(i, lambda slot: score_finish(i, slot, True))

    one16 = jnp.ones((), jnp.int16)
    zero16 = jnp.zeros((), jnp.int16)

    def count_ge(src_ref, cand):
        cand_b = jnp.broadcast_to(cand.astype(jnp.int16), (blk, blk))

        def add_tile(j, cnts, on):
            ind = jnp.where(src_ref[j] >= cand_b, on, zero16)
            cnts = list(cnts)
            for g in range(blk // pack):
                cnts[g % len(cnts)] = cnts[g % len(cnts)] + ind[g * pack:(g + 1) * pack]
            return tuple(cnts)

        def body(t, cnts):
            return add_tile(2 * t + 1, add_tile(2 * t, cnts, one16), one16)

        zeros = tuple(jnp.zeros((pack, blk), jnp.int16) for _ in range(4))
        cnts = lax.fori_loop(0, (i + 1) // 2, body, zeros)
        cnts = add_tile(i, cnts, ((i + 1) % 2).astype(jnp.int16))
        cnt = (cnts[0] + cnts[1]) + (cnts[2] + cnts[3])
        return jnp.sum(cnt.astype(jnp.int32).astype(F32), axis=0, keepdims=True)

    def bisect16(src_ref, rank, cnt_all):
        def bit_body(b, st):
            prefix, c_acc, c_rej = st
            cand = prefix + lax.shift_left(jnp.int32(1), 15 - b)
            cnt = count_ge(src_ref, cand)
            ok = cnt >= rank
            return (jnp.where(ok, cand, prefix), jnp.where(ok, cnt, c_acc), jnp.where(ok, c_rej, cnt))
        init = (jnp.full((1, blk), -32768, jnp.int32), cnt_all, jnp.zeros((1, blk), F32))
        return lax.fori_loop(0, 16, bit_body, init)

    ncols = jnp.full((1, blk), ((i + 1) * blk).astype(F32), F32)
    p_hi, c_acc1, c_rej1 = bisect16(hi_ref, jnp.float32(topk), ncols)
    p_hi_b = jnp.broadcast_to(p_hi.astype(jnp.int16), (blk, blk))

    def group_body(j, carry):
        lo_ref[j] = jnp.where(hi_ref[j] == p_hi_b, lo_ref[j], jnp.int16(-32768))
        return carry

    lax.fori_loop(0, i + 1, group_body, 0)
    rank2 = jnp.float32(topk) - c_rej1
    p_lo, c_acc2, c_rej2 = bisect16(lo_ref, rank2, c_acc1 - c_rej1)
    thr = lax.shift_left(p_hi, 16) + (p_lo + 32768)
    need = rank2 - c_rej2
    has_ties = jnp.max((c_acc2 - c_rej2) - need) > 0.5

    m_ref[...] = jnp.full(m_ref.shape, NEG_BIG, F32)
    l_ref[...] = jnp.zeros(l_ref.shape, F32)
    acc_ref[...] = jnp.zeros(acc_ref.shape, F32)
    carry_ref[...] = jnp.zeros(carry_ref.shape, F32)

    def qk_dots(j, slot, mask_slot):
        start = pl.multiple_of(j * blk, blk)
        for p in range(ATT_HEADS // 2):
            mb2 = jnp.concatenate([mb_ref[mask_slot], mb_ref[mask_slot]], axis=1)
            big_ref[slot, :, 2 * p * blk:(2 * p + 2) * blk] = _dot_nt(
                kn_ref[pl.ds(start, blk), p * LANES:(p + 1) * LANES], qm_ref[2 * p * blk:(2 * p + 2) * blk]) + mb2

    def tile_mask(j, kind, slot):
        kt = key_ref[j]

        @pl.when(jnp.logical_not(has_ties))
        def _():
            mb = jnp.where(kt >= thr, 0.0, NEG_BIG)
            mb_ref[slot] = jnp.where(future, NEG_BIG, mb) if kind == 2 else mb

        @pl.when(has_ties)
        def _():
            eq = jnp.where(kt == thr, 1.0, 0.0)
            lower = jnp.where(krow >= qcol, 1.0, 0.0).astype(BF16)
            seen = _dot(lower, eq.astype(BF16)) + carry_ref[0:1, :]
            keep = jnp.where(seen <= need, 0.0, NEG_BIG)
            mb = jnp.where(kt > thr, 0.0, jnp.where(kt == thr, keep, NEG_BIG))
            mb_ref[slot] = jnp.where(future, NEG_BIG, mb) if kind == 2 else mb
            carry_ref[0:1, :] += jnp.sum(eq, axis=0, keepdims=True)

    ones_rows = jnp.ones((pack, blk), BF16)

    def softmax_pv(j, slot, kind):
        vt = vt_ref[j]
        for h in range(ATT_HEADS):
            half = blk // 2
            s = big_ref[slot, :, h * blk:(h + 1) * blk]
            if kind > 0:
                s = s + bias_ref[kind - 1, h]
            m_prev = m_ref[h]
            m_new = jnp.maximum(m_prev, jnp.max(s, axis=0, keepdims=True))
            alpha = jnp.exp2(m_prev - m_new)
            if kind > 0:
                pexp = jnp.exp2(s - m_new[0:1, :]).astype(BF16)
            else:
                pexp = jnp.concatenate(
                    [jnp.exp2(big_ref[slot, r * half:(r + 1) * half, h * blk:(h + 1) * blk]
                              - m_new[0:1, :]).astype(BF16) for r in range(2)], axis=0)
            m_ref[h] = m_new
            ch = slice(h * ATT_HEAD_DIM, (h + 1) * ATT_HEAD_DIM)
            pv = _dot(jnp.concatenate([vt[ch, :], ones_rows], axis=0), pexp)
            l_ref[h] = alpha * l_ref[h] + pv[ATT_HEAD_DIM:ATT_HEAD_DIM + 1, :]
            acc_ref[ch, :] = alpha[0:1, :] * acc_ref[ch, :] + pv[:ATT_HEAD_DIM, :]

    n_far = i - 1

    @pl.when(i == 0)
    def _():
        tile_mask(0, 2, 0)

    @pl.when(i > 0)
    def _():
        tile_mask(0, 0, 0)

    qk_dots(0, 0, 0)

    def attn_block(base, count):
        for u in range(count):
            tile_mask(base + u + 1, 0, (u + 1) % 4)
        for u in range(count):
            qk_dots(base + u + 1, (u + 1) % 2, (u + 1) % 4)
            softmax_pv(base + u, u % 2, 0)

    def attn_quad(t, carry):
        attn_block(4 * t, 4)
        return carry

    lax.fori_loop(0, n_far // 4, attn_quad, 0)

    @pl.when(jnp.logical_and(n_far > 0, n_far % 4 >= 2))
    def _():
        attn_block(4 * (n_far // 4), 2)

    @pl.when(jnp.logical_and(n_far > 0, n_far % 2 == 1))
    def _():
        attn_block(n_far - 1, 1)

    def near_tiles(slot):
        tile_mask(i, 2, 1 - slot)
        qk_dots(i, 1 - slot, 1 - slot)
        softmax_pv(i - 1, slot, 1)
        softmax_pv(i, 1 - slot, 2)

    @pl.when(i > 0)
    def _():
        by_parity(i - 1, near_tiles)

    @pl.when(i == 0)
    def _():
        softmax_pv(0, 0, 2)

    outs = []
    for h in range(ATT_HEADS):
        ch = slice(h * ATT_HEAD_DIM, (h + 1) * ATT_HEAD_DIM)
        outs.append(acc_ref[ch, :] / l_ref[h][0:1, :])
    o_ref[...] = jnp.concatenate(outs, axis=0).T.astype(BF16)


def _t5_bucket(dist):
    n = jnp.maximum(dist, 0)
    max_exact = REL_BUCKETS // 2
    large = max_exact + (jnp.log(jnp.maximum(n, max_exact).astype(F32) / max_exact)
                         / math.log(REL_MAX_DIST / max_exact)
                         * (REL_BUCKETS - max_exact)).astype(jnp.int32)
    large = jnp.minimum(large, REL_BUCKETS - 1)
    return jnp.where(n < max_exact, n, large)


def _bias_tiles(rel_bias, blk):
    width = 2 * blk
    f = ((rel_bias[_t5_bucket(jnp.arange(width, dtype=jnp.int32))] - rel_bias[REL_BUCKETS - 1]) * LOG2E).T
    h_sub = jnp.concatenate([f[:, blk:], f[:, :blk]], axis=1)
    h = jnp.stack([h_sub, f])
    skew = jnp.tile(h, (1, 1, blk))[:, :, :blk * (width - 1)].reshape(2, ATT_HEADS, blk, width - 1)
    return skew[:, :, :, :blk]


def _dsa(qn, qi, proj, kn, vt, ki, bias, *, bsz, seq, blk):
    n = bsz * seq
    nb = seq // blk
    topk = min(TOPK_MAX, seq // 4)
    once = pl.Buffered(1)
    return pl.pallas_call(
        functools.partial(_dsa_kernel, blk=blk, topk=topk),
        out_shape=jax.ShapeDtypeStruct((n, D_ATT), BF16),
        grid=(bsz, nb),
        in_specs=[
            pl.BlockSpec((blk, D_ATT), lambda b, i: (b * nb + i, 0)),
            pl.BlockSpec((blk, D_ATT), lambda b, i: (b * nb + i, 0)),
            pl.BlockSpec((blk, LANES), lambda b, i: (b * nb + i, COL_WIDX // LANES)),
            pl.BlockSpec((seq, D_ATT), lambda b, i: (b, 0), pipeline_mode=once),
            pl.BlockSpec((nb, D_ATT, blk), lambda b, i: (b, 0, 0), pipeline_mode=once),
            pl.BlockSpec((seq, LANES), lambda b, i: (b, 0), pipeline_mode=once),
            pl.BlockSpec((2, ATT_HEADS, blk, blk), lambda b, i: (0, 0, 0, 0), pipeline_mode=once),
        ],
        out_specs=pl.BlockSpec((blk, D_ATT), lambda b, i: (b * nb + i, 0)),
        scratch_shapes=[
            pltpu.VMEM((nb, blk, blk), jnp.int32),
            pltpu.VMEM((nb, blk, blk), jnp.int16),
            pltpu.VMEM((nb, blk, blk), jnp.int16),
            pltpu.VMEM((2, blk, ATT_HEADS * blk), F32),
            pltpu.VMEM((ATT_HEADS * blk, LANES), BF16),
            pltpu.VMEM((IDX_HEADS * blk, LANES), BF16),
            pltpu.VMEM((LANES, blk), F32),
            pltpu.VMEM((4, blk, blk), F32),
            pltpu.VMEM((SUBLANES, blk), F32),
            pltpu.VMEM((ATT_HEADS, SUBLANES, blk), F32),
            pltpu.VMEM((ATT_HEADS, SUBLANES, blk), F32),
            pltpu.VMEM((D_ATT, blk), F32),
        ],
        compiler_params=_cparams("arbitrary", "arbitrary"),
        name="dsa_attention",
    )(qn, qi, proj, kn, vt, ki, bias)


def _outproj_kernel(x_ref, yssd_ref, yatt_ref, scb_ref, scc_ref, sch_ref, hc_ref, hh_ref,
                    cw_ref, w_ref, o_ref, cat_ref, *, blocks_per_seq):
    i = pl.program_id(0)

    @pl.when(pl.program_id(1) == 0)
    def _():
        prev = jnp.where(i % blocks_per_seq == 0, 0.0, hc_ref[...] * hh_ref[...])
        ysc = scb_ref[...] * _causal_conv(scc_ref[...] * sch_ref[...], prev, cw_ref[...])
        cat_ref[:, :D_SSD] = yssd_ref[...]
        cat_ref[:, D_SSD:D_SSD + D_ATT] = yatt_ref[...]
        cat_ref[:, D_SSD + D_ATT:] = ysc.astype(BF16)

    o_ref[...] = x_ref[...] + _dot(cat_ref[...], w_ref[...])


def _outproj(x, y_ssd, y_att, proj, sc_w, w_out, layer, *, seq, tm, tn):
    n, d = x.shape
    halo = tm // SUBLANES
    blk = lambda col: pl.BlockSpec((tm, D_SC), lambda i, j: (i, col // D_SC))
    hblk = lambda col: pl.BlockSpec(
        (SUBLANES, D_SC), lambda i, j: (jnp.maximum(i * halo - 1, 0), col // D_SC))
    return pl.pallas_call(
        functools.partial(_outproj_kernel, blocks_per_seq=seq // tm),
        out_shape=jax.ShapeDtypeStruct((n, d), F32),
        grid=(n // tm, d // tn),
        in_specs=[
            pl.BlockSpec((tm, tn), lambda i, j: (i, j)),
            pl.BlockSpec((tm, D_SSD), lambda i, j: (i, 0)),
            pl.BlockSpec((tm, D_ATT), lambda i, j: (i, 0)),
            blk(COL_SCB), blk(COL_SCC), blk(COL_SCH), hblk(COL_SCC), hblk(COL_SCH),
            pl.BlockSpec((SC_CONV, D_SC), lambda i, j: (0, 0)),
            pl.BlockSpec((None, d, tn), lambda i, j: (layer, 0, j)),
        ],
        out_specs=pl.BlockSpec((tm, tn), lambda i, j: (i, j)),
        scratch_shapes=[pltpu.VMEM((tm, d), BF16)],
        compiler_params=_cparams("arbitrary", "arbitrary"),
        name="out_proj",
    )(x, y_ssd, y_att, proj, proj, proj, proj, proj, sc_w, w_out)


def _ffn_kernel(x_ref, g_ref, wg_ref, wu_ref, cw_ref, wd_ref, o_ref, h_ref, acc_ref, halo_ref,
                *, blocks_per_seq):
    i = pl.program_id(0)
    j = pl.program_id(1)
    last = pl.num_programs(1) - 1
    tm = x_ref.shape[0]

    @pl.when(i % blocks_per_seq == 0)
    def _():
        halo_ref[j] = jnp.zeros(halo_ref.shape[1:], F32)

    def step(first, final):
        if first:
            x = x_ref[...]
            ms = jnp.mean(x * x, axis=-1, keepdims=True)
            h = (x * lax.rsqrt(ms + NORM_EPS) * g_ref[...]).astype(BF16)
            h_ref[...] = h
        else:
            h = h_ref[...]
        gate = _dot(h, wg_ref[...])
        up = _dot(h, wu_ref[...])
        prev = halo_ref[j]
        halo_ref[j] = gate[tm - SUBLANES:, :]
        gc = _causal_conv(gate, prev, cw_ref[...])
        act = (gc * _sigmoid(gc) * up).astype(BF16)
        down = _dot(act, wd_ref[...])
        if first:
            acc_ref[...] = down
        elif final:
            o_ref[...] = x_ref[...] + (acc_ref[...] + down)
        else:
            acc_ref[...] += down

    @pl.when(j == 0)
    def _():
        step(True, False)

    @pl.when(jnp.logical_and(j > 0, j < last))
    def _():
        step(False, False)

    @pl.when(j == last)
    def _():
        step(False, True)


def _ffn(x, g, wg, wu, cw, wd, layer, *, seq, tm, tf):
    n, d = x.shape
    dff = wg.shape[2]
    assert dff // tf >= 2, "the kernel has distinct first and last column steps"
    return pl.pallas_call(
        functools.partial(_ffn_kernel, blocks_per_seq=seq // tm),
        out_shape=jax.ShapeDtypeStruct((n, d), F32),
        grid=(n // tm, dff // tf),
        in_specs=[
            pl.BlockSpec((tm, d), lambda i, j: (i, 0)),
            pl.BlockSpec((1, d), lambda i, j: (0, 0)),
            pl.BlockSpec((None, d, tf), lambda i, j: (layer, 0, j)),
            pl.BlockSpec((None, d, tf), lambda i, j: (layer, 0, j)),
            pl.BlockSpec((FFN_CONV, tf), lambda i, j: (0, j)),
            pl.BlockSpec((None, tf, d), lambda i, j: (layer, j, 0)),
        ],
        out_specs=pl.BlockSpec((tm, d), lambda i, j: (i, 0)),
        scratch_shapes=[
            pltpu.VMEM((tm, d), BF16),
            pltpu.VMEM((tm, d), F32),
            pltpu.VMEM((dff // tf, SUBLANES, tf), F32),
        ],
        compiler_params=_cparams("arbitrary", "arbitrary"),
        name="conv_gated_mlp",
    )(x, g, wg, wu, cw, wd)


_IN_SIZES = (D_SSD, SSD_CONV_DIM, SSD_HEADS, D_ATT, D_ATT, D_ATT,
             IDX_HEADS * IDX_DIM, IDX_DIM, IDX_HEADS, D_SC, D_SC, D_SC)
_IN_OFFS = tuple(sum(_IN_SIZES[:t]) for t in range(len(_IN_SIZES) + 1))
D_IN_PROJ = _IN_OFFS[-1]
_WIDE_SEGMENTS = ((COL_XBC, 1), (COL_Q, 3), (COL_Z, 0), (COL_K, 4), (COL_V, 5), (COL_QIDX, 6),
                  (COL_SCB, 9), (COL_SCC, 10), (COL_SCH, 11))


def _pack_w_in_kernel(w_ref, o_ref):
    tc = o_ref.shape[1]
    pack = 16

    def rows(seg):
        return w_ref[_IN_OFFS[seg]:_IN_OFFS[seg + 1], :]

    def put(dst, val, total):
        n = val.shape[0]
        pad = (-n) % pack
        if pad:
            val = jnp.concatenate([val, jnp.zeros((pad, tc), F32)], axis=0)
        o_ref[dst:dst + n + pad, :] = val.astype(BF16)
        if total > n + pad:
            o_ref[dst + n + pad:dst + total, :] = jnp.zeros((total - n - pad, tc), BF16)

    for dst, seg in _WIDE_SEGMENTS:
        put(dst, rows(seg), _IN_SIZES[seg])
    put(COL_DT, rows(2), LANES)
    put(COL_KIDX, jnp.concatenate([rows(7)] * (LANES // IDX_DIM), axis=0), LANES)
    put(COL_WIDX, rows(8), D_PACKED - COL_WIDX)


def _pack_w_in(w_in, *, tc=256):
    depth, d, _ = w_in.shape
    w_t = jnp.swapaxes(w_in, 1, 2)
    return pl.pallas_call(
        _pack_w_in_kernel,
        out_shape=jax.ShapeDtypeStruct((depth, D_PACKED, d), BF16),
        grid=(depth, d // tc),
        in_specs=[pl.BlockSpec((None, D_IN_PROJ, tc), lambda l, i: (l, 0, i))],
        out_specs=pl.BlockSpec((None, D_PACKED, tc), lambda l, i: (l, 0, i)),
        compiler_params=_cparams("arbitrary", "arbitrary"),
        name="pack_w_in",
    )(w_t)


def _forward(x, norm_mix, w_in, ssd_conv_w, ssd_conv_b, ssd_dt_bias, ssd_a_log, ssd_d, ssd_norm,
             att_q_norm, att_k_norm, rel_bias, sc_conv_w, w_out, norm_ffn,
             ffn_w_gate, ffn_w_up, ffn_conv_w, ffn_w_down, *, tm_proj, tn_proj, tm_out, tn_out,
             tm_ffn, tf_ffn, tm_prep, att_blk):
    bsz, seq, d = x.shape
    depth = w_in.shape[0]
    n = bsz * seq
    xf = x.reshape(n, d)
    w_in_p = _pack_w_in(w_in)
    w_out_b = w_out.astype(BF16)
    wg_b = ffn_w_gate.astype(BF16)
    wu_b = ffn_w_up.astype(BF16)
    wd_b = ffn_w_down.astype(BF16)
    bias = _bias_tiles(rel_bias, att_blk)
    for l in range(depth):
        proj = _rms_matmul(xf, norm_mix[l].reshape(1, d), w_in_p, l, tm=tm_proj, tn=tn_proj)
        y_ssd = _ssd(proj, ssd_conv_w[l], ssd_conv_b[l], ssd_dt_bias[l], ssd_a_log[l], ssd_d[l],
                     ssd_norm[l], bsz=bsz, seq=seq)
        qn, kn, vt, qi, ki = _dsa_prep(proj, att_q_norm[l], att_k_norm[l], tm=tm_prep, blk=att_blk)
        y_att = _dsa(qn, qi, proj, kn, vt, ki, bias, bsz=bsz, seq=seq, blk=att_blk)
        xf = _outproj(xf, y_ssd, y_att, proj, sc_conv_w[l], w_out_b, l, seq=seq, tm=tm_out, tn=tn_out)
        xf = _ffn(xf, norm_ffn[l].reshape(1, d), wg_b, wu_b, ffn_conv_w[l], wd_b, l,
                  seq=seq, tm=tm_ffn, tf=tf_ffn)
    return xf.reshape(bsz, seq, d)


def kernel(x, norm_mix, w_in, ssd_conv_w, ssd_conv_b, ssd_dt_bias, ssd_a_log, ssd_d, ssd_norm, att_q_norm, att_k_norm, rel_bias, sc_conv_w, w_out, norm_ffn, ffn_w_gate, ffn_w_up, ffn_conv_w, ffn_w_down):
    return _forward(x, norm_mix, w_in, ssd_conv_w, ssd_conv_b, ssd_dt_bias, ssd_a_log, ssd_d, ssd_norm,
                    att_q_norm, att_k_norm, rel_bias, sc_conv_w, w_out, norm_ffn,
                    ffn_w_gate, ffn_w_up, ffn_conv_w, ffn_w_down,
                    tm_proj=1024, tn_proj=512, tm_out=512, tn_out=2048,
                    tm_ffn=512, tf_ffn=512, tm_prep=1024, att_blk=256)
```

```python
import functools
import math

import jax
import jax.numpy as jnp
from jax import lax
from jax.experimental import pallas as pl
from jax.experimental.pallas import tpu as pltpu

F32 = jnp.float32
BF16 = jnp.bfloat16
HIGHEST = lax.Precision.HIGHEST

LANES = 128
SUBLANES = 8
VMEM_LIMIT_BYTES = 56 * 1024 * 1024

D_MODEL = 2048
D_SSD = 1024
D_ATT = 512
D_SC = 512
SSD_HEAD_DIM = 64
SSD_HEADS = 16
SSD_GROUPS = 2
SSD_STATE = 128
SSD_CONV = 4
SSD_CHUNK = 128
SSD_CONV_DIM = D_SSD + 2 * SSD_GROUPS * SSD_STATE
ATT_HEAD_DIM = 64
ATT_HEADS = 8
IDX_HEADS = 8
IDX_DIM = 64
TOPK_MAX = 256
REL_BUCKETS = 32
REL_MAX_DIST = 128
SC_CONV = 3
D_FF = 5632
FFN_CONV = 3
NORM_EPS = 1e-6

COL_XBC = 0
COL_Q = 1536
COL_Z = 2048
COL_K = 3072
COL_V = 3584
COL_QIDX = 4096
COL_SCB = 4608
COL_SCC = 5120
COL_SCH = 5632
COL_DT = 6144
COL_KIDX = 6272
COL_WIDX = 6400
D_PACKED = 6656

NEG_BIG = -1e30
LOG2E = math.log2(math.e)


def _cparams(*sem):
    return pltpu.CompilerParams(dimension_semantics=sem, vmem_limit_bytes=VMEM_LIMIT_BYTES)


def _dot(a, b):
    return jnp.dot(a, b, preferred_element_type=F32)


def _dot_nt(a, b):
    return lax.dot_general(a, b, (((1,), (1,)), ((), ())), preferred_element_type=F32)


def _dot_exact(a, b):
    return jnp.dot(a, b, preferred_element_type=F32, precision=HIGHEST)


def _expand_exact(x, onehot3):
    x1 = x.astype(BF16)
    r1 = x - x1.astype(F32)
    x2 = r1.astype(BF16)
    x3 = (r1 - x2.astype(F32)).astype(BF16)
    return _dot(jnp.concatenate([x1, x2, x3], axis=1), onehot3)


def _sigmoid(x):
    return 1.0 / (1.0 + jnp.exp(-x))


def _shift_rows(x, prev8, s):
    xr = pltpu.roll(x, s, 0)
    pr = pltpu.roll(prev8, s, 0)
    rows = lax.broadcasted_iota(jnp.int32, (SUBLANES, x.shape[1]), 0)
    top = jnp.where(rows < s, pr, xr[:SUBLANES])
    return jnp.concatenate([top, xr[SUBLANES:]], axis=0)


def _causal_conv(x, prev8, w):
    k = w.shape[0]
    y = w[k - 1:k] * x
    for s in range(1, k):
        y = y + w[k - 1 - s:k - s] * _shift_rows(x, prev8, s)
    return y


def _rms_matmul_kernel(x_ref, g_ref, w_ref, o_ref, h_ref):
    @pl.when(pl.program_id(1) == 0)
    def _():
        x = x_ref[...]
        ms = jnp.mean(x * x, axis=-1, keepdims=True)
        h = (x * lax.rsqrt(ms + NORM_EPS) * g_ref[...]).astype(BF16)
        h_ref[...] = h
        o_ref[...] = _dot_nt(h, w_ref[...])

    @pl.when(pl.program_id(1) > 0)
    def _():
        o_ref[...] = _dot_nt(h_ref[...], w_ref[...])


def _rms_matmul(x, g, w_t, layer, *, tm, tn):
    n, d = x.shape
    dout = w_t.shape[1]
    return pl.pallas_call(
        _rms_matmul_kernel,
        out_shape=jax.ShapeDtypeStruct((n, dout), F32),
        grid=(n // tm, dout // tn),
        in_specs=[
            pl.BlockSpec((tm, d), lambda i, j: (i, 0)),
            pl.BlockSpec((1, d), lambda i, j: (0, 0)),
            pl.BlockSpec((None, tn, d), lambda i, j: (layer, j, 0)),
        ],
        out_specs=pl.BlockSpec((tm, tn), lambda i, j: (i, j)),
        scratch_shapes=[pltpu.VMEM((tm, d), BF16)],
        compiler_params=_cparams("arbitrary", "arbitrary"),
        name="rms_in_proj",
    )(x, g, w_t)


def _ssd_kernel(xbc_ref, halo_ref, z_ref, dt_ref, cw_ref, cb_ref, dtb_ref, alog_ref,
                dx_ref, ng_ref, ex_ref, o_ref, h_ref, y_ref):
    c = pl.program_id(1)
    L = SSD_CHUNK

    @pl.when(c == 0)
    def _():
        h_ref[...] = jnp.zeros(h_ref.shape, F32)

    prev = jnp.where(c == 0, 0.0, halo_ref[...])
    xbc = _causal_conv(xbc_ref[...], prev, cw_ref[...]) + cb_ref[...]
    xbc = xbc * _sigmoid(xbc)
    xs = xbc[:, :D_SSD]

    row = lax.broadcasted_iota(jnp.int32, (L, L), 0)
    col = lax.broadcasted_iota(jnp.int32, (L, L), 1)
    tril = (row >= col).astype(F32)

    def softplus(v):
        return jnp.maximum(v, 0.0) + jnp.log1p(jnp.exp(-jnp.abs(v)))

    dt_s = softplus(dt_ref[...] + dtb_ref[...])
    a_s = dt_s * (-jnp.exp(alog_ref[...]))
    acs_s = _dot_exact(tril, a_s)
    acs_t = acs_s.T
    ex = ex_ref[...]
    dt_x = _expand_exact(dt_s, ex)
    e_x = _expand_exact(jnp.exp(acs_s), ex)
    de_x = _expand_exact(jnp.exp(acs_s[L - 1:L, :] - acs_s), ex)
    decay = e_x[L - 1:L, :]
    xdt = xs * dt_x
    xdt_end = (xdt * de_x).astype(BF16)
    xdt = xdt.astype(BF16)

    lane = lax.broadcasted_iota(jnp.int32, (L, LANES), 1)
    lo = lane < SSD_HEAD_DIM
    lower = row >= col

    for g in range(SSD_GROUPS):
        bm = xbc[:, D_SSD + g * SSD_STATE:D_SSD + (g + 1) * SSD_STATE].astype(BF16)
        cm = xbc[:, D_SSD + (SSD_GROUPS + g) * SSD_STATE:
                 D_SSD + (SSD_GROUPS + g + 1) * SSD_STATE].astype(BF16)
        cb = _dot_nt(cm, bm)
        bm_t = bm.T
        for q in range(4):
            pair = g * 4 + q
            sl = slice(pair * LANES, (pair + 1) * LANES)
            gs = []
            for hh in range(2):
                h = 2 * pair + hh
                seg = (jnp.broadcast_to(acs_s[:, h:h + 1], (L, L))
                       - jnp.broadcast_to(acs_t[h:h + 1, :], (L, L)))
                lm = jnp.where(lower, jnp.exp(seg), 0.0)
                gs.append((cb * lm).astype(BF16))
            gpair = jnp.concatenate(gs, axis=1)
            xp = xdt[:, sl]
            zero = jnp.zeros_like(xp)
            x2 = jnp.concatenate([jnp.where(lo, xp, zero), jnp.where(lo, zero, xp)], axis=0)
            y = _dot(gpair, x2)
            hprev = h_ref[pair]
            y = y + e_x[:, sl] * _dot(cm, hprev.astype(BF16))
            y = y + xs[:, sl] * dx_ref[:, sl]
            h_ref[pair] = hprev * decay[:, sl] + _dot(bm_t, xdt_end[:, sl])
            y_ref[:, sl] = y

    z = z_ref[...]
    y = y_ref[...] * (z * _sigmoid(z))
    half = D_SSD // SSD_GROUPS
    outs = []
    for g in range(SSD_GROUPS):
        yg = y[:, g * half:(g + 1) * half]
        ms = jnp.mean(yg * yg, axis=-1, keepdims=True)
        outs.append(yg * lax.rsqrt(ms + NORM_EPS))
    o_ref[...] = (jnp.concatenate(outs, axis=1) * ng_ref[...]).astype(BF16)


def _ssd(proj, conv_w, conv_b, dt_bias, a_log, d_skip, norm_g, *, bsz, seq):
    nc = seq // SSD_CHUNK
    n = bsz * seq
    pad = LANES - SSD_HEADS
    dtb = jnp.pad(dt_bias, (0, pad)).reshape(1, LANES)
    alog = jnp.pad(a_log, (0, pad), constant_values=NEG_BIG).reshape(1, LANES)
    rep = lambda v: jnp.repeat(v, SSD_HEAD_DIM).reshape(1, D_SSD)
    expand = (jnp.arange(LANES)[:, None] == (jnp.arange(D_SSD)[None, :] // SSD_HEAD_DIM)).astype(BF16)
    expand = jnp.concatenate([expand] * 3, axis=0)
    rows_per_halo = SSD_CHUNK // SUBLANES
    const = lambda shape: pl.BlockSpec(shape, lambda b, c: (0,) * len(shape))
    return pl.pallas_call(
        _ssd_kernel,
        out_shape=jax.ShapeDtypeStruct((n, D_SSD), BF16),
        grid=(bsz, nc),
        in_specs=[
            pl.BlockSpec((SSD_CHUNK, SSD_CONV_DIM), lambda b, c: (b * nc + c, COL_XBC // SSD_CONV_DIM)),
            pl.BlockSpec((SUBLANES, SSD_CONV_DIM),
                         lambda b, c: (jnp.maximum((b * nc + c) * rows_per_halo - 1, 0), 0)),
            pl.BlockSpec((SSD_CHUNK, D_SSD), lambda b, c: (b * nc + c, COL_Z // D_SSD)),
            pl.BlockSpec((SSD_CHUNK, LANES), lambda b, c: (b * nc + c, COL_DT // LANES)),
            const((SSD_CONV, SSD_CONV_DIM)),
            const((1, SSD_CONV_DIM)),
            const((1, LANES)),
            const((1, LANES)),
            const((1, D_SSD)),
            const((1, D_SSD)),
            const((3 * LANES, D_SSD)),
        ],
        out_specs=pl.BlockSpec((SSD_CHUNK, D_SSD), lambda b, c: (b * nc + c, 0)),
        scratch_shapes=[pltpu.VMEM((SSD_HEADS // 2, SSD_STATE, LANES), F32),
                        pltpu.VMEM((SSD_CHUNK, D_SSD), F32)],
        compiler_params=_cparams("arbitrary", "arbitrary"),
        name="ssd_scan",
    )(proj, proj, proj, proj, conv_w, conv_b.reshape(1, -1), dtb, alog,
      rep(d_skip), norm_g.reshape(1, -1), expand)


def _dsa_prep_kernel(q_ref, k_ref, v_ref, qi_ref, ki_ref, qg_ref, kg_ref, seg_ref,
                     qn_ref, kn_ref, vt_ref, qib_ref, kib_ref, *, blk):
    def head_norm(x, g):
        ms = _expand_exact(x * x, seg_ref[...]) * (1.0 / ATT_HEAD_DIM)
        return x * lax.rsqrt(ms + NORM_EPS) * g

    qn_ref[...] = (head_norm(q_ref[...], qg_ref[...]) * (LOG2E * ATT_HEAD_DIM ** -0.5)).astype(BF16)
    kn_ref[...] = head_norm(k_ref[...], kg_ref[...]).astype(BF16)
    for c in range(vt_ref.shape[0]):
        vt_ref[c] = v_ref[c * blk:(c + 1) * blk, :].T.astype(BF16)
    qib_ref[...] = qi_ref[...].astype(BF16)
    kib_ref[...] = ki_ref[...].astype(BF16)


def _dsa_prep(proj, q_norm, k_norm, *, tm, blk):
    n = proj.shape[0]
    seg = (jnp.arange(D_ATT)[:, None] // ATT_HEAD_DIM
           == jnp.arange(D_ATT)[None, :] // ATT_HEAD_DIM).astype(BF16)
    seg = jnp.concatenate([seg] * 3, axis=0)
    tile = lambda v: jnp.tile(v, ATT_HEADS).reshape(1, D_ATT)
    col_blk = lambda col, w: pl.BlockSpec((tm, w), lambda i: (i, col // w))
    const = lambda shape: pl.BlockSpec(shape, lambda i: (0,) * len(shape))
    return pl.pallas_call(
        functools.partial(_dsa_prep_kernel, blk=blk),
        out_shape=(
            jax.ShapeDtypeStruct((n, D_ATT), BF16),
            jax.ShapeDtypeStruct((n, D_ATT), BF16),
            jax.ShapeDtypeStruct((n // blk, D_ATT, blk), BF16),
            jax.ShapeDtypeStruct((n, D_ATT), BF16),
            jax.ShapeDtypeStruct((n, LANES), BF16),
        ),
        grid=(n // tm,),
        in_specs=[col_blk(COL_Q, D_ATT), col_blk(COL_K, D_ATT), col_blk(COL_V, D_ATT),
                  col_blk(COL_QIDX, D_ATT), col_blk(COL_KIDX, LANES),
                  const((1, D_ATT)), const((1, D_ATT)), const((3 * D_ATT, D_ATT))],
        out_specs=(
            pl.BlockSpec((tm, D_ATT), lambda i: (i, 0)),
            pl.BlockSpec((tm, D_ATT), lambda i: (i, 0)),
            pl.BlockSpec((tm // blk, D_ATT, blk), lambda i: (i, 0, 0)),
            pl.BlockSpec((tm, D_ATT), lambda i: (i, 0)),
            pl.BlockSpec((tm, LANES), lambda i: (i, 0)),
        ),
        compiler_params=_cparams("arbitrary"),
        name="dsa_prep",
    )(proj, proj, proj, proj, proj, tile(q_norm), tile(k_norm), seg)


def _dsa_kernel(qn_ref, qi_ref, w_ref, kn_ref, vt_ref, ki_ref, bias_ref, o_ref,
                key_ref, hi_ref, lo_ref, big_ref, qm_ref, qim_ref, wt_ref, mb_ref, carry_ref,
                m_ref, l_ref, acc_ref, *, blk, topk):
    i = pl.program_id(1)
    pack = 16
    lane = lax.broadcasted_iota(jnp.int32, (blk, LANES), 1)
    lo_lanes = lane < ATT_HEAD_DIM
    krow = lax.broadcasted_iota(jnp.int32, (blk, blk), 0)
    qcol = lax.broadcasted_iota(jnp.int32, (blk, blk), 1)
    future = krow > qcol

    for p in range(ATT_HEADS // 2):
        sl = slice(p * LANES, (p + 1) * LANES)
        qp = qn_ref[:, sl]
        qip = qi_ref[:, sl]
        zero = jnp.zeros_like(qp)
        qm_ref[2 * p * blk:(2 * p + 1) * blk] = jnp.where(lo_lanes, qp, zero)
        qm_ref[(2 * p + 1) * blk:(2 * p + 2) * blk] = jnp.where(lo_lanes, zero, qp)
        qim_ref[2 * p * blk:(2 * p + 1) * blk] = jnp.where(lo_lanes, qip, zero)
        qim_ref[(2 * p + 1) * blk:(2 * p + 2) * blk] = jnp.where(lo_lanes, zero, qip)
    wt_ref[...] = w_ref[...].T

    def score_dots(j, slot):
        kk = ki_ref[pl.ds(pl.multiple_of(j * blk, blk), blk), :]
        big_ref[slot] = _dot_nt(kk, qim_ref[...])

    def score_finish(j, slot, diag):
        s = jnp.zeros((blk, blk), F32)
        for h in range(IDX_HEADS):
            s = s + jnp.maximum(big_ref[slot, :, h * blk:(h + 1) * blk], 0.0) * wt_ref[h:h + 1, :]
        if diag:
            s = jnp.where(future, -jnp.inf, s)
        s = jnp.where(s == 0.0, 0.0, s)
        bits = pltpu.bitcast(s, jnp.int32)
        key = bits ^ ((bits >> 31) & 0x7FFFFFFF)
        key_ref[j] = key
        hi_ref[j] = (key >> 16).astype(jnp.int16)
        lo_ref[j] = ((key & 0xFFFF) - 32768).astype(jnp.int16)

    def by_parity(x, fn):
        @pl.when(x % 2 == 0)
        def _():
            fn(0)

        @pl.when(x % 2 == 1)
        def _():
            fn(1)

    score_dots(0, 0)

    def score_block(base, count):
        for u in range(count):
            score_dots(base + u + 1, (u + 1) % 2)
            score_finish(base + u, u % 2, False)

    def score_oct(t, carry):
        score_block(8 * t, 8)
        return carry

    lax.fori_loop(0, i // 8, score_oct, 0)

    @pl.when(i % 8 >= 4)
    def _():
        score_block(8 * (i // 8), 4)

    @pl.when(i % 4 >= 2)
    def _():
        score_block(4 * (i // 4), 2)

    @pl.when(i % 2 == 1)
    def _():
        score_block(i - 1, 1)

    by_parity(i, lambda slot: score_finish(i, slot, True))

    one16 = jnp.ones((), jnp.int16)
    zero16 = jnp.zeros((), jnp.int16)

    def count_ge(src_ref, cand):
        cand_b = jnp.broadcast_to(cand.astype(jnp.int16), (blk, blk))

        def add_tile(j, cnts, on):
            ind = jnp.where(src_ref[j] >= cand_b, on, zero16)
            cnts = list(cnts)
            for g in range(blk // pack):
                cnts[g % len(cnts)] = cnts[g % len(cnts)] + ind[g * pack:(g + 1) * pack]
            return tuple(cnts)

        def body(t, cnts):
            return add_tile(2 * t + 1, add_tile(2 * t, cnts, one16), one16)

        zeros = tuple(jnp.zeros((pack, blk), jnp.int16) for _ in range(4))
        cnts = lax.fori_loop(0, (i + 1) // 2, body, zeros)
        cnts = add_tile(i, cnts, ((i + 1) % 2).astype(jnp.int16))
        cnt = (cnts[0] + cnts[1]) + (cnts[2] + cnts[3])
        return jnp.sum(cnt.astype(jnp.int32).astype(F32), axis=0, keepdims=True)

    def bisect16(src_ref, rank, cnt_all):
        def bit_body(b, st):
            prefix, c_acc, c_rej = st
            cand = prefix + lax.shift_left(jnp.int32(1), 15 - b)
            cnt = count_ge(src_ref, cand)
            ok = cnt >= rank
            return (jnp.where(ok, cand, prefix), jnp.where(ok, cnt, c_acc), jnp.where(ok, c_rej, cnt))
        init = (jnp.full((1, blk), -32768, jnp.int32), cnt_all, jnp.zeros((1, blk), F32))
        return lax.fori_loop(0, 16, bit_body, init)

    ncols = jnp.full((1, blk), ((i + 1) * blk).astype(F32), F32)
    p_hi, c_acc1, c_rej1 = bisect16(hi_ref, jnp.float32(topk), ncols)
    p_hi_b = jnp.broadcast_to(p_hi.astype(jnp.int16), (blk, blk))

    def group_body(j, carry):
        lo_ref[j] = jnp.where(hi_ref[j] == p_hi_b, lo_ref[j], jnp.int16(-32768))
        return carry

    lax.fori_loop(0, i + 1, group_body, 0)
    rank2 = jnp.float32(topk) - c_rej1
    p_lo, c_acc2, c_rej2 = bisect16(lo_ref, rank2, c_acc1 - c_rej1)
    thr = lax.shift_left(p_hi, 16) + (p_lo + 32768)
    need = rank2 - c_rej2
    has_ties = jnp.max((c_acc2 - c_rej2) - need) > 0.5

    m_ref[...] = jnp.full(m_ref.shape, NEG_BIG, F32)
    l_ref[...] = jnp.zeros(l_ref.shape, F32)
    acc_ref[...] = jnp.zeros(acc_ref.shape, F32)
    carry_ref[...] = jnp.zeros(carry_ref.shape, F32)

    def qk_dots(j, slot, mask_slot):
        start = pl.multiple_of(j * blk, blk)
        for p in range(ATT_HEADS // 2):
            mb2 = jnp.concatenate([mb_ref[mask_slot], mb_ref[mask_slot]], axis=1)
            big_ref[slot, :, 2 * p * blk:(2 * p + 2) * blk] = _dot_nt(
                kn_ref[pl.ds(start, blk), p * LANES:(p + 1) * LANES], qm_ref[2 * p * blk:(2 * p + 2) * blk]) + mb2

    def tile_mask(j, kind, slot):
        kt = key_ref[j]

        @pl.when(jnp.logical_not(has_ties))
        def _():
            mb = jnp.where(kt >= thr, 0.0, NEG_BIG)
            mb_ref[slot] = jnp.where(future, NEG_BIG, mb) if kind == 2 else mb

        @pl.when(has_ties)
        def _():
            eq = jnp.where(kt == thr, 1.0, 0.0)
            lower = jnp.where(krow >= qcol, 1.0, 0.0).astype(BF16)
            seen = _dot(lower, eq.astype(BF16)) + carry_ref[0:1, :]
            keep = jnp.where(seen <= need, 0.0, NEG_BIG)
            mb = jnp.where(kt > thr, 0.0, jnp.where(kt == thr, keep, NEG_BIG))
            mb_ref[slot] = jnp.where(future, NEG_BIG, mb) if kind == 2 else mb
            carry_ref[0:1, :] += jnp.sum(eq, axis=0, keepdims=True)

    ones_rows = jnp.ones((pack, blk), BF16)

    def softmax_pv(j, slot, kind):
        vt = vt_ref[j]
        for h in range(ATT_HEADS):
            half = blk // 2
            s = big_ref[slot, :, h * blk:(h + 1) * blk]
            if kind > 0:
                s = s + bias_ref[kind - 1, h]
            m_prev = m_ref[h]
            m_new = jnp.maximum(m_prev, jnp.max(s, axis=0, keepdims=True))
            alpha = jnp.exp2(m_prev - m_new)
            if kind > 0:
                pexp = jnp.exp2(s - m_new[0:1, :]).astype(BF16)
            else:
                pexp = jnp.concatenate(
                    [jnp.exp2(big_ref[slot, r * half:(r + 1) * half, h * blk:(h + 1) * blk]
                              - m_new[0:1, :]).astype(BF16) for r in range(2)], axis=0)
            m_ref[h] = m_new
            ch = slice(h * ATT_HEAD_DIM, (h + 1) * ATT_HEAD_DIM)
            pv = _dot(jnp.concatenate([vt[ch, :], ones_rows], axis=0), pexp)
            l_ref[h] = alpha * l_ref[h] + pv[ATT_HEAD_DIM:ATT_HEAD_DIM + 1, :]
            acc_ref[ch, :] = alpha[0:1, :] * acc_ref[ch, :] + pv[:ATT_HEAD_DIM, :]

    n_far = i - 1

    @pl.when(i == 0)
    def _():
        tile_mask(0, 2, 0)

    @pl.when(i > 0)
    def _():
        tile_mask(0, 0, 0)

    qk_dots(0, 0, 0)

    def attn_block(base, count):
        for u in range(count):
            tile_mask(base + u + 1, 0, (u + 1) % 4)
        for u in range(count):
            qk_dots(base + u + 1, (u + 1) % 2, (u + 1) % 4)
            softmax_pv(base + u, u % 2, 0)

    def attn_quad(t, carry):
        attn_block(4 * t, 4)
        return carry

    lax.fori_loop(0, n_far // 4, attn_quad, 0)

    @pl.when(jnp.logical_and(n_far > 0, n_far % 4 >= 2))
    def _():
        attn_block(4 * (n_far // 4), 2)

    @pl.when(jnp.logical_and(n_far > 0, n_far % 2 == 1))
    def _():
        attn_block(n_far - 1, 1)

    def near_tiles(slot):
        tile_mask(i, 2, 1 - slot)
        qk_dots(i, 1 - slot, 1 - slot)
        softmax_pv(i - 1, slot, 1)
        softmax_pv(i, 1 - slot, 2)

    @pl.when(i > 0)
    def _():
        by_parity(i - 1, near_tiles)

    @pl.when(i == 0)
    def _():
        softmax_pv(0, 0, 2)

    outs = []
    for h in range(ATT_HEADS):
        ch = slice(h * ATT_HEAD_DIM, (h + 1) * ATT_HEAD_DIM)
        outs.append(acc_ref[ch, :] / l_ref[h][0:1, :])
    o_ref[...] = jnp.concatenate(outs, axis=0).T.astype(BF16)


def _t5_bucket(dist):
    n = jnp.maximum(dist, 0)
    max_exact = REL_BUCKETS // 2
    large = max_exact + (jnp.log(jnp.maximum(n, max_exact).astype(F32) / max_exact)
                         / math.log(REL_MAX_DIST / max_exact)
                         * (REL_BUCKETS - max_exact)).astype(jnp.int32)
    large = jnp.minimum(large, REL_BUCKETS - 1)
    return jnp.where(n < max_exact, n, large)


def _bias_tiles(rel_bias, blk):
    width = 2 * blk
    f = ((rel_bias[_t5_bucket(jnp.arange(width, dtype=jnp.int32))] - rel_bias[REL_BUCKETS - 1]) * LOG2E).T
    h_sub = jnp.concatenate([f[:, blk:], f[:, :blk]], axis=1)
    h = jnp.stack([h_sub, f])
    skew = jnp.tile(h, (1, 1, blk))[:, :, :blk * (width - 1)].reshape(2, ATT_HEADS, blk, width - 1)
    return skew[:, :, :, :blk]


def _dsa(qn, qi, proj, kn, vt, ki, bias, *, bsz, seq, blk):
    n = bsz * seq
    nb = seq // blk
    topk = min(TOPK_MAX, seq // 4)
    once = pl.Buffered(1)
    return pl.pallas_call(
        functools.partial(_dsa_kernel, blk=blk, topk=topk),
        out_shape=jax.ShapeDtypeStruct((n, D_ATT), BF16),
        grid=(bsz, nb),
        in_specs=[
            pl.BlockSpec((blk, D_ATT), lambda b, i: (b * nb + i, 0)),
            pl.BlockSpec((blk, D_ATT), lambda b, i: (b * nb + i, 0)),
            pl.BlockSpec((blk, LANES), lambda b, i: (b * nb + i, COL_WIDX // LANES)),
            pl.BlockSpec((seq, D_ATT), lambda b, i: (b, 0), pipeline_mode=once),
            pl.BlockSpec((nb, D_ATT, blk), lambda b, i: (b, 0, 0), pipeline_mode=once),
            pl.BlockSpec((seq, LANES), lambda b, i: (b, 0), pipeline_mode=once),
            pl.BlockSpec((2, ATT_HEADS, blk, blk), lambda b, i: (0, 0, 0, 0), pipeline_mode=once),
        ],
        out_specs=pl.BlockSpec((blk, D_ATT), lambda b, i: (b * nb + i, 0)),
        scratch_shapes=[
            pltpu.VMEM((nb, blk, blk), jnp.int32),
            pltpu.VMEM((nb, blk, blk), jnp.int16),
            pltpu.VMEM((nb, blk, blk), jnp.int16),
            pltpu.VMEM((2, blk, ATT_HEADS * blk), F32),
            pltpu.VMEM((ATT_HEADS * blk, LANES), BF16),
            pltpu.VMEM((IDX_HEADS * blk, LANES), BF16),
            pltpu.VMEM((LANES, blk), F32),
            pltpu.VMEM((4, blk, blk), F32),
            pltpu.VMEM((SUBLANES, blk), F32),
            pltpu.VMEM((ATT_HEADS, SUBLANES, blk), F32),
            pltpu.VMEM((ATT_HEADS, SUBLANES, blk), F32),
            pltpu.VMEM((D_ATT, blk), F32),
        ],
        compiler_params=_cparams("arbitrary", "arbitrary"),
        name="dsa_attention",
    )(qn, qi, proj, kn, vt, ki, bias)


def _outproj_kernel(x_ref, yssd_ref, yatt_ref, scb_ref, scc_ref, sch_ref, hc_ref, hh_ref,
                    cw_ref, w_ref, o_ref, cat_ref, *, blocks_per_seq):
    i = pl.program_id(0)

    @pl.when(pl.program_id(1) == 0)
    def _():
        prev = jnp.where(i % blocks_per_seq == 0, 0.0, hc_ref[...] * hh_ref[...])
        ysc = scb_ref[...] * _causal_conv(scc_ref[...] * sch_ref[...], prev, cw_ref[...])
        cat_ref[:, :D_SSD] = yssd_ref[...]
        cat_ref[:, D_SSD:D_SSD + D_ATT] = yatt_ref[...]
        cat_ref[:, D_SSD + D_ATT:] = ysc.astype(BF16)

    o_ref[...] = x_ref[...] + _dot(cat_ref[...], w_ref[...])


def _outproj(x, y_ssd, y_att, proj, sc_w, w_out, layer, *, seq, tm, tn):
    n, d = x.shape
    halo = tm // SUBLANES
    blk = lambda col: pl.BlockSpec((tm, D_SC), lambda i, j: (i, col // D_SC))
    hblk = lambda col: pl.BlockSpec(
        (SUBLANES, D_SC), lambda i, j: (jnp.maximum(i * halo - 1, 0), col // D_SC))
    return pl.pallas_call(
        functools.partial(_outproj_kernel, blocks_per_seq=seq // tm),
        out_shape=jax.ShapeDtypeStruct((n, d), F32),
        grid=(n // tm, d // tn),
        in_specs=[
            pl.BlockSpec((tm, tn), lambda i, j: (i, j)),
            pl.BlockSpec((tm, D_SSD), lambda i, j: (i, 0)),
            pl.BlockSpec((tm, D_ATT), lambda i, j: (i, 0)),
            blk(COL_SCB), blk(COL_SCC), blk(COL_SCH), hblk(COL_SCC), hblk(COL_SCH),
            pl.BlockSpec((SC_CONV, D_SC), lambda i, j: (0, 0)),
            pl.BlockSpec((None, d, tn), lambda i, j: (layer, 0, j)),
        ],
        out_specs=pl.BlockSpec((tm, tn), lambda i, j: (i, j)),
        scratch_shapes=[pltpu.VMEM((tm, d), BF16)],
        compiler_params=_cparams("arbitrary", "arbitrary"),
        name="out_proj",
    )(x, y_ssd, y_att, proj, proj, proj, proj, proj, sc_w, w_out)


def _ffn_kernel(x_ref, g_ref, wg_ref, wu_ref, cw_ref, wd_ref, o_ref, h_ref, acc_ref, halo_ref,
                *, blocks_per_seq):
    i = pl.program_id(0)
    j = pl.program_id(1)
    last = pl.num_programs(1) - 1
    tm = x_ref.shape[0]

    @pl.when(i % blocks_per_seq == 0)
    def _():
        halo_ref[j] = jnp.zeros(halo_ref.shape[1:], F32)

    def step(first, final):
        if first:
            x = x_ref[...]
            ms = jnp.mean(x * x, axis=-1, keepdims=True)
            h = (x * lax.rsqrt(ms + NORM_EPS) * g_ref[...]).astype(BF16)
            h_ref[...] = h
        else:
            h = h_ref[...]
        gate = _dot(h, wg_ref[...])
        up = _dot(h, wu_ref[...])
        prev = halo_ref[j]
        halo_ref[j] = gate[tm - SUBLANES:, :]
        gc = _causal_conv(gate, prev, cw_ref[...])
        act = (gc * _sigmoid(gc) * up).astype(BF16)
        down = _dot(act, wd_ref[...])
        if first:
            acc_ref[...] = down
        elif final:
            o_ref[...] = x_ref[...] + (acc_ref[...] + down)
        else:
            acc_ref[...] += down

    @pl.when(j == 0)
    def _():
        step(True, False)

    @pl.when(jnp.logical_and(j > 0, j < last))
    def _():
        step(False, False)

    @pl.when(j == last)
    def _():
        step(False, True)


def _ffn(x, g, wg, wu, cw, wd, layer, *, seq, tm, tf):
    n, d = x.shape
    dff = wg.shape[2]
    assert dff // tf >= 2, "the kernel has distinct first and last column steps"
    return pl.pallas_call(
        functools.partial(_ffn_kernel, blocks_per_seq=seq // tm),
        out_shape=jax.ShapeDtypeStruct((n, d), F32),
        grid=(n // tm, dff // tf),
        in_specs=[
            pl.BlockSpec((tm, d), lambda i, j: (i, 0)),
            pl.BlockSpec((1, d), lambda i, j: (0, 0)),
            pl.BlockSpec((None, d, tf), lambda i, j: (layer, 0, j)),
            pl.BlockSpec((None, d, tf), lambda i, j: (layer, 0, j)),
            pl.BlockSpec((FFN_CONV, tf), lambda i, j: (0, j)),
            pl.BlockSpec((None, tf, d), lambda i, j: (layer, j, 0)),
        ],
        out_specs=pl.BlockSpec((tm, d), lambda i, j: (i, 0)),
        scratch_shapes=[
            pltpu.VMEM((tm, d), BF16),
            pltpu.VMEM((tm, d), F32),
            pltpu.VMEM((dff // tf, SUBLANES, tf), F32),
        ],
        compiler_params=_cparams("arbitrary", "arbitrary"),
        name="conv_gated_mlp",
    )(x, g, wg, wu, cw, wd)


_IN_SIZES = (D_SSD, SSD_CONV_DIM, SSD_HEADS, D_ATT, D_ATT, D_ATT,
             IDX_HEADS * IDX_DIM, IDX_DIM, IDX_HEADS, D_SC, D_SC, D_SC)
_IN_OFFS = tuple(sum(_IN_SIZES[:t]) for t in range(len(_IN_SIZES) + 1))
D_IN_PROJ = _IN_OFFS[-1]
_WIDE_SEGMENTS = ((COL_XBC, 1), (COL_Q, 3), (COL_Z, 0), (COL_K, 4), (COL_V, 5), (COL_QIDX, 6),
                  (COL_SCB, 9), (COL_SCC, 10), (COL_SCH, 11))


def _pack_w_in_kernel(w_ref, o_ref):
    tc = o_ref.shape[1]
    pack = 16

    def rows(seg):
        return w_ref[_IN_OFFS[seg]:_IN_OFFS[seg + 1], :]

    def put(dst, val, total):
        n = val.shape[0]
        pad = (-n) % pack
        if pad:
            val = jnp.concatenate([val, jnp.zeros((pad, tc), F32)], axis=0)
        o_ref[dst:dst + n + pad, :] = val.astype(BF16)
        if total > n + pad:
            o_ref[dst + n + pad:dst + total, :] = jnp.zeros((total - n - pad, tc), BF16)

    for dst, seg in _WIDE_SEGMENTS:
        put(dst, rows(seg), _IN_SIZES[seg])
    put(COL_DT, rows(2), LANES)
    put(COL_KIDX, jnp.concatenate([rows(7)] * (LANES // IDX_DIM), axis=0), LANES)
    put(COL_WIDX, rows(8), D_PACKED - COL_WIDX)


def _pack_w_in(w_in, *, tc=256):
    depth, d, _ = w_in.shape
    w_t = jnp.swapaxes(w_in, 1, 2)
    return pl.pallas_call(
        _pack_w_in_kernel,
        out_shape=jax.ShapeDtypeStruct((depth, D_PACKED, d), BF16),
        grid=(depth, d // tc),
        in_specs=[pl.BlockSpec((None, D_IN_PROJ, tc), lambda l, i: (l, 0, i))],
        out_specs=pl.BlockSpec((None, D_PACKED, tc), lambda l, i: (l, 0, i)),
        compiler_params=_cparams("arbitrary", "arbitrary"),
        name="pack_w_in",
    )(w_t)


def _forward(x, norm_mix, w_in, ssd_conv_w, ssd_conv_b, ssd_dt_bias, ssd_a_log, ssd_d, ssd_norm,
             att_q_norm, att_k_norm, rel_bias, sc_conv_w, w_out, norm_ffn,
             ffn_w_gate, ffn_w_up, ffn_conv_w, ffn_w_down, *, tm_proj, tn_proj, tm_out, tn_out,
             tm_ffn, tf_ffn, tm_prep, att_blk):
    bsz, seq, d = x.shape
    depth = w_in.shape[0]
    n = bsz * seq
    xf = x.reshape(n, d)
    w_in_p = _pack_w_in(w_in)
    w_out_b = w_out.astype(BF16)
    wg_b = ffn_w_gate.astype(BF16)
    wu_b = ffn_w_up.astype(BF16)
    wd_b = ffn_w_down.astype(BF16)
    bias = _bias_tiles(rel_bias, att_blk)
    for l in range(depth):
        proj = _rms_matmul(xf, norm_mix[l].reshape(1, d), w_in_p, l, tm=tm_proj, tn=tn_proj)
        y_ssd = _ssd(proj, ssd_conv_w[l], ssd_conv_b[l], ssd_dt_bias[l], ssd_a_log[l], ssd_d[l],
                     ssd_norm[l], bsz=bsz, seq=seq)
        qn, kn, vt, qi, ki = _dsa_prep(proj, att_q_norm[l], att_k_norm[l], tm=tm_prep, blk=att_blk)
        y_att = _dsa(qn, qi, proj, kn, vt, ki, bias, bsz=bsz, seq=seq, blk=att_blk)
        xf = _outproj(xf, y_ssd, y_att, proj, sc_conv_w[l], w_out_b, l, seq=seq, tm=tm_out, tn=tn_out)
        xf = _ffn(xf, norm_ffn[l].reshape(1, d), wg_b, wu_b, ffn_conv_w[l], wd_b, l,
                  seq=seq, tm=tm_ffn, tf=tf_ffn)
    return xf.reshape(bsz, seq, d)


def kernel(x, norm_mix, w_in, ssd_conv_w, ssd_conv_b, ssd_dt_bias, ssd_a_log, ssd_d, ssd_norm, att_q_norm, att_k_norm, rel_bias, sc_conv_w, w_out, norm_ffn, ffn_w_gate, ffn_w_up, ffn_conv_w, ffn_w_down):
    return _forward(x, norm_mix, w_in, ssd_conv_w, ssd_conv_b, ssd_dt_bias, ssd_a_log, ssd_d, ssd_norm,
                    att_q_norm, att_k_norm, rel_bias, sc_conv_w, w_out, norm_ffn,
                    ffn_w_gate, ffn_w_up, ffn_conv_w, ffn_w_down,
                    tm_proj=1024, tn_proj=512, tm_out=512, tn_out=2048,
                    tm_ffn=512, tf_ffn=512, tm_prep=1024, att_blk=256)
```

```python
import functools
import math

import jax
import jax.numpy as jnp
from jax import lax
from jax.experimental import pallas as pl
from jax.experimental.pallas import tpu as pltpu

F32 = jnp.float32
BF16 = jnp.bfloat16
HIGHEST = lax.Precision.HIGHEST

LANES = 128
SUBLANES = 8
VMEM_LIMIT_BYTES = 56 * 1024 * 1024

D_MODEL = 2048
D_SSD = 1024
D_ATT = 512
D_SC = 512
SSD_HEAD_DIM = 64
SSD_HEADS = 16
SSD_GROUPS = 2
SSD_STATE = 128
SSD_CONV = 4
SSD_CHUNK = 128
SSD_CONV_DIM = D_SSD + 2 * SSD_GROUPS * SSD_STATE
ATT_HEAD_DIM = 64
ATT_HEADS = 8
IDX_HEADS = 8
IDX_DIM = 64
TOPK_MAX = 256
REL_BUCKETS = 32
REL_MAX_DIST = 128
SC_CONV = 3
D_FF = 5632
FFN_CONV = 3
NORM_EPS = 1e-6

COL_XBC = 0
COL_Q = 1536
COL_Z = 2048
COL_K = 3072
COL_V = 3584
COL_QIDX = 4096
COL_SCB = 4608
COL_SCC = 5120
COL_SCH = 5632
COL_DT = 6144
COL_KIDX = 6272
COL_WIDX = 6400
D_PACKED = 6656

NEG_BIG = -1e30
LOG2E = math.log2(math.e)


def _cparams(*sem):
    return pltpu.CompilerParams(dimension_semantics=sem, vmem_limit_bytes=VMEM_LIMIT_BYTES)


def _dot(a, b):
    return jnp.dot(a, b, preferred_element_type=F32)


def _dot_nt(a, b):
    return lax.dot_general(a, b, (((1,), (1,)), ((), ())), preferred_element_type=F32)


def _dot_exact(a, b):
    return jnp.dot(a, b, preferred_element_type=F32, precision=HIGHEST)


def _expand_exact(x, onehot3):
    x1 = x.astype(BF16)
    r1 = x - x1.astype(F32)
    x2 = r1.astype(BF16)
    x3 = (r1 - x2.astype(F32)).astype(BF16)
    return _dot(jnp.concatenate([x1, x2, x3], axis=1), onehot3)


def _sigmoid(x):
    return 1.0 / (1.0 + jnp.exp(-x))


def _shift_rows(x, prev8, s):
    xr = pltpu.roll(x, s, 0)
    pr = pltpu.roll(prev8, s, 0)
    rows = lax.broadcasted_iota(jnp.int32, (SUBLANES, x.shape[1]), 0)
    top = jnp.where(rows < s, pr, xr[:SUBLANES])
    return jnp.concatenate([top, xr[SUBLANES:]], axis=0)


def _causal_conv(x, prev8, w):
    k = w.shape[0]
    y = w[k - 1:k] * x
    for s in range(1, k):
        y = y + w[k - 1 - s:k - s] * _shift_rows(x, prev8, s)
    return y


def _rms_matmul_kernel(x_ref, g_ref, w_ref, o_ref, h_ref):
    @pl.when(pl.program_id(1) == 0)
    def _():
        x = x_ref[...]
        ms = jnp.mean(x * x, axis=-1, keepdims=True)
        h = (x * lax.rsqrt(ms + NORM_EPS) * g_ref[...]).astype(BF16)
        h_ref[...] = h
        o_ref[...] = _dot_nt(h, w_ref[...])

    @pl.when(pl.program_id(1) > 0)
    def _():
        o_ref[...] = _dot_nt(h_ref[...], w_ref[...])


def _rms_matmul(x, g, w_t, layer, *, tm, tn):
    n, d = x.shape
    dout = w_t.shape[1]
    return pl.pallas_call(
        _rms_matmul_kernel,
        out_shape=jax.ShapeDtypeStruct((n, dout), F32),
        grid=(n // tm, dout // tn),
        in_specs=[
            pl.BlockSpec((tm, d), lambda i, j: (i, 0)),
            pl.BlockSpec((1, d), lambda i, j: (0, 0)),
            pl.BlockSpec((None, tn, d), lambda i, j: (layer, j, 0)),
        ],
        out_specs=pl.BlockSpec((tm, tn), lambda i, j: (i, j)),
        scratch_shapes=[pltpu.VMEM((tm, d), BF16)],
        compiler_params=_cparams("arbitrary", "arbitrary"),
        name="rms_in_proj",
    )(x, g, w_t)


def _ssd_kernel(xbc_ref, halo_ref, z_ref, dt_ref, cw_ref, cb_ref, dtb_ref, alog_ref,
                dx_ref, ng_ref, ex_ref, o_ref, h_ref, y_ref):
    c = pl.program_id(1)
    L = SSD_CHUNK

    @pl.when(c == 0)
    def _():
        h_ref[...] = jnp.zeros(h_ref.shape, F32)

    prev = jnp.where(c == 0, 0.0, halo_ref[...])
    xbc = _causal_conv(xbc_ref[...], prev, cw_ref[...]) + cb_ref[...]
    xbc = xbc * _sigmoid(xbc)
    xs = xbc[:, :D_SSD]

    row = lax.broadcasted_iota(jnp.int32, (L, L), 0)
    col = lax.broadcasted_iota(jnp.int32, (L, L), 1)
    tril = (row >= col).astype(F32)

    def softplus(v):
        return jnp.maximum(v, 0.0) + jnp.log1p(jnp.exp(-jnp.abs(v)))

    dt_s = softplus(dt_ref[...] + dtb_ref[...])
    a_s = dt_s * (-jnp.exp(alog_ref[...]))
    acs_s = _dot_exact(tril, a_s)
    acs_t = acs_s.T
    ex = ex_ref[...]
    dt_x = _expand_exact(dt_s, ex)
    e_x = _expand_exact(jnp.exp(acs_s), ex)
    de_x = _expand_exact(jnp.exp(acs_s[L - 1:L, :] - acs_s), ex)
    decay = e_x[L - 1:L, :]
    xdt = xs * dt_x
    xdt_end = (xdt * de_x).astype(BF16)
    xdt = xdt.astype(BF16)

    lane = lax.broadcasted_iota(jnp.int32, (L, LANES), 1)
    lo = lane < SSD_HEAD_DIM
    lower = row >= col

    for g in range(SSD_GROUPS):
        bm = xbc[:, D_SSD + g * SSD_STATE:D_SSD + (g + 1) * SSD_STATE].astype(BF16)
        cm = xbc[:, D_SSD + (SSD_GROUPS + g) * SSD_STATE:
                 D_SSD + (SSD_GROUPS + g + 1) * SSD_STATE].astype(BF16)
        cb = _dot_nt(cm, bm)
        bm_t = bm.T
        for q in range(4):
            pair = g * 4 + q
            sl = slice(pair * LANES, (pair + 1) * LANES)
            gs = []
            for hh in range(2):
                h = 2 * pair + hh
                seg = (jnp.broadcast_to(acs_s[:, h:h + 1], (L, L))
                       - jnp.broadcast_to(acs_t[h:h + 1, :], (L, L)))
                lm = jnp.where(lower, jnp.exp(seg), 0.0)
                gs.append((cb * lm).astype(BF16))
            gpair = jnp.concatenate(gs, axis=1)
            xp = xdt[:, sl]
            zero = jnp.zeros_like(xp)
            x2 = jnp.concatenate([jnp.where(lo, xp, zero), jnp.where(lo, zero, xp)], axis=0)
            y = _dot(gpair, x2)
            hprev = h_ref[pair]
            y = y + e_x[:, sl] * _dot(cm, hprev.astype(BF16))
            y = y + xs[:, sl] * dx_ref[:, sl]
            h_ref[pair] = hprev * decay[:, sl] + _dot(bm_t, xdt_end[:, sl])
            y_ref[:, sl] = y

    z = z_ref[...]
    y = y_ref[...] * (z * _sigmoid(z))
    half = D_SSD // SSD_GROUPS
    outs = []
    for g in range(SSD_GROUPS):
        yg = y[:, g * half:(g + 1) * half]
        ms = jnp.mean(yg * yg, axis=-1, keepdims=True)
        outs.append(yg * lax.rsqrt(ms + NORM_EPS))
    o_ref[...] = (jnp.concatenate(outs, axis=1) * ng_ref[...]).astype(BF16)


def _ssd(proj, conv_w, conv_b, dt_bias, a_log, d_skip, norm_g, *, bsz, seq):
    nc = seq // SSD_CHUNK
    n = bsz * seq
    pad = LANES - SSD_HEADS
    dtb = jnp.pad(dt_bias, (0, pad)).reshape(1, LANES)
    alog = jnp.pad(a_log, (0, pad), constant_values=NEG_BIG).reshape(1, LANES)
    rep = lambda v: jnp.repeat(v, SSD_HEAD_DIM).reshape(1, D_SSD)
    expand = (jnp.arange(LANES)[:, None] == (jnp.arange(D_SSD)[None, :] // SSD_HEAD_DIM)).astype(BF16)
    expand = jnp.concatenate([expand] * 3, axis=0)
    rows_per_halo = SSD_CHUNK // SUBLANES
    const = lambda shape: pl.BlockSpec(shape, lambda b, c: (0,) * len(shape))
    return pl.pallas_call(
        _ssd_kernel,
        out_shape=jax.ShapeDtypeStruct((n, D_SSD), BF16),
        grid=(bsz, nc),
        in_specs=[
            pl.BlockSpec((SSD_CHUNK, SSD_CONV_DIM), lambda b, c: (b * nc + c, COL_XBC // SSD_CONV_DIM)),
            pl.BlockSpec((SUBLANES, SSD_CONV_DIM),
                         lambda b, c: (jnp.maximum((b * nc + c) * rows_per_halo - 1, 0), 0)),
            pl.BlockSpec((SSD_CHUNK, D_SSD), lambda b, c: (b * nc + c, COL_Z // D_SSD)),
            pl.BlockSpec((SSD_CHUNK, LANES), lambda b, c: (b * nc + c, COL_DT // LANES)),
            const((SSD_CONV, SSD_CONV_DIM)),
            const((1, SSD_CONV_DIM)),
            const((1, LANES)),
            const((1, LANES)),
            const((1, D_SSD)),
            const((1, D_SSD)),
            const((3 * LANES, D_SSD)),
        ],
        out_specs=pl.BlockSpec((SSD_CHUNK, D_SSD), lambda b, c: (b * nc + c, 0)),
        scratch_shapes=[pltpu.VMEM((SSD_HEADS // 2, SSD_STATE, LANES), F32),
                        pltpu.VMEM((SSD_CHUNK, D_SSD), F32)],
        compiler_params=_cparams("arbitrary", "arbitrary"),
        name="ssd_scan",
    )(proj, proj, proj, proj, conv_w, conv_b.reshape(1, -1), dtb, alog,
      rep(d_skip), norm_g.reshape(1, -1), expand)


def _dsa_prep_kernel(q_ref, k_ref, v_ref, qi_ref, ki_ref, qg_ref, kg_ref, seg_ref,
                     qn_ref, kn_ref, vt_ref, qib_ref, kib_ref, *, blk):
    def head_norm(x, g):
        ms = _expand_exact(x * x, seg_ref[...]) * (1.0 / ATT_HEAD_DIM)
        return x * lax.rsqrt(ms + NORM_EPS) * g

    qn_ref[...] = (head_norm(q_ref[...], qg_ref[...]) * (LOG2E * ATT_HEAD_DIM ** -0.5)).astype(BF16)
    kn_ref[...] = head_norm(k_ref[...], kg_ref[...]).astype(BF16)
    for c in range(vt_ref.shape[0]):
        vt_ref[c] = v_ref[c * blk:(c + 1) * blk, :].T.astype(BF16)
    qib_ref[...] = qi_ref[...].astype(BF16)
    kib_ref[...] = ki_ref[...].astype(BF16)


def _dsa_prep(proj, q_norm, k_norm, *, tm, blk):
    n = proj.shape[0]
    seg = (jnp.arange(D_ATT)[:, None] // ATT_HEAD_DIM
           == jnp.arange(D_ATT)[None, :] // ATT_HEAD_DIM).astype(BF16)
    seg = jnp.concatenate([seg] * 3, axis=0)
    tile = lambda v: jnp.tile(v, ATT_HEADS).reshape(1, D_ATT)
    col_blk = lambda col, w: pl.BlockSpec((tm, w), lambda i: (i, col // w))
    const = lambda shape: pl.BlockSpec(shape, lambda i: (0,) * len(shape))
    return pl.pallas_call(
        functools.partial(_dsa_prep_kernel, blk=blk),
        out_shape=(
            jax.ShapeDtypeStruct((n, D_ATT), BF16),
            jax.ShapeDtypeStruct((n, D_ATT), BF16),
            jax.ShapeDtypeStruct((n // blk, D_ATT, blk), BF16),
            jax.ShapeDtypeStruct((n, D_ATT), BF16),
            jax.ShapeDtypeStruct((n, LANES), BF16),
        ),
        grid=(n // tm,),
        in_specs=[col_blk(COL_Q, D_ATT), col_blk(COL_K, D_ATT), col_blk(COL_V, D_ATT),
                  col_blk(COL_QIDX, D_ATT), col_blk(COL_KIDX, LANES),
                  const((1, D_ATT)), const((1, D_ATT)), const((3 * D_ATT, D_ATT))],
        out_specs=(
            pl.BlockSpec((tm, D_ATT), lambda i: (i, 0)),
            pl.BlockSpec((tm, D_ATT), lambda i: (i, 0)),
            pl.BlockSpec((tm // blk, D_ATT, blk), lambda i: (i, 0, 0)),
            pl.BlockSpec((tm, D_ATT), lambda i: (i, 0)),
            pl.BlockSpec((tm, LANES), lambda i: (i, 0)),
        ),
        compiler_params=_cparams("arbitrary"),
        name="dsa_prep",
    )(proj, proj, proj, proj, proj, tile(q_norm), tile(k_norm), seg)


def _dsa_kernel(qn_ref, qi_ref, w_ref, kn_ref, vt_ref, ki_ref, bias_ref, o_ref,
                key_ref, hi_ref, lo_ref, big_ref, qm_ref, qim_ref, wt_ref, mb_ref, carry_ref,
                m_ref, l_ref, acc_ref, *, blk, topk):
    i = pl.program_id(1)
    pack = 16
    lane = lax.broadcasted_iota(jnp.int32, (blk, LANES), 1)
    lo_lanes = lane < ATT_HEAD_DIM
    krow = lax.broadcasted_iota(jnp.int32, (blk, blk), 0)
    qcol = lax.broadcasted_iota(jnp.int32, (blk, blk), 1)
    future = krow > qcol

    for p in range(ATT_HEADS // 2):
        sl = slice(p * LANES, (p + 1) * LANES)
        qp = qn_ref[:, sl]
        qip = qi_ref[:, sl]
        zero = jnp.zeros_like(qp)
        qm_ref[2 * p * blk:(2 * p + 1) * blk] = jnp.where(lo_lanes, qp, zero)
        qm_ref[(2 * p + 1) * blk:(2 * p + 2) * blk] = jnp.where(lo_lanes, zero, qp)
        qim_ref[2 * p * blk:(2 * p + 1) * blk] = jnp.where(lo_lanes, qip, zero)
        qim_ref[(2 * p + 1) * blk:(2 * p + 2) * blk] = jnp.where(lo_lanes, zero, qip)
    wt_ref[...] = w_ref[...].T

    def score_dots(j, slot):
        kk = ki_ref[pl.ds(pl.multiple_of(j * blk, blk), blk), :]
        big_ref[slot] = _dot_nt(kk, qim_ref[...])

    def score_finish(j, slot, diag):
        s = jnp.zeros((blk, blk), F32)
        for h in range(IDX_HEADS):
            s = s + jnp.maximum(big_ref[slot, :, h * blk:(h + 1) * blk], 0.0) * wt_ref[h:h + 1, :]
        if diag:
            s = jnp.where(future, -jnp.inf, s)
        s = jnp.where(s == 0.0, 0.0, s)
        bits = pltpu.bitcast(s, jnp.int32)
        key = bits ^ ((bits >> 31) & 0x7FFFFFFF)
        key_ref[j] = key
        hi_ref[j] = (key >> 16).astype(jnp.int16)
        lo_ref[j] = ((key & 0xFFFF) - 32768).astype(jnp.int16)

    def by_parity(x, fn):
        @pl.when(x % 2 == 0)
        def _():
            fn(0)

        @pl.when(x % 2 == 1)
        def _():
            fn(1)

    score_dots(0, 0)

    def score_block(base, count):
        for u in range(count):
            score_dots(base + u + 1, (u + 1) % 2)
            score_finish(base + u, u % 2, False)

    def score_oct(t, carry):
        score_block(8 * t, 8)
        return carry

    lax.fori_loop(0, i // 8, score_oct, 0)

    @pl.when(i % 8 >= 4)
    def _():
        score_block(8 * (i // 8), 4)

    @pl.when(i % 4 >= 2)
    def _():
        score_block(4 * (i // 4), 2)

    @pl.when(i % 2 == 1)
    def _():
        score_block(i - 1, 1)

    by_parity(i, lambda slot: score_finish(i, slot, True))

    one16 = jnp.ones((), jnp.int16)
    zero16 = jnp.zeros((), jnp.int16)

    def count_ge(src_ref, cand):
        cand_b = jnp.broadcast_to(cand.astype(jnp.int16), (blk, blk))

        def add_tile(j, cnts, on):
            ind = jnp.where(src_ref[j] >= cand_b, on, zero16)
            cnts = list(cnts)
            for g in range(blk // pack):
                cnts[g % len(cnts)] = cnts[g % len(cnts)] + ind[g * pack:(g + 1) * pack]
            return tuple(cnts)

        def body(t, cnts):
            return add_tile(2 * t + 1, add_tile(2 * t, cnts, one16), one16)

        zeros = tuple(jnp.zeros((pack, blk), jnp.int16) for _ in range(4))
        cnts = lax.fori_loop(0, (i + 1) // 2, body, zeros)
        cnts = add_tile(i, cnts, ((i + 1) % 2).astype(jnp.int16))
        cnt = (cnts[0] + cnts[1]) + (cnts[2] + cnts[3])
        return jnp.sum(cnt.astype(jnp.int32).astype(F32), axis=0, keepdims=True)

    def bisect16(src_ref, rank, cnt_all):
        def bit_body(b, st):
            prefix, c_acc, c_rej = st
            cand = prefix + lax.shift_left(jnp.int32(1), 15 - b)
            cnt = count_ge(src_ref, cand)
            ok = cnt >= rank
            return (jnp.where(ok, cand, prefix), jnp.where(ok, cnt, c_acc), jnp.where(ok, c_rej, cnt))
        init = (jnp.full((1, blk), -32768, jnp.int32), cnt_all, jnp.zeros((1, blk), F32))
        return lax.fori_loop(0, 16, bit_body, init)

    ncols = jnp.full((1, blk), ((i + 1) * blk).astype(F32), F32)
    p_hi, c_acc1, c_rej1 = bisect16(hi_ref, jnp.float32(topk), ncols)
    p_hi_b = jnp.broadcast_to(p_hi.astype(jnp.int16), (blk, blk))

    def group_body(j, carry):
        lo_ref[j] = jnp.where(hi_ref[j] == p_hi_b, lo_ref[j], jnp.int16(-32768))
        return carry

    lax.fori_loop(0, i + 1, group_body, 0)
    rank2 = jnp.float32(topk) - c_rej1
    p_lo, c_acc2, c_rej2 = bisect16(lo_ref, rank2, c_acc1 - c_rej1)
    thr = lax.shift_left(p_hi, 16) + (p_lo + 32768)
    need = rank2 - c_rej2
    has_ties = jnp.max((c_acc2 - c_rej2) - need) > 0.5

    m_ref[...] = jnp.full(m_ref.shape, NEG_BIG, F32)
    l_ref[...] = jnp.zeros(l_ref.shape, F32)
    acc_ref[...] = jnp.zeros(acc_ref.shape, F32)
    carry_ref[...] = jnp.zeros(carry_ref.shape, F32)

    def qk_dots(j, slot, mask_slot):
        start = pl.multiple_of(j * blk, blk)
        for p in range(ATT_HEADS // 2):
            mb2 = jnp.concatenate([mb_ref[mask_slot], mb_ref[mask_slot]], axis=1)
            big_ref[slot, :, 2 * p * blk:(2 * p + 2) * blk] = _dot_nt(
                kn_ref[pl.ds(start, blk), p * LANES:(p + 1) * LANES], qm_ref[2 * p * blk:(2 * p + 2) * blk]) + mb2

    def tile_mask(j, kind, slot):
        kt = key_ref[j]

        @pl.when(jnp.logical_not(has_ties))
        def _():
            mb = jnp.where(kt >= thr, 0.0, NEG_BIG)
            mb_ref[slot] = jnp.where(future, NEG_BIG, mb) if kind == 2 else mb

        @pl.when(has_ties)
        def _():
            eq = jnp.where(kt == thr, 1.0, 0.0)
            lower = jnp.where(krow >= qcol, 1.0, 0.0).astype(BF16)
            seen = _dot(lower, eq.astype(BF16)) + carry_ref[0:1, :]
            keep = jnp.where(seen <= need, 0.0, NEG_BIG)
            mb = jnp.where(kt > thr, 0.0, jnp.where(kt == thr, keep, NEG_BIG))
            mb_ref[slot] = jnp.where(future, NEG_BIG, mb) if kind == 2 else mb
            carry_ref[0:1, :] += jnp.sum(eq, axis=0, keepdims=True)

    ones_rows = jnp.ones((pack, blk), BF16)

    def softmax_pv(j, slot, kind):
        vt = vt_ref[j]
        for h in range(ATT_HEADS):
            half = blk // 2
            s = big_ref[slot, :, h * blk:(h + 1) * blk]
            if kind > 0:
                s = s + bias_ref[kind - 1, h]
            m_prev = m_ref[h]
            m_new = jnp.maximum(m_prev, jnp.max(s, axis=0, keepdims=True))
            alpha = jnp.exp2(m_prev - m_new)
            if kind > 0:
                pexp = jnp.exp2(s - m_new[0:1, :]).astype(BF16)
            else:
                pexp = jnp.concatenate(
                    [jnp.exp2(big_ref[slot, r * half:(r + 1) * half, h * blk:(h + 1) * blk]
                              - m_new[0:1, :]).astype(BF16) for r in range(2)], axis=0)
            m_ref[h] = m_new
            ch = slice(h * ATT_HEAD_DIM, (h + 1) * ATT_HEAD_DIM)
            pv = _dot(jnp.concatenate([vt[ch, :], ones_rows], axis=0), pexp)
            l_ref[h] = alpha * l_ref[h] + pv[ATT_HEAD_DIM:ATT_HEAD_DIM + 1, :]
            acc_ref[ch, :] = alpha[0:1, :] * acc_ref[ch, :] + pv[:ATT_HEAD_DIM, :]

    n_far = i - 1

    @pl.when(i == 0)
    def _():
        tile_mask(0, 2, 0)

    @pl.when(i > 0)
    def _():
        tile_mask(0, 0, 0)

    qk_dots(0, 0, 0)

    def attn_block(base, count):
        for u in range(count):
            tile_mask(base + u + 1, 0, (u + 1) % 4)
        for u in range(count):
            qk_dots(base + u + 1, (u + 1) % 2, (u + 1) % 4)
            softmax_pv(base + u, u % 2, 0)

    def attn_quad(t, carry):
        attn_block(4 * t, 4)
        return carry

    lax.fori_loop(0, n_far // 4, attn_quad, 0)

    @pl.when(jnp.logical_and(n_far > 0, n_far % 4 >= 2))
    def _():
        attn_block(4 * (n_far // 4), 2)

    @pl.when(jnp.logical_and(n_far > 0, n_far % 2 == 1))
    def _():
        attn_block(n_far - 1, 1)

    def near_tiles(slot):
        tile_mask(i, 2, 1 - slot)
        qk_dots(i, 1 - slot, 1 - slot)
        softmax_pv(i - 1, slot, 1)
        softmax_pv(i, 1 - slot, 2)

    @pl.when(i > 0)
    def _():
        by_parity(i - 1, near_tiles)

    @pl.when(i == 0)
    def _():
        softmax_pv(0, 0, 2)

    outs = []
    for h in range(ATT_HEADS):
        ch = slice(h * ATT_HEAD_DIM, (h + 1) * ATT_HEAD_DIM)
        outs.append(acc_ref[ch, :] / l_ref[h][0:1, :])
    o_ref[...] = jnp.concatenate(outs, axis=0).T.astype(BF16)


def _t5_bucket(dist):
    n = jnp.maximum(dist, 0)
    max_exact = REL_BUCKETS // 2
    large = max_exact + (jnp.log(jnp.maximum(n, max_exact).astype(F32) / max_exact)
                         / math.log(REL_MAX_DIST / max_exact)
                         * (REL_BUCKETS - max_exact)).astype(jnp.int32)
    large = jnp.minimum(large, REL_BUCKETS - 1)
    return jnp.where(n < max_exact, n, large)


def _bias_tiles(rel_bias, blk):
    width = 2 * blk
    f = ((rel_bias[_t5_bucket(jnp.arange(width, dtype=jnp.int32))] - rel_bias[REL_BUCKETS - 1]) * LOG2E).T
    h_sub = jnp.concatenate([f[:, blk:], f[:, :blk]], axis=1)
    h = jnp.stack([h_sub, f])
    skew = jnp.tile(h, (1, 1, blk))[:, :, :blk * (width - 1)].reshape(2, ATT_HEADS, blk, width - 1)
    return skew[:, :, :, :blk]


def _dsa(qn, qi, proj, kn, vt, ki, bias, *, bsz, seq, blk):
    n = bsz * seq
    nb = seq // blk
    topk = min(TOPK_MAX, seq // 4)
    once = pl.Buffered(1)
    return pl.pallas_call(
        functools.partial(_dsa_kernel, blk=blk, topk=topk),
        out_shape=jax.ShapeDtypeStruct((n, D_ATT), BF16),
        grid=(bsz, nb),
        in_specs=[
            pl.BlockSpec((blk, D_ATT), lambda b, i: (b * nb + i, 0)),
            pl.BlockSpec((blk, D_ATT), lambda b, i: (b * nb + i, 0)),
            pl.BlockSpec((blk, LANES), lambda b, i: (b * nb + i, COL_WIDX // LANES)),
            pl.BlockSpec((seq, D_ATT), lambda b, i: (b, 0), pipeline_mode=once),
            pl.BlockSpec((nb, D_ATT, blk), lambda b, i: (b, 0, 0), pipeline_mode=once),
            pl.BlockSpec((seq, LANES), lambda b, i: (b, 0), pipeline_mode=once),
            pl.BlockSpec((2, ATT_HEADS, blk, blk), lambda b, i: (0, 0, 0, 0), pipeline_mode=once),
        ],
        out_specs=pl.BlockSpec((blk, D_ATT), lambda b, i: (b * nb + i, 0)),
        scratch_shapes=[
            pltpu.VMEM((nb, blk, blk), jnp.int32),
            pltpu.VMEM((nb, blk, blk), jnp.int16),
            pltpu.VMEM((nb, blk, blk), jnp.int16),
            pltpu.VMEM((2, blk, ATT_HEADS * blk), F32),
            pltpu.VMEM((ATT_HEADS * blk, LANES), BF16),
            pltpu.VMEM((IDX_HEADS * blk, LANES), BF16),
            pltpu.VMEM((LANES, blk), F32),
            pltpu.VMEM((4, blk, blk), F32),
            pltpu.VMEM((SUBLANES, blk), F32),
            pltpu.VMEM((ATT_HEADS, SUBLANES, blk), F32),
            pltpu.VMEM((ATT_HEADS, SUBLANES, blk), F32),
            pltpu.VMEM((D_ATT, blk), F32),
        ],
        compiler_params=_cparams("arbitrary", "arbitrary"),
        name="dsa_attention",
    )(qn, qi, proj, kn, vt, ki, bias)


def _outproj_kernel(x_ref, yssd_ref, yatt_ref, scb_ref, scc_ref, sch_ref, hc_ref, hh_ref,
                    cw_ref, w_ref, o_ref, *, blocks_per_seq):
    i = pl.program_id(0)
    prev = jnp.where(i % blocks_per_seq == 0, 0.0, hc_ref[...] * hh_ref[...])
    ysc = scb_ref[...] * _causal_conv(scc_ref[...] * sch_ref[...], prev, cw_ref[...])
    y = _dot(yssd_ref[...], w_ref[:D_SSD, :])
    y = y + _dot(yatt_ref[...], w_ref[D_SSD:D_SSD + D_ATT, :])
    y = y + _dot(ysc.astype(BF16), w_ref[D_SSD + D_ATT:, :])
    o_ref[...] = x_ref[...] + y


def _outproj(x, y_ssd, y_att, proj, sc_w, w_out, layer, *, seq, tm):
    n, d = x.shape
    halo = tm // SUBLANES
    blk = lambda col: pl.BlockSpec((tm, D_SC), lambda i: (i, col // D_SC))
    hblk = lambda col: pl.BlockSpec(
        (SUBLANES, D_SC), lambda i: (jnp.maximum(i * halo - 1, 0), col // D_SC))
    return pl.pallas_call(
        functools.partial(_outproj_kernel, blocks_per_seq=seq // tm),
        out_shape=jax.ShapeDtypeStruct((n, d), F32),
        grid=(n // tm,),
        in_specs=[
            pl.BlockSpec((tm, d), lambda i: (i, 0)),
            pl.BlockSpec((tm, D_SSD), lambda i: (i, 0)),
            pl.BlockSpec((tm, D_ATT), lambda i: (i, 0)),
            blk(COL_SCB), blk(COL_SCC), blk(COL_SCH), hblk(COL_SCC), hblk(COL_SCH),
            pl.BlockSpec((SC_CONV, D_SC), lambda i: (0, 0)),
            pl.BlockSpec((None, d, d), lambda i: (layer, 0, 0)),
        ],
        out_specs=pl.BlockSpec((tm, d), lambda i: (i, 0)),
        compiler_params=_cparams("arbitrary"),
        name="out_proj",
    )(x, y_ssd, y_att, proj, proj, proj, proj, proj, sc_w, w_out)


def _ffn_kernel(x_ref, g_ref, wg_ref, wu_ref, cw_ref, wd_ref, o_ref, h_ref, acc_ref, halo_ref,
                *, blocks_per_seq):
    i = pl.program_id(0)
    j = pl.program_id(1)
    last = pl.num_programs(1) - 1
    tm = x_ref.shape[0]

    @pl.when(i % blocks_per_seq == 0)
    def _():
        halo_ref[j] = jnp.zeros(halo_ref.shape[1:], F32)

    def step(first, final):
        if first:
            x = x_ref[...]
            ms = jnp.mean(x * x, axis=-1, keepdims=True)
            h = (x * lax.rsqrt(ms + NORM_EPS) * g_ref[...]).astype(BF16)
            h_ref[...] = h
        else:
            h = h_ref[...]
        gate = _dot(h, wg_ref[...])
        up = _dot(h, wu_ref[...])
        prev = halo_ref[j]
        halo_ref[j] = gate[tm - SUBLANES:, :]
        gc = _causal_conv(gate, prev, cw_ref[...])
        act = (gc * _sigmoid(gc) * up).astype(BF16)
        down = _dot(act, wd_ref[...])
        if first:
            acc_ref[...] = down
        elif final:
            o_ref[...] = x_ref[...] + (acc_ref[...] + down)
        else:
            acc_ref[...] += down

    @pl.when(j == 0)
    def _():
        step(True, False)

    @pl.when(jnp.logical_and(j > 0, j < last))
    def _():
        step(False, False)

    @pl.when(j == last)
    def _():
        step(False, True)


def _ffn(x, g, wg, wu, cw, wd, layer, *, seq, tm, tf):
    n, d = x.shape
    dff = wg.shape[2]
    assert dff // tf >= 2, "the kernel has distinct first and last column steps"
    return pl.pallas_call(
        functools.partial(_ffn_kernel, blocks_per_seq=seq // tm),
        out_shape=jax.ShapeDtypeStruct((n, d), F32),
        grid=(n // tm, dff // tf),
        in_specs=[
            pl.BlockSpec((tm, d), lambda i, j: (i, 0)),
            pl.BlockSpec((1, d), lambda i, j: (0, 0)),
            pl.BlockSpec((None, d, tf), lambda i, j: (layer, 0, j)),
            pl.BlockSpec((None, d, tf), lambda i, j: (layer, 0, j)),
            pl.BlockSpec((FFN_CONV, tf), lambda i, j: (0, j)),
            pl.BlockSpec((None, tf, d), lambda i, j: (layer, j, 0)),
        ],
        out_specs=pl.BlockSpec((tm, d), lambda i, j: (i, 0)),
        scratch_shapes=[
            pltpu.VMEM((tm, d), BF16),
            pltpu.VMEM((tm, d), F32),
            pltpu.VMEM((dff // tf, SUBLANES, tf), F32),
        ],
        compiler_params=_cparams("arbitrary", "arbitrary"),
        name="conv_gated_mlp",
    )(x, g, wg, wu, cw, wd)


_IN_SIZES = (D_SSD, SSD_CONV_DIM, SSD_HEADS, D_ATT, D_ATT, D_ATT,
             IDX_HEADS * IDX_DIM, IDX_DIM, IDX_HEADS, D_SC, D_SC, D_SC)
_IN_OFFS = tuple(sum(_IN_SIZES[:t]) for t in range(len(_IN_SIZES) + 1))
D_IN_PROJ = _IN_OFFS[-1]
_WIDE_SEGMENTS = ((COL_XBC, 1), (COL_Q, 3), (COL_Z, 0), (COL_K, 4), (COL_V, 5), (COL_QIDX, 6),
                  (COL_SCB, 9), (COL_SCC, 10), (COL_SCH, 11))


def _pack_w_in_kernel(w_ref, o_ref):
    tc = o_ref.shape[1]
    pack = 16

    def rows(seg):
        return w_ref[_IN_OFFS[seg]:_IN_OFFS[seg + 1], :]

    def put(dst, val, total):
        n = val.shape[0]
        pad = (-n) % pack
        if pad:
            val = jnp.concatenate([val, jnp.zeros((pad, tc), F32)], axis=0)
        o_ref[dst:dst + n + pad, :] = val.astype(BF16)
        if total > n + pad:
            o_ref[dst + n + pad:dst + total, :] = jnp.zeros((total - n - pad, tc), BF16)

    for dst, seg in _WIDE_SEGMENTS:
        put(dst, rows(seg), _IN_SIZES[seg])
    put(COL_DT, rows(2), LANES)
    put(COL_KIDX, jnp.concatenate([rows(7)] * (LANES // IDX_DIM), axis=0), LANES)
    put(COL_WIDX, rows(8), D_PACKED - COL_WIDX)


def _pack_w_in(w_in, *, tc=256):
    depth, d, _ = w_in.shape
    w_t = jnp.swapaxes(w_in, 1, 2)
    return pl.pallas_call(
        _pack_w_in_kernel,
        out_shape=jax.ShapeDtypeStruct((depth, D_PACKED, d), BF16),
        grid=(depth, d // tc),
        in_specs=[pl.BlockSpec((None, D_IN_PROJ, tc), lambda l, i: (l, 0, i))],
        out_specs=pl.BlockSpec((None, D_PACKED, tc), lambda l, i: (l, 0, i)),
        compiler_params=_cparams("arbitrary", "arbitrary"),
        name="pack_w_in",
    )(w_t)


def _forward(x, norm_mix, w_in, ssd_conv_w, ssd_conv_b, ssd_dt_bias, ssd_a_log, ssd_d, ssd_norm,
             att_q_norm, att_k_norm, rel_bias, sc_conv_w, w_out, norm_ffn,
             ffn_w_gate, ffn_w_up, ffn_conv_w, ffn_w_down, *, tm_proj, tn_proj, tm_out,
             tm_ffn, tf_ffn, tm_prep, att_blk):
    bsz, seq, d = x.shape
    depth = w_in.shape[0]
    n = bsz * seq
    xf = x.reshape(n, d)
    w_in_p = _pack_w_in(w_in)
    w_out_b = w_out.astype(BF16)
    wg_b = ffn_w_gate.astype(BF16)
    wu_b = ffn_w_up.astype(BF16)
    wd_b = ffn_w_down.astype(BF16)
    bias = _bias_tiles(rel_bias, att_blk)
    for l in range(depth):
        proj = _rms_matmul(xf, norm_mix[l].reshape(1, d), w_in_p, l, tm=tm_proj, tn=tn_proj)
        y_ssd = _ssd(proj, ssd_conv_w[l], ssd_conv_b[l], ssd_dt_bias[l], ssd_a_log[l], ssd_d[l],
                     ssd_norm[l], bsz=bsz, seq=seq)
        qn, kn, vt, qi, ki = _dsa_prep(proj, att_q_norm[l], att_k_norm[l], tm=tm_prep, blk=att_blk)
        y_att = _dsa(qn, qi, proj, kn, vt, ki, bias, bsz=bsz, seq=seq, blk=att_blk)
        xf = _outproj(xf, y_ssd, y_att, proj, sc_conv_w[l], w_out_b, l, seq=seq, tm=tm_out)
        xf = _ffn(xf, norm_ffn[l].reshape(1, d), wg_b, wu_b, ffn_conv_w[l], wd_b, l,
                  seq=seq, tm=tm_ffn, tf=tf_ffn)
    return xf.reshape(bsz, seq, d)


def kernel(x, norm_mix, w_in, ssd_conv_w, ssd_conv_b, ssd_dt_bias, ssd_a_log, ssd_d, ssd_norm, att_q_norm, att_k_norm, rel_bias, sc_conv_w, w_out, norm_ffn, ffn_w_gate, ffn_w_up, ffn_conv_w, ffn_w_down):
    return _forward(x, norm_mix, w_in, ssd_conv_w, ssd_conv_b, ssd_dt_bias, ssd_a_log, ssd_d, ssd_norm,
                    att_q_norm, att_k_norm, rel_bias, sc_conv_w, w_out, norm_ffn,
                    ffn_w_gate, ffn_w_up, ffn_conv_w, ffn_w_down,
                    tm_proj=1024, tn_proj=512, tm_out=512,
                    tm_ffn=512, tf_ffn=512, tm_prep=1024, att_blk=256)
```

```python
import functools
import math

import jax
import jax.numpy as jnp
from jax import lax
from jax.experimental import pallas as pl
from jax.experimental.pallas import tpu as pltpu

F32 = jnp.float32
BF16 = jnp.bfloat16
HIGHEST = lax.Precision.HIGHEST

LANES = 128
SUBLANES = 8
VMEM_LIMIT_BYTES = 56 * 1024 * 1024

D_MODEL = 2048
D_SSD = 1024
D_ATT = 512
D_SC = 512
SSD_HEAD_DIM = 64
SSD_HEADS = 16
SSD_GROUPS = 2
SSD_STATE = 128
SSD_CONV = 4
SSD_CHUNK = 128
SSD_CONV_DIM = D_SSD + 2 * SSD_GROUPS * SSD_STATE
ATT_HEAD_DIM = 64
ATT_HEADS = 8
IDX_HEADS = 8
IDX_DIM = 64
TOPK_MAX = 256
REL_BUCKETS = 32
REL_MAX_DIST = 128
SC_CONV = 3
D_FF = 5632
FFN_CONV = 3
NORM_EPS = 1e-6

COL_XBC = 0
COL_Q = 1536
COL_Z = 2048
COL_K = 3072
COL_V = 3584
COL_QIDX = 4096
COL_SCB = 4608
COL_SCC = 5120
COL_SCH = 5632
COL_DT = 6144
COL_KIDX = 6272
COL_WIDX = 6400
D_PACKED = 6656

NEG_BIG = -1e30
LOG2E = math.log2(math.e)


def _cparams(*sem):
    return pltpu.CompilerParams(dimension_semantics=sem, vmem_limit_bytes=VMEM_LIMIT_BYTES)


def _dot(a, b):
    return jnp.dot(a, b, preferred_element_type=F32)


def _dot_nt(a, b):
    return lax.dot_general(a, b, (((1,), (1,)), ((), ())), preferred_element_type=F32)


def _dot_exact(a, b):
    return jnp.dot(a, b, preferred_element_type=F32, precision=HIGHEST)


def _expand_exact(x, onehot3):
    x1 = x.astype(BF16)
    r1 = x - x1.astype(F32)
    x2 = r1.astype(BF16)
    x3 = (r1 - x2.astype(F32)).astype(BF16)
    return _dot(jnp.concatenate([x1, x2, x3], axis=1), onehot3)


def _sigmoid(x):
    return 1.0 / (1.0 + jnp.exp(-x))


def _shift_rows(x, prev8, s):
    xr = pltpu.roll(x, s, 0)
    pr = pltpu.roll(prev8, s, 0)
    rows = lax.broadcasted_iota(jnp.int32, (SUBLANES, x.shape[1]), 0)
    top = jnp.where(rows < s, pr, xr[:SUBLANES])
    return jnp.concatenate([top, xr[SUBLANES:]], axis=0)


def _causal_conv(x, prev8, w):
    k = w.shape[0]
    y = w[k - 1:k] * x
    for s in range(1, k):
        y = y + w[k - 1 - s:k - s] * _shift_rows(x, prev8, s)
    return y


def _rms_matmul_kernel(x_ref, g_ref, w_ref, o_ref, h_ref):
    @pl.when(pl.program_id(1) == 0)
    def _():
        x = x_ref[...]
        ms = jnp.mean(x * x, axis=-1, keepdims=True)
        h = (x * lax.rsqrt(ms + NORM_EPS) * g_ref[...]).astype(BF16)
        h_ref[...] = h
        o_ref[...] = _dot_nt(h, w_ref[...])

    @pl.when(pl.program_id(1) > 0)
    def _():
        o_ref[...] = _dot_nt(h_ref[...], w_ref[...])


def _rms_matmul(x, g, w_t, layer, *, tm, tn):
    n, d = x.shape
    dout = w_t.shape[1]
    return pl.pallas_call(
        _rms_matmul_kernel,
        out_shape=jax.ShapeDtypeStruct((n, dout), F32),
        grid=(n // tm, dout // tn),
        in_specs=[
            pl.BlockSpec((tm, d), lambda i, j: (i, 0)),
            pl.BlockSpec((1, d), lambda i, j: (0, 0)),
            pl.BlockSpec((None, tn, d), lambda i, j: (layer, j, 0)),
        ],
        out_specs=pl.BlockSpec((tm, tn), lambda i, j: (i, j)),
        scratch_shapes=[pltpu.VMEM((tm, d), BF16)],
        compiler_params=_cparams("arbitrary", "arbitrary"),
        name="rms_in_proj",
    )(x, g, w_t)


def _ssd_kernel(xbc_ref, halo_ref, z_ref, dt_ref, cw_ref, cb_ref, dtb_ref, alog_ref,
                dx_ref, ng_ref, ex_ref, o_ref, h_ref, y_ref):
    c = pl.program_id(1)
    L = SSD_CHUNK

    @pl.when(c == 0)
    def _():
        h_ref[...] = jnp.zeros(h_ref.shape, F32)

    prev = jnp.where(c == 0, 0.0, halo_ref[...])
    xbc = _causal_conv(xbc_ref[...], prev, cw_ref[...]) + cb_ref[...]
    xbc = xbc * _sigmoid(xbc)
    xs = xbc[:, :D_SSD]

    row = lax.broadcasted_iota(jnp.int32, (L, L), 0)
    col = lax.broadcasted_iota(jnp.int32, (L, L), 1)
    tril = (row >= col).astype(F32)

    def softplus(v):
        return jnp.maximum(v, 0.0) + jnp.log1p(jnp.exp(-jnp.abs(v)))

    dt_s = softplus(dt_ref[...] + dtb_ref[...])
    a_s = dt_s * (-jnp.exp(alog_ref[...]))
    acs_s = _dot_exact(tril, a_s)
    acs_t = acs_s.T
    ex = ex_ref[...]
    dt_x = _expand_exact(dt_s, ex)
    e_x = _expand_exact(jnp.exp(acs_s), ex)
    de_x = _expand_exact(jnp.exp(acs_s[L - 1:L, :] - acs_s), ex)
    decay = e_x[L - 1:L, :]
    xdt = xs * dt_x
    xdt_end = (xdt * de_x).astype(BF16)
    xdt = xdt.astype(BF16)

    lane = lax.broadcasted_iota(jnp.int32, (L, LANES), 1)
    lo = lane < SSD_HEAD_DIM
    lower = row >= col

    for g in range(SSD_GROUPS):
        bm = xbc[:, D_SSD + g * SSD_STATE:D_SSD + (g + 1) * SSD_STATE].astype(BF16)
        cm = xbc[:, D_SSD + (SSD_GROUPS + g) * SSD_STATE:
                 D_SSD + (SSD_GROUPS + g + 1) * SSD_STATE].astype(BF16)
        cb = _dot_nt(cm, bm)
        bm_t = bm.T
        for q in range(4):
            pair = g * 4 + q
            sl = slice(pair * LANES, (pair + 1) * LANES)
            gs = []
            for hh in range(2):
                h = 2 * pair + hh
                seg = (jnp.broadcast_to(acs_s[:, h:h + 1], (L, L))
                       - jnp.broadcast_to(acs_t[h:h + 1, :], (L, L)))
                lm = jnp.where(lower, jnp.exp(seg), 0.0)
                gs.append((cb * lm).astype(BF16))
            gpair = jnp.concatenate(gs, axis=1)
            xp = xdt[:, sl]
            zero = jnp.zeros_like(xp)
            x2 = jnp.concatenate([jnp.where(lo, xp, zero), jnp.where(lo, zero, xp)], axis=0)
            y = _dot(gpair, x2)
            hprev = h_ref[pair]
            y = y + e_x[:, sl] * _dot(cm, hprev.astype(BF16))
            y = y + xs[:, sl] * dx_ref[:, sl]
            h_ref[pair] = hprev * decay[:, sl] + _dot(bm_t, xdt_end[:, sl])
            y_ref[:, sl] = y

    z = z_ref[...]
    y = y_ref[...] * (z * _sigmoid(z))
    half = D_SSD // SSD_GROUPS
    outs = []
    for g in range(SSD_GROUPS):
        yg = y[:, g * half:(g + 1) * half]
        ms = jnp.mean(yg * yg, axis=-1, keepdims=True)
        outs.append(yg * lax.rsqrt(ms + NORM_EPS))
    o_ref[...] = (jnp.concatenate(outs, axis=1) * ng_ref[...]).astype(BF16)


def _ssd(proj, conv_w, conv_b, dt_bias, a_log, d_skip, norm_g, *, bsz, seq):
    nc = seq // SSD_CHUNK
    n = bsz * seq
    pad = LANES - SSD_HEADS
    dtb = jnp.pad(dt_bias, (0, pad)).reshape(1, LANES)
    alog = jnp.pad(a_log, (0, pad), constant_values=NEG_BIG).reshape(1, LANES)
    rep = lambda v: jnp.repeat(v, SSD_HEAD_DIM).reshape(1, D_SSD)
    expand = (jnp.arange(LANES)[:, None] == (jnp.arange(D_SSD)[None, :] // SSD_HEAD_DIM)).astype(BF16)
    expand = jnp.concatenate([expand] * 3, axis=0)
    rows_per_halo = SSD_CHUNK // SUBLANES
    const = lambda shape: pl.BlockSpec(shape, lambda b, c: (0,) * len(shape))
    return pl.pallas_call(
        _ssd_kernel,
        out_shape=jax.ShapeDtypeStruct((n, D_SSD), BF16),
        grid=(bsz, nc),
        in_specs=[
            pl.BlockSpec((SSD_CHUNK, SSD_CONV_DIM), lambda b, c: (b * nc + c, COL_XBC // SSD_CONV_DIM)),
            pl.BlockSpec((SUBLANES, SSD_CONV_DIM),
                         lambda b, c: (jnp.maximum((b * nc + c) * rows_per_halo - 1, 0), 0)),
            pl.BlockSpec((SSD_CHUNK, D_SSD), lambda b, c: (b * nc + c, COL_Z // D_SSD)),
            pl.BlockSpec((SSD_CHUNK, LANES), lambda b, c: (b * nc + c, COL_DT // LANES)),
            const((SSD_CONV, SSD_CONV_DIM)),
            const((1, SSD_CONV_DIM)),
            const((1, LANES)),
            const((1, LANES)),
            const((1, D_SSD)),
            const((1, D_SSD)),
            const((3 * LANES, D_SSD)),
        ],
        out_specs=pl.BlockSpec((SSD_CHUNK, D_SSD), lambda b, c: (b * nc + c, 0)),
        scratch_shapes=[pltpu.VMEM((SSD_HEADS // 2, SSD_STATE, LANES), F32),
                        pltpu.VMEM((SSD_CHUNK, D_SSD), F32)],
        compiler_params=_cparams("arbitrary", "arbitrary"),
        name="ssd_scan",
    )(proj, proj, proj, proj, conv_w, conv_b.reshape(1, -1), dtb, alog,
      rep(d_skip), norm_g.reshape(1, -1), expand)


def _dsa_prep_kernel(q_ref, k_ref, v_ref, qi_ref, ki_ref, qg_ref, kg_ref, seg_ref,
                     qn_ref, kn_ref, vt_ref, qib_ref, kib_ref, *, blk):
    def head_norm(x, g):
        ms = _expand_exact(x * x, seg_ref[...]) * (1.0 / ATT_HEAD_DIM)
        return x * lax.rsqrt(ms + NORM_EPS) * g

    qn_ref[...] = (head_norm(q_ref[...], qg_ref[...]) * (LOG2E * ATT_HEAD_DIM ** -0.5)).astype(BF16)
    kn_ref[...] = head_norm(k_ref[...], kg_ref[...]).astype(BF16)
    for c in range(vt_ref.shape[0]):
        vt_ref[c] = v_ref[c * blk:(c + 1) * blk, :].T.astype(BF16)
    qib_ref[...] = qi_ref[...].astype(BF16)
    kib_ref[...] = ki_ref[...].astype(BF16)


def _dsa_prep(proj, q_norm, k_norm, *, tm, blk):
    n = proj.shape[0]
    seg = (jnp.arange(D_ATT)[:, None] // ATT_HEAD_DIM
           == jnp.arange(D_ATT)[None, :] // ATT_HEAD_DIM).astype(BF16)
    seg = jnp.concatenate([seg] * 3, axis=0)
    tile = lambda v: jnp.tile(v, ATT_HEADS).reshape(1, D_ATT)
    col_blk = lambda col, w: pl.BlockSpec((tm, w), lambda i: (i, col // w))
    const = lambda shape: pl.BlockSpec(shape, lambda i: (0,) * len(shape))
    return pl.pallas_call(
        functools.partial(_dsa_prep_kernel, blk=blk),
        out_shape=(
            jax.ShapeDtypeStruct((n, D_ATT), BF16),
            jax.ShapeDtypeStruct((n, D_ATT), BF16),
            jax.ShapeDtypeStruct((n // blk, D_ATT, blk), BF16),
            jax.ShapeDtypeStruct((n, D_ATT), BF16),
            jax.ShapeDtypeStruct((n, LANES), BF16),
        ),
        grid=(n // tm,),
        in_specs=[col_blk(COL_Q, D_ATT), col_blk(COL_K, D_ATT), col_blk(COL_V, D_ATT),
                  col_blk(COL_QIDX, D_ATT), col_blk(COL_KIDX, LANES),
                  const((1, D_ATT)), const((1, D_ATT)), const((3 * D_ATT, D_ATT))],
        out_specs=(
            pl.BlockSpec((tm, D_ATT), lambda i: (i, 0)),
            pl.BlockSpec((tm, D_ATT), lambda i: (i, 0)),
            pl.BlockSpec((tm // blk, D_ATT, blk), lambda i: (i, 0, 0)),
            pl.BlockSpec((tm, D_ATT), lambda i: (i, 0)),
            pl.BlockSpec((tm, LANES), lambda i: (i, 0)),
        ),
        compiler_params=_cparams("arbitrary"),
        name="dsa_prep",
    )(proj, proj, proj, proj, proj, tile(q_norm), tile(k_norm), seg)


def _dsa_kernel(qn_ref, qi_ref, w_ref, kn_ref, vt_ref, ki_ref, bias_ref, o_ref,
                key_ref, hi_ref, lo_ref, big_ref, qm_ref, qim_ref, wt_ref, mb_ref, carry_ref,
                m_ref, l_ref, acc_ref, *, blk, topk):
    i = pl.program_id(1)
    pack = 16
    lane = lax.broadcasted_iota(jnp.int32, (blk, LANES), 1)
    lo_lanes = lane < ATT_HEAD_DIM
    krow = lax.broadcasted_iota(jnp.int32, (blk, blk), 0)
    qcol = lax.broadcasted_iota(jnp.int32, (blk, blk), 1)
    future = krow > qcol

    for p in range(ATT_HEADS // 2):
        sl = slice(p * LANES, (p + 1) * LANES)
        qp = qn_ref[:, sl]
        qip = qi_ref[:, sl]
        zero = jnp.zeros_like(qp)
        qm_ref[2 * p * blk:(2 * p + 1) * blk] = jnp.where(lo_lanes, qp, zero)
        qm_ref[(2 * p + 1) * blk:(2 * p + 2) * blk] = jnp.where(lo_lanes, zero, qp)
        qim_ref[2 * p * blk:(2 * p + 1) * blk] = jnp.where(lo_lanes, qip, zero)
        qim_ref[(2 * p + 1) * blk:(2 * p + 2) * blk] = jnp.where(lo_lanes, zero, qip)
    wt_ref[...] = w_ref[...].T

    def score_dots(j, slot):
        kk = ki_ref[pl.ds(pl.multiple_of(j * blk, blk), blk), :]
        big_ref[slot] = _dot_nt(kk, qim_ref[...])

    def score_finish(j, slot, diag):
        s = jnp.zeros((blk, blk), F32)
        for h in range(IDX_HEADS):
            s = s + jnp.maximum(big_ref[slot, :, h * blk:(h + 1) * blk], 0.0) * wt_ref[h:h + 1, :]
        if diag:
            s = jnp.where(future, -jnp.inf, s)
        s = jnp.where(s == 0.0, 0.0, s)
        bits = pltpu.bitcast(s, jnp.int32)
        key = bits ^ ((bits >> 31) & 0x7FFFFFFF)
        key_ref[j] = key
        hi_ref[j] = (key >> 16).astype(jnp.int16)
        lo_ref[j] = ((key & 0xFFFF) - 32768).astype(jnp.int16)

    def by_parity(x, fn):
        @pl.when(x % 2 == 0)
        def _():
            fn(0)

        @pl.when(x % 2 == 1)
        def _():
            fn(1)

    score_dots(0, 0)

    def score_block(base, count):
        for u in range(count):
            score_dots(base + u + 1, (u + 1) % 2)
            score_finish(base + u, u % 2, False)

    def score_oct(t, carry):
        score_block(8 * t, 8)
        return carry

    lax.fori_loop(0, i // 8, score_oct, 0)

    @pl.when(i % 8 >= 4)
    def _():
        score_block(8 * (i // 8), 4)

    @pl.when(i % 4 >= 2)
    def _():
        score_block(4 * (i // 4), 2)

    @pl.when(i % 2 == 1)
    def _():
        score_block(i - 1, 1)

    by_parity(i, lambda slot: score_finish(i, slot, True))

    one16 = jnp.ones((), jnp.int16)
    zero16 = jnp.zeros((), jnp.int16)

    def count_ge(src_ref, cand):
        cand_b = jnp.broadcast_to(cand.astype(jnp.int16), (blk, blk))

        def add_tile(j, cnts, on):
            ind = jnp.where(src_ref[j] >= cand_b, on, zero16)
            cnts = list(cnts)
            for g in range(blk // pack):
                cnts[g % len(cnts)] = cnts[g % len(cnts)] + ind[g * pack:(g + 1) * pack]
            return tuple(cnts)

        def body(t, cnts):
            return add_tile(2 * t + 1, add_tile(2 * t, cnts, one16), one16)

        zeros = tuple(jnp.zeros((pack, blk), jnp.int16) for _ in range(4))
        cnts = lax.fori_loop(0, (i + 1) // 2, body, zeros)
        cnts = add_tile(i, cnts, ((i + 1) % 2).astype(jnp.int16))
        cnt = (cnts[0] + cnts[1]) + (cnts[2] + cnts[3])
        return jnp.sum(cnt.astype(jnp.int32).astype(F32), axis=0, keepdims=True)

    def bisect16(src_ref, rank, cnt_all, adaptive=False):
        def bit_body(b, st):
            prefix, c_acc, c_rej = st
            cand = prefix + lax.shift_left(jnp.int32(1), 15 - b)
            cnt = count_ge(src_ref, cand)
            ok = cnt >= rank
            return (jnp.where(ok, cand, prefix), jnp.where(ok, cnt, c_acc), jnp.where(ok, c_rej, cnt))
        init = (jnp.full((1, blk), -32768, jnp.int32), cnt_all, jnp.zeros((1, blk), F32))
        if not adaptive:
            return lax.fori_loop(0, 16, bit_body, init)

        def unsettled(c_acc):
            return jnp.max(c_acc - rank) > 0.5

        def w_cond(st):
            return jnp.logical_and(st[0] < 16, st[1])

        def w_body(st):
            new = bit_body(st[0], st[2:])
            return (st[0] + 1, unsettled(new[1])) + new

        out = lax.while_loop(w_cond, w_body, (jnp.int32(0), unsettled(cnt_all)) + init)
        return out[2:]

    ncols = jnp.full((1, blk), ((i + 1) * blk).astype(F32), F32)
    p_hi, c_acc1, c_rej1 = bisect16(hi_ref, jnp.float32(topk), ncols)
    p_hi_b = jnp.broadcast_to(p_hi.astype(jnp.int16), (blk, blk))

    def group_body(j, carry):
        lo_ref[j] = jnp.where(hi_ref[j] == p_hi_b, lo_ref[j], jnp.int16(-32768))
        return carry

    lax.fori_loop(0, i + 1, group_body, 0)
    rank2 = jnp.float32(topk) - c_rej1
    p_lo, c_acc2, c_rej2 = bisect16(lo_ref, rank2, c_acc1 - c_rej1, adaptive=True)
    thr = lax.shift_left(p_hi, 16) + (p_lo + 32768)
    need = rank2 - c_rej2
    has_ties = jnp.max(c_acc2 - rank2) > 0.5

    m_ref[...] = jnp.full(m_ref.shape, NEG_BIG, F32)
    l_ref[...] = jnp.zeros(l_ref.shape, F32)
    acc_ref[...] = jnp.zeros(acc_ref.shape, F32)
    carry_ref[...] = jnp.zeros(carry_ref.shape, F32)

    def qk_dots(j, slot, mask_slot):
        start = pl.multiple_of(j * blk, blk)
        for p in range(ATT_HEADS // 2):
            mb2 = jnp.concatenate([mb_ref[mask_slot], mb_ref[mask_slot]], axis=1)
            big_ref[slot, :, 2 * p * blk:(2 * p + 2) * blk] = _dot_nt(
                kn_ref[pl.ds(start, blk), p * LANES:(p + 1) * LANES], qm_ref[2 * p * blk:(2 * p + 2) * blk]) + mb2

    def tile_mask(j, kind, slot):
        kt = key_ref[j]

        @pl.when(jnp.logical_not(has_ties))
        def _():
            mb = jnp.where(kt >= thr, 0.0, NEG_BIG)
            mb_ref[slot] = jnp.where(future, NEG_BIG, mb) if kind == 2 else mb

        @pl.when(has_ties)
        def _():
            eq = jnp.where(kt == thr, 1.0, 0.0)
            lower = jnp.where(krow >= qcol, 1.0, 0.0).astype(BF16)
            seen = _dot(lower, eq.astype(BF16)) + carry_ref[0:1, :]
            keep = jnp.where(seen <= need, 0.0, NEG_BIG)
            mb = jnp.where(kt > thr, 0.0, jnp.where(kt == thr, keep, NEG_BIG))
            mb_ref[slot] = jnp.where(future, NEG_BIG, mb) if kind == 2 else mb
            carry_ref[0:1, :] += jnp.sum(eq, axis=0, keepdims=True)

    ones_rows = jnp.ones((pack, blk), BF16)

    def softmax_pv(j, slot, kind):
        vt = vt_ref[j]
        for h in range(ATT_HEADS):
            half = blk // 2
            s = big_ref[slot, :, h * blk:(h + 1) * blk]
            if kind > 0:
                s = s + bias_ref[kind - 1, h]
            m_prev = m_ref[h]
            m_new = jnp.maximum(m_prev, jnp.max(s, axis=0, keepdims=True))
            alpha = jnp.exp2(m_prev - m_new)
            if kind > 0:
                pexp = jnp.exp2(s - m_new[0:1, :]).astype(BF16)
            else:
                pexp = jnp.concatenate(
                    [jnp.exp2(big_ref[slot, r * half:(r + 1) * half, h * blk:(h + 1) * blk]
                              - m_new[0:1, :]).astype(BF16) for r in range(2)], axis=0)
            m_ref[h] = m_new
            ch = slice(h * ATT_HEAD_DIM, (h + 1) * ATT_HEAD_DIM)
            pv = _dot(jnp.concatenate([vt[ch, :], ones_rows], axis=0), pexp)
            l_ref[h] = alpha * l_ref[h] + pv[ATT_HEAD_DIM:ATT_HEAD_DIM + 1, :]
            acc_ref[ch, :] = alpha[0:1, :] * acc_ref[ch, :] + pv[:ATT_HEAD_DIM, :]

    n_far = i - 1

    @pl.when(i == 0)
    def _():
        tile_mask(0, 2, 0)

    @pl.when(i > 0)
    def _():
        tile_mask(0, 0, 0)

    qk_dots(0, 0, 0)

    def attn_block(base, count):
        for u in range(count):
            tile_mask(base + u + 1, 0, (u + 1) % 4)
        for u in range(count):
            qk_dots(base + u + 1, (u + 1) % 2, (u + 1) % 4)
            softmax_pv(base + u, u % 2, 0)

    def attn_quad(t, carry):
        attn_block(4 * t, 4)
        return carry

    lax.fori_loop(0, n_far // 4, attn_quad, 0)

    @pl.when(jnp.logical_and(n_far > 0, n_far % 4 >= 2))
    def _():
        attn_block(4 * (n_far // 4), 2)

    @pl.when(jnp.logical_and(n_far > 0, n_far % 2 == 1))
    def _():
        attn_block(n_far - 1, 1)

    def near_tiles(slot):
        tile_mask(i, 2, 1 - slot)
        qk_dots(i, 1 - slot, 1 - slot)
        softmax_pv(i - 1, slot, 1)
        softmax_pv(i, 1 - slot, 2)

    @pl.when(i > 0)
    def _():
        by_parity(i - 1, near_tiles)

    @pl.when(i == 0)
    def _():
        softmax_pv(0, 0, 2)

    outs = []
    for h in range(ATT_HEADS):
        ch = slice(h * ATT_HEAD_DIM, (h + 1) * ATT_HEAD_DIM)
        outs.append(acc_ref[ch, :] / l_ref[h][0:1, :])
    o_ref[...] = jnp.concatenate(outs, axis=0).T.astype(BF16)


def _t5_bucket(dist):
    n = jnp.maximum(dist, 0)
    max_exact = REL_BUCKETS // 2
    large = max_exact + (jnp.log(jnp.maximum(n, max_exact).astype(F32) / max_exact)
                         / math.log(REL_MAX_DIST / max_exact)
                         * (REL_BUCKETS - max_exact)).astype(jnp.int32)
    large = jnp.minimum(large, REL_BUCKETS - 1)
    return jnp.where(n < max_exact, n, large)


def _bias_tiles(rel_bias, blk):
    width = 2 * blk
    f = ((rel_bias[_t5_bucket(jnp.arange(width, dtype=jnp.int32))] - rel_bias[REL_BUCKETS - 1]) * LOG2E).T
    h_sub = jnp.concatenate([f[:, blk:], f[:, :blk]], axis=1)
    h = jnp.stack([h_sub, f])
    skew = jnp.tile(h, (1, 1, blk))[:, :, :blk * (width - 1)].reshape(2, ATT_HEADS, blk, width - 1)
    return skew[:, :, :, :blk]


def _dsa(qn, qi, proj, kn, vt, ki, bias, *, bsz, seq, blk):
    n = bsz * seq
    nb = seq // blk
    topk = min(TOPK_MAX, seq // 4)
    once = pl.Buffered(1)
    return pl.pallas_call(
        functools.partial(_dsa_kernel, blk=blk, topk=topk),
        out_shape=jax.ShapeDtypeStruct((n, D_ATT), BF16),
        grid=(bsz, nb),
        in_specs=[
            pl.BlockSpec((blk, D_ATT), lambda b, i: (b * nb + i, 0)),
            pl.BlockSpec((blk, D_ATT), lambda b, i: (b * nb + i, 0)),
            pl.BlockSpec((blk, LANES), lambda b, i: (b * nb + i, COL_WIDX // LANES)),
            pl.BlockSpec((seq, D_ATT), lambda b, i: (b, 0), pipeline_mode=once),
            pl.BlockSpec((nb, D_ATT, blk), lambda b, i: (b, 0, 0), pipeline_mode=once),
            pl.BlockSpec((seq, LANES), lambda b, i: (b, 0), pipeline_mode=once),
            pl.BlockSpec((2, ATT_HEADS, blk, blk), lambda b, i: (0, 0, 0, 0), pipeline_mode=once),
        ],
        out_specs=pl.BlockSpec((blk, D_ATT), lambda b, i: (b * nb + i, 0)),
        scratch_shapes=[
            pltpu.VMEM((nb, blk, blk), jnp.int32),
            pltpu.VMEM((nb, blk, blk), jnp.int16),
            pltpu.VMEM((nb, blk, blk), jnp.int16),
            pltpu.VMEM((2, blk, ATT_HEADS * blk), F32),
            pltpu.VMEM((ATT_HEADS * blk, LANES), BF16),
            pltpu.VMEM((IDX_HEADS * blk, LANES), BF16),
            pltpu.VMEM((LANES, blk), F32),
            pltpu.VMEM((4, blk, blk), F32),
            pltpu.VMEM((SUBLANES, blk), F32),
            pltpu.VMEM((ATT_HEADS, SUBLANES, blk), F32),
            pltpu.VMEM((ATT_HEADS, SUBLANES, blk), F32),
            pltpu.VMEM((D_ATT, blk), F32),
        ],
        compiler_params=_cparams("arbitrary", "arbitrary"),
        name="dsa_attention",
    )(qn, qi, proj, kn, vt, ki, bias)


def _outproj_kernel(x_ref, yssd_ref, yatt_ref, scb_ref, scc_ref, sch_ref, hc_ref, hh_ref,
                    cw_ref, w_ref, o_ref, cat_ref, *, blocks_per_seq):
    i = pl.program_id(0)

    @pl.when(pl.program_id(1) == 0)
    def _():
        prev = jnp.where(i % blocks_per_seq == 0, 0.0, hc_ref[...] * hh_ref[...])
        ysc = scb_ref[...] * _causal_conv(scc_ref[...] * sch_ref[...], prev, cw_ref[...])
        cat_ref[:, :D_SSD] = yssd_ref[...]
        cat_ref[:, D_SSD:D_SSD + D_ATT] = yatt_ref[...]
        cat_ref[:, D_SSD + D_ATT:] = ysc.astype(BF16)

    o_ref[...] = x_ref[...] + _dot(cat_ref[...], w_ref[...])


def _outproj(x, y_ssd, y_att, proj, sc_w, w_out, layer, *, seq, tm, tn):
    n, d = x.shape
    halo = tm // SUBLANES
    blk = lambda col: pl.BlockSpec((tm, D_SC), lambda i, j: (i, col // D_SC))
    hblk = lambda col: pl.BlockSpec(
        (SUBLANES, D_SC), lambda i, j: (jnp.maximum(i * halo - 1, 0), col // D_SC))
    return pl.pallas_call(
        functools.partial(_outproj_kernel, blocks_per_seq=seq // tm),
        out_shape=jax.ShapeDtypeStruct((n, d), F32),
        grid=(n // tm, d // tn),
        in_specs=[
            pl.BlockSpec((tm, tn), lambda i, j: (i, j)),
            pl.BlockSpec((tm, D_SSD), lambda i, j: (i, 0)),
            pl.BlockSpec((tm, D_ATT), lambda i, j: (i, 0)),
            blk(COL_SCB), blk(COL_SCC), blk(COL_SCH), hblk(COL_SCC), hblk(COL_SCH),
            pl.BlockSpec((SC_CONV, D_SC), lambda i, j: (0, 0)),
            pl.BlockSpec((None, d, tn), lambda i, j: (layer, 0, j)),
        ],
        out_specs=pl.BlockSpec((tm, tn), lambda i, j: (i, j)),
        scratch_shapes=[pltpu.VMEM((tm, d), BF16)],
        compiler_params=_cparams("arbitrary", "arbitrary"),
        name="out_proj",
    )(x, y_ssd, y_att, proj, proj, proj, proj, proj, sc_w, w_out)


def _ffn_kernel(x_ref, g_ref, wg_ref, wu_ref, cw_ref, wd_ref, o_ref, h_ref, acc_ref, halo_ref,
                *, blocks_per_seq):
    i = pl.program_id(0)
    j = pl.program_id(1)
    last = pl.num_programs(1) - 1
    tm = x_ref.shape[0]

    @pl.when(i % blocks_per_seq == 0)
    def _():
        halo_ref[j] = jnp.zeros(halo_ref.shape[1:], F32)

    def step(first, final):
        if first:
            x = x_ref[...]
            ms = jnp.mean(x * x, axis=-1, keepdims=True)
            h = (x * lax.rsqrt(ms + NORM_EPS) * g_ref[...]).astype(BF16)
            h_ref[...] = h
        else:
            h = h_ref[...]
        gate = _dot(h, wg_ref[...])
        up = _dot(h, wu_ref[...])
        prev = halo_ref[j]
        halo_ref[j] = gate[tm - SUBLANES:, :]
        gc = _causal_conv(gate, prev, cw_ref[...])
        act = (gc * _sigmoid(gc) * up).astype(BF16)
        down = _dot(act, wd_ref[...])
        if first:
            acc_ref[...] = down
        elif final:
            o_ref[...] = x_ref[...] + (acc_ref[...] + down)
        else:
            acc_ref[...] += down

    @pl.when(j == 0)
    def _():
        step(True, False)

    @pl.when(jnp.logical_and(j > 0, j < last))
    def _():
        step(False, False)

    @pl.when(j == last)
    def _():
        step(False, True)


def _ffn(x, g, wg, wu, cw, wd, layer, *, seq, tm, tf):
    n, d = x.shape
    dff = wg.shape[2]
    assert dff // tf >= 2, "the kernel has distinct first and last column steps"
    return pl.pallas_call(
        functools.partial(_ffn_kernel, blocks_per_seq=seq // tm),
        out_shape=jax.ShapeDtypeStruct((n, d), F32),
        grid=(n // tm, dff // tf),
        in_specs=[
            pl.BlockSpec((tm, d), lambda i, j: (i, 0)),
            pl.BlockSpec((1, d), lambda i, j: (0, 0)),
            pl.BlockSpec((None, d, tf), lambda i, j: (layer, 0, j)),
            pl.BlockSpec((None, d, tf), lambda i, j: (layer, 0, j)),
            pl.BlockSpec((FFN_CONV, tf), lambda i, j: (0, j)),
            pl.BlockSpec((None, tf, d), lambda i, j: (layer, j, 0)),
        ],
        out_specs=pl.BlockSpec((tm, d), lambda i, j: (i, 0)),
        scratch_shapes=[
            pltpu.VMEM((tm, d), BF16),
            pltpu.VMEM((tm, d), F32),
            pltpu.VMEM((dff // tf, SUBLANES, tf), F32),
        ],
        compiler_params=_cparams("arbitrary", "arbitrary"),
        name="conv_gated_mlp",
    )(x, g, wg, wu, cw, wd)


_IN_SIZES = (D_SSD, SSD_CONV_DIM, SSD_HEADS, D_ATT, D_ATT, D_ATT,
             IDX_HEADS * IDX_DIM, IDX_DIM, IDX_HEADS, D_SC, D_SC, D_SC)
_IN_OFFS = tuple(sum(_IN_SIZES[:t]) for t in range(len(_IN_SIZES) + 1))
D_IN_PROJ = _IN_OFFS[-1]
_WIDE_SEGMENTS = ((COL_XBC, 1), (COL_Q, 3), (COL_Z, 0), (COL_K, 4), (COL_V, 5), (COL_QIDX, 6),
                  (COL_SCB, 9), (COL_SCC, 10), (COL_SCH, 11))


def _pack_w_in_kernel(w_ref, o_ref):
    tc = o_ref.shape[1]
    pack = 16

    def rows(seg):
        return w_ref[_IN_OFFS[seg]:_IN_OFFS[seg + 1], :]

    def put(dst, val, total):
        n = val.shape[0]
        pad = (-n) % pack
        if pad:
            val = jnp.concatenate([val, jnp.zeros((pad, tc), F32)], axis=0)
        o_ref[dst:dst + n + pad, :] = val.astype(BF16)
        if total > n + pad:
            o_ref[dst + n + pad:dst + total, :] = jnp.zeros((total - n - pad, tc), BF16)

    for dst, seg in _WIDE_SEGMENTS:
        put(dst, rows(seg), _IN_SIZES[seg])
    put(COL_DT, rows(2), LANES)
    put(COL_KIDX, jnp.concatenate([rows(7)] * (LANES // IDX_DIM), axis=0), LANES)
    put(COL_WIDX, rows(8), D_PACKED - COL_WIDX)


def _pack_w_in(w_in, *, tc=256):
    depth, d, _ = w_in.shape
    w_t = jnp.swapaxes(w_in, 1, 2)
    return pl.pallas_call(
        _pack_w_in_kernel,
        out_shape=jax.ShapeDtypeStruct((depth, D_PACKED, d), BF16),
        grid=(depth, d // tc),
        in_specs=[pl.BlockSpec((None, D_IN_PROJ, tc), lambda l, i: (l, 0, i))],
        out_specs=pl.BlockSpec((None, D_PACKED, tc), lambda l, i: (l, 0, i)),
        compiler_params=_cparams("arbitrary", "arbitrary"),
        name="pack_w_in",
    )(w_t)


def _forward(x, norm_mix, w_in, ssd_conv_w, ssd_conv_b, ssd_dt_bias, ssd_a_log, ssd_d, ssd_norm,
             att_q_norm, att_k_norm, rel_bias, sc_conv_w, w_out, norm_ffn,
             ffn_w_gate, ffn_w_up, ffn_conv_w, ffn_w_down, *, tm_proj, tn_proj, tm_out, tn_out,
             tm_ffn, tf_ffn, tm_prep, att_blk):
    bsz, seq, d = x.shape
    depth = w_in.shape[0]
    n = bsz * seq
    xf = x.reshape(n, d)
    w_in_p = _pack_w_in(w_in)
    w_out_b = w_out.astype(BF16)
    wg_b = ffn_w_gate.astype(BF16)
    wu_b = ffn_w_up.astype(BF16)
    wd_b = ffn_w_down.astype(BF16)
    bias = _bias_tiles(rel_bias, att_blk)
    for l in range(depth):
        proj = _rms_matmul(xf, norm_mix[l].reshape(1, d), w_in_p, l, tm=tm_proj, tn=tn_proj)
        y_ssd = _ssd(proj, ssd_conv_w[l], ssd_conv_b[l], ssd_dt_bias[l], ssd_a_log[l], ssd_d[l],
                     ssd_norm[l], bsz=bsz, seq=seq)
        qn, kn, vt, qi, ki = _dsa_prep(proj, att_q_norm[l], att_k_norm[l], tm=tm_prep, blk=att_blk)
        y_att = _dsa(qn, qi, proj, kn, vt, ki, bias, bsz=bsz, seq=seq, blk=att_blk)
        xf = _outproj(xf, y_ssd, y_att, proj, sc_conv_w[l], w_out_b, l, seq=seq, tm=tm_out, tn=tn_out)
        xf = _ffn(xf, norm_ffn[l].reshape(1, d), wg_b, wu_b, ffn_conv_w[l], wd_b, l,
                  seq=seq, tm=tm_ffn, tf=tf_ffn)
    return xf.reshape(bsz, seq, d)


def kernel(x, norm_mix, w_in, ssd_conv_w, ssd_conv_b, ssd_dt_bias, ssd_a_log, ssd_d, ssd_norm, att_q_norm, att_k_norm, rel_bias, sc_conv_w, w_out, norm_ffn, ffn_w_gate, ffn_w_up, ffn_conv_w, ffn_w_down):
    return _forward(x, norm_mix, w_in, ssd_conv_w, ssd_conv_b, ssd_dt_bias, ssd_a_log, ssd_d, ssd_norm,
                    att_q_norm, att_k_norm, rel_bias, sc_conv_w, w_out, norm_ffn,
                    ffn_w_gate, ffn_w_up, ffn_conv_w, ffn_w_down,
                    tm_proj=1024, tn_proj=512, tm_out=512, tn_out=2048,
                    tm_ffn=512, tf_ffn=512, tm_prep=1024, att_blk=256)
```
